```python
import jax, jax.numpy as jnp
from jax import lax
import numpy as np

D_MODEL = 1024
BATCH = 1
SEQ = 16384
DEPTH = 4
DEC_BATCH = 32
DEC_SEQ = 32
PAST_LEN = 2048

CHUNK = 64
D_FF = 4 * D_MODEL
N_AB_LAYERS = (DEPTH + 1) // 2
N_C_LAYERS = DEPTH // 2
RET_HEADS = 4
RET_DK = D_MODEL // 16
RET_DV = D_MODEL // 8
ROPE_BASE = 10000.0
HG_HEADS = 4
HG_DK = D_MODEL // 8
HG_DV = D_MODEL // 8
AB_IN = 2 * RET_HEADS * RET_DK + 2 * RET_HEADS * RET_DV + 2 * HG_HEADS * HG_DK + 2 * HG_HEADS * HG_DV
AB_OUT = RET_HEADS * RET_DV + HG_HEADS * HG_DV
RW_N = 64
RW_HEADS = D_MODEL // RW_N
RW_DECAY_LORA = 64
RW_AAA_LORA = 64
RW_MV_LORA = 32
RW_GATE_LORA = 160
NORM_EPS = 1e-6
RW_LN_EPS = 64e-5

kernel_name = 'retention_hgrn2_rwkv7_adaln_stream_step'


def _rmsnorm(x, g):
    xf = x.astype(jnp.float32)
    y = xf * lax.rsqrt(jnp.mean(xf * xf, -1, keepdims=True) + NORM_EPS)
    return (y * g.astype(jnp.float32)).astype(x.dtype)


def _head_rms(o):
    return o * lax.rsqrt(jnp.mean(o * o, -1, keepdims=True) + NORM_EPS)


def _rotary(x, pos):
    half = x.shape[-1] // 2
    inv = ROPE_BASE ** (-jnp.arange(half, dtype=jnp.float32) / half)
    ang = pos[:, None] * inv[None, :]
    cos = jnp.cos(ang)[None, :, None, :]
    sin = jnp.sin(ang)[None, :, None, :]
    x1, x2 = x[..., :half], x[..., half:]
    return jnp.concatenate([x1 * cos - x2 * sin, x1 * sin + x2 * cos], -1)


def _chunked_scan(step, s0, *seqs):
    b, t = seqs[0].shape[:2]
    l = min(t, CHUNK)
    n = t // l
    blocks = tuple(jnp.moveaxis(a.reshape((b, n, l) + a.shape[2:]), 1, 0) for a in seqs)
    s, out = lax.scan(lambda s, xs: step(s, *xs), s0, blocks)
    out = jnp.moveaxis(out, 0, 1)
    return out.reshape((b, t) + out.shape[3:]), s


def _retention_step(s, q, k, v):
    l = q.shape[1]
    log_g = jnp.log1p(-jnp.exp2(-5.0 - jnp.arange(RET_HEADS, dtype=jnp.float32)))
    idx = jnp.arange(l, dtype=jnp.float32)
    rel = idx[:, None] - idx[None, :]
    dmat = jnp.exp(jnp.where(rel[None] >= 0, rel[None] * log_g[:, None, None], -jnp.inf))
    scores = jnp.einsum('bihd,bjhd->bhij', q, k) * dmat
    inner = jnp.einsum('bhij,bjhe->bihe', scores, v)
    cross = jnp.einsum('bihd,bhde->bihe', q, s) * jnp.exp((idx[:, None] + 1.0) * log_g[None, :])[None, :, :, None]
    k_dec = k * jnp.exp((l - 1.0 - idx)[:, None] * log_g[None, :])[None, :, :, None]
    s_new = jnp.exp(l * log_g)[None, :, None, None] * s + jnp.einsum('bjhd,bjhe->bhde', k_dec, v)
    return s_new, inner + cross


def _hgrn2_step(s, q, k, v, log_f):
    l = q.shape[1]
    b = jnp.cumsum(log_f, axis=1)
    causal = jnp.tril(jnp.ones((l, l), dtype=bool))
    diff = b[:, :, None] - b[:, None, :]
    decay = jnp.exp(jnp.where(causal[None, :, :, None, None], diff, -jnp.inf))
    attn = jnp.einsum('bihc,bjhc,bijhc->bhij', q, k, decay)
    inner = jnp.einsum('bhij,bjhe->bihe', attn, v)
    cross = jnp.einsum('bihc,bhce->bihe', q * jnp.exp(b), s)
    b_last = b[:, -1:]
    s_new = jnp.exp(b_last[:, 0])[..., None] * s + jnp.einsum('bjhc,bjhe->bhce', k * jnp.exp(b_last - b), v)
    return s_new, inner + cross


def _mix_ab(h, pos, s_ret, s_hg, w_in, lb, hg_g, w_out):
    f32 = jnp.float32
    bsz, t, _ = h.shape
    z = (h @ w_in).astype(f32)
    widths = [RET_HEADS * RET_DK] * 2 + [RET_HEADS * RET_DV] * 2 + [HG_HEADS * HG_DK] * 2 + [HG_HEADS * HG_DV] * 2
    cuts = [int(c) for c in np.cumsum(widths)[:-1]]
    q_a, k_a, v_a, g_a, q_b, f_b, i_b, g_b = jnp.split(z, cuts, axis=-1)
    q = _rotary(q_a.reshape(bsz, t, RET_HEADS, RET_DK), pos) * (RET_DK ** -0.5)
    k = _rotary(k_a.reshape(bsz, t, RET_HEADS, RET_DK), pos)
    v = v_a.reshape(bsz, t, RET_HEADS, RET_DV)
    o_a, s_ret = _chunked_scan(_retention_step, s_ret.astype(f32), q, k, v)
    o_a = _head_rms(o_a).reshape(bsz, t, -1) * jax.nn.silu(g_a)
    lb = lb.reshape(HG_HEADS, HG_DK)
    zf = f_b.reshape(bsz, t, HG_HEADS, HG_DK)
    log_f = jnp.logaddexp(jnp.log(lb), jnp.log1p(-lb) + jax.nn.log_sigmoid(zf))
    k_b = (1.0 - lb) * jax.nn.sigmoid(-zf)
    q_h = jax.nn.silu(q_b).reshape(bsz, t, HG_HEADS, HG_DK)
    v_h = i_b.reshape(bsz, t, HG_HEADS, HG_DV)
    o_b, s_hg = _chunked_scan(_hgrn2_step, s_hg.astype(f32), q_h, k_b, v_h, log_f)
    o_b = (_head_rms(o_b) * hg_g.astype(f32)).reshape(bsz, t, -1) * jax.nn.sigmoid(g_b)
    out = jnp.concatenate([o_a, o_b], -1).astype(h.dtype) @ w_out
    return out, s_ret, s_hg


def _mix_rwkv(h, shift, s_wkv, v_first, vres, mu, w_rkv, w0, w1, w2, a0, a1, a2, g1, g2, k_k, k_a, r_k, ln_g, ln_b, w_out):
    f32 = jnp.float32
    bsz, t, d = h.shape
    prev = jnp.concatenate([shift[:, None].astype(h.dtype), h[:, :-1]], axis=1)
    xx = prev - h
    xr, xw, xk, xv, xa, xg = [h + xx * mu[i] for i in range(6)]
    r = (xr @ w_rkv[0]).astype(f32)
    k = (xk @ w_rkv[1]).astype(f32)
    v = (xv @ w_rkv[2]).astype(f32)
    if vres is None:
        v_first = v
    else:
        v0, v1, v2 = vres
        v = v + (v_first - v) * jax.nn.sigmoid((v0 + (xv @ v1) @ v2).astype(f32))
    w = -jax.nn.softplus(-(w0 + jnp.tanh(xw @ w1) @ w2).astype(f32)) - 0.5
    decay = jnp.exp(-jnp.exp(w))
    a = jax.nn.sigmoid((a0 + (xa @ a1) @ a2).astype(f32))
    g = jax.nn.sigmoid(xg @ g1) @ g2
    heads = lambda u: u.reshape(bsz, t, RW_HEADS, RW_N)
    kk = heads(k * k_k)
    kk = kk / jnp.maximum(jnp.sqrt(jnp.sum(kk * kk, -1, keepdims=True)), 1e-12)
    k_h = heads(k * (1.0 + (a - 1.0) * k_a))
    r_h, v_h, w_h, a_h = heads(r), heads(v), heads(decay), heads(a)

    def step(s, inp):
        r_t, w_t, k_t, v_t, kk_t, a_t = inp
        sa = jnp.einsum('bhij,bhj->bhi', s, -kk_t)
        s = s * w_t[:, :, None, :] + sa[..., None] * (kk_t * a_t)[:, :, None, :] + v_t[..., None] * k_t[:, :, None, :]
        return s, jnp.einsum('bhij,bhj->bhi', s, r_t)

    tm = lambda u: jnp.moveaxis(u, 1, 0)
    s_wkv, y = lax.scan(step, s_wkv.astype(f32), tuple(tm(u) for u in (r_h, w_h, k_h, v_h, kk, a_h)))
    y = tm(y)
    mean = jnp.mean(y, -1, keepdims=True)
    var = jnp.mean(jnp.square(y - mean), -1, keepdims=True)
    y = ((y - mean) * lax.rsqrt(var + RW_LN_EPS)).reshape(bsz, t, d) * ln_g + ln_b
    y = y + (jnp.sum(r_h * k_h * r_k, -1, keepdims=True) * v_h).reshape(bsz, t, d)
    out = (y * g).astype(h.dtype) @ w_out
    return out, s_wkv, h[:, -1], v_first


def _trunk(x, c, pos, s_ret, s_hg, s_wkv, s_shift, W):
    lb_all = jnp.cumsum(jax.nn.softmax(W['hg_lb'].astype(jnp.float32), axis=0), axis=0)
    lb_all = lb_all - lb_all[:1]
    cond = jax.nn.silu(c)
    new_ret, new_hg, new_wkv, new_shift = [], [], [], []
    v_first = None
    for layer in range(DEPTH):
        mod = cond @ W['mod_w'][layer] + W['mod_b'][layer]
        sh1, sc1, gt1, sh2, sc2, gt2 = [u[:, None] for u in jnp.split(mod, 6, axis=-1)]
        hmix = _rmsnorm(x, W['norm_mix_g'][layer]) * (1.0 + sc1) + sh1
        m = layer // 2
        if layer % 2 == 0:
            out, r_s, h_s = _mix_ab(hmix, pos, s_ret[m], s_hg[m], W['ab_w_in'][m], lb_all[m], W['hg_norm_g'][m], W['ab_w_out'][m])
            new_ret.append(r_s)
            new_hg.append(h_s)
        else:
            vres = None if m == 0 else (W['rw_v0'][m - 1], W['rw_v1'][m - 1], W['rw_v2'][m - 1])
            out, w_s, sh_s, v_first = _mix_rwkv(hmix, s_shift[m], s_wkv[m], v_first, vres, W['rw_mu'][m], W['rw_w_rkv'][m], W['rw_w0'][m], W['rw_w1'][m], W['rw_w2'][m], W['rw_a0'][m], W['rw_a1'][m], W['rw_a2'][m], W['rw_g1'][m], W['rw_g2'][m], W['rw_k_k'][m], W['rw_k_a'][m], W['rw_r_k'][m], W['rw_ln_g'][m], W['rw_ln_b'][m], W['rw_w_out'][m])
            new_wkv.append(w_s)
            new_shift.append(sh_s)
        x = x + gt1 * out
        hmlp = _rmsnorm(x, W['norm_mlp_g'][layer]) * (1.0 + sc2) + sh2
        x = x + gt2 * (jnp.square(jax.nn.relu(hmlp @ W['mlp_w1'][layer])) @ W['mlp_w2'][layer])
    y = _rmsnorm(x, W['final_g'])
    return (y, jnp.stack(new_ret).astype(s_ret.dtype), jnp.stack(new_hg).astype(s_hg.dtype),
            jnp.stack(new_wkv).astype(s_wkv.dtype), jnp.stack(new_shift).astype(s_shift.dtype))


def setup_inputs(seed: int = 0) -> dict:
    key = jax.random.key(seed)
    ks = iter(jax.random.split(key, 64))
    f32 = jnp.float32

    def nrm(shape, scale):
        return scale * jax.random.normal(next(ks), shape, f32)

    def uni(shape, lo, hi):
        return jax.random.uniform(next(ks), shape, f32, lo, hi)

    d = D_MODEL
    return {
        'x_prompt': nrm((BATCH, SEQ, d), 1.0),
        'x_sample': nrm((DEC_BATCH, DEC_SEQ, d), 1.0),
        'state_ret': nrm((N_AB_LAYERS, DEC_BATCH, RET_HEADS, RET_DK, RET_DV), 0.5),
        'state_hgrn': nrm((N_AB_LAYERS, DEC_BATCH, HG_HEADS, HG_DK, HG_DV), 0.5),
        'state_wkv': nrm((N_C_LAYERS, DEC_BATCH, RW_HEADS, RW_N, RW_N), 0.3),
        'state_shift': nrm((N_C_LAYERS, DEC_BATCH, d), 1.0),
        'c_prompt': nrm((BATCH, d), 1.0),
        'c_sample': nrm((DEC_BATCH, d), 1.0),
        'mod_w': nrm((DEPTH, d, 6 * d), 0.5 * d ** -0.5),
        'mod_b': nrm((DEPTH, 6 * d), 0.02),
        'norm_mix_g': 1.0 + nrm((DEPTH, d), 0.05),
        'norm_mlp_g': 1.0 + nrm((DEPTH, d), 0.05),
        'final_g': 1.0 + nrm((d,), 0.05),
        'mlp_w1': nrm((DEPTH, d, D_FF), d ** -0.5),
        'mlp_w2': nrm((DEPTH, D_FF, d), D_FF ** -0.5),
        'ab_w_in': nrm((N_AB_LAYERS, d, AB_IN), d ** -0.5),
        'ab_w_out': nrm((N_AB_LAYERS, AB_OUT, d), AB_OUT ** -0.5),
        'hg_lb': nrm((N_AB_LAYERS, HG_HEADS * HG_DK), 1.0),
        'hg_norm_g': 1.0 + nrm((N_AB_LAYERS, HG_DV), 0.05),
        'rw_mu': uni((N_C_LAYERS, 6, d), 0.0, 1.0),
        'rw_w_rkv': nrm((N_C_LAYERS, 3, d, d), d ** -0.5),
        'rw_w0': uni((N_C_LAYERS, d), -4.0, 0.5),
        'rw_w1': nrm((N_C_LAYERS, d, RW_DECAY_LORA), d ** -0.5),
        'rw_w2': nrm((N_C_LAYERS, RW_DECAY_LORA, d), 0.1 * RW_DECAY_LORA ** -0.5),
        'rw_a0': nrm((N_C_LAYERS, d), 0.5),
        'rw_a1': nrm((N_C_LAYERS, d, RW_AAA_LORA), d ** -0.5),
        'rw_a2': nrm((N_C_LAYERS, RW_AAA_LORA, d), 0.1 * RW_AAA_LORA ** -0.5),
        'rw_v0': 1.0 + nrm((N_C_LAYERS - 1, d), 0.1),
        'rw_v1': nrm((N_C_LAYERS - 1, d, RW_MV_LORA), d ** -0.5),
        'rw_v2': nrm((N_C_LAYERS - 1, RW_MV_LORA, d), 0.1 * RW_MV_LORA ** -0.5),
        'rw_g1': nrm((N_C_LAYERS, d, RW_GATE_LORA), d ** -0.5),
        'rw_g2': nrm((N_C_LAYERS, RW_GATE_LORA, d), RW_GATE_LORA ** -0.5),
        'rw_k_k': 0.85 + nrm((N_C_LAYERS, d), 0.05),
        'rw_k_a': 1.0 + nrm((N_C_LAYERS, d), 0.05),
        'rw_r_k': nrm((N_C_LAYERS, RW_HEADS, RW_N), 0.1),
        'rw_ln_g': 1.0 + nrm((N_C_LAYERS, d), 0.05),
        'rw_ln_b': nrm((N_C_LAYERS, d), 0.02),
        'rw_w_out': nrm((N_C_LAYERS, d, d), d ** -0.5),
    }


def reference(x_prompt, x_sample, state_ret, state_hgrn, state_wkv, state_shift, c_prompt, c_sample,
              mod_w, mod_b, norm_mix_g, norm_mlp_g, final_g, mlp_w1, mlp_w2, ab_w_in, ab_w_out, hg_lb, hg_norm_g,
              rw_mu, rw_w_rkv, rw_w0, rw_w1, rw_w2, rw_a0, rw_a1, rw_a2, rw_v0, rw_v1, rw_v2, rw_g1, rw_g2,
              rw_k_k, rw_k_a, rw_r_k, rw_ln_g, rw_ln_b, rw_w_out):
    W = dict(mod_w=mod_w, mod_b=mod_b, norm_mix_g=norm_mix_g, norm_mlp_g=norm_mlp_g, final_g=final_g,
             mlp_w1=mlp_w1, mlp_w2=mlp_w2, ab_w_in=ab_w_in, ab_w_out=ab_w_out, hg_lb=hg_lb, hg_norm_g=hg_norm_g,
             rw_mu=rw_mu, rw_w_rkv=rw_w_rkv, rw_w0=rw_w0, rw_w1=rw_w1, rw_w2=rw_w2, rw_a0=rw_a0, rw_a1=rw_a1,
             rw_a2=rw_a2, rw_v0=rw_v0, rw_v1=rw_v1, rw_v2=rw_v2, rw_g1=rw_g1, rw_g2=rw_g2, rw_k_k=rw_k_k,
             rw_k_a=rw_k_a, rw_r_k=rw_r_k, rw_ln_g=rw_ln_g, rw_ln_b=rw_ln_b, rw_w_out=rw_w_out)
    f32 = jnp.float32
    bp = x_prompt.shape[0]
    dt = x_prompt.dtype
    pos_p = jnp.arange(x_prompt.shape[1], dtype=f32)
    pos_s = PAST_LEN + jnp.arange(x_sample.shape[1], dtype=f32)
    z_ret = jnp.zeros((N_AB_LAYERS, bp, RET_HEADS, RET_DK, RET_DV), dt)
    z_hg = jnp.zeros((N_AB_LAYERS, bp, HG_HEADS, HG_DK, HG_DV), dt)
    z_wkv = jnp.zeros((N_C_LAYERS, bp, RW_HEADS, RW_N, RW_N), dt)
    z_sh = jnp.zeros((N_C_LAYERS, bp, D_MODEL), dt)
    y_p, ret_p, hg_p, wkv_p, sh_p = _trunk(x_prompt, c_prompt, pos_p, z_ret, z_hg, z_wkv, z_sh, W)
    y_s, ret_s, hg_s, wkv_s, sh_s = _trunk(x_sample, c_sample, pos_s, state_ret, state_hgrn, state_wkv, state_shift, W)
    return (y_p, y_s, ret_p, ret_s, hg_p, hg_s, wkv_p, wkv_s, sh_p, sh_s)
```

```python
import functools

import numpy as np
import jax
import jax.numpy as jnp
from jax import lax
from jax.experimental import pallas as pl
from jax.experimental.pallas import tpu as pltpu

F32 = jnp.float32
BF16 = jnp.bfloat16

CHUNK = 64
PAST_LEN = 2048
ROPE_BASE = 10000.0
RET_HEADS = 4
HG_HEADS = 4
RW_N = 64
NORM_EPS = 1e-6
RW_LN_EPS = 64e-5

LANES = 128
SUB = 16
TM = 256
VMEM_LIMIT = 56 * 1024 * 1024


def _sigmoid(x):
    return 1.0 / (1.0 + jnp.exp(-x))


def _split2(x):
    hi = x.astype(BF16)
    lo = (x - hi.astype(F32)).astype(BF16)
    return hi, lo


def _split3(x):
    a = x.astype(BF16)
    r = x - a.astype(F32)
    b = r.astype(BF16)
    c = (r - b.astype(F32)).astype(BF16)
    return a, b, c


_NN = (((1,), (0,)), ((), ()))
_NT = (((1,), (1,)), ((), ()))
_TN = (((0,), (0,)), ((), ()))


def _dg(a, b, dims):
    return lax.dot_general(a, b, dims, preferred_element_type=F32)


def _dot1(a, b, dims=_NN):
    return _dg(a.astype(BF16), b.astype(BF16), dims)


def _dot3(a, b, dims=_NN):
    ah, al = _split2(a)
    bh, bl = _split2(b)
    return _dg(ah, bh, dims) + _dg(ah, bl, dims) + _dg(al, bh, dims)


def _cumsum_rows(tri_bf16, x):
    a, b, c = _split3(x)
    return _dg(tri_bf16, a, _NN) + _dg(tri_bf16, b, _NN) + _dg(tri_bf16, c, _NN)


def _segsum(x, seg_bf16):
    hi, lo = _split2(x)
    return _dg(hi, seg_bf16, _NN) + _dg(lo, seg_bf16, _NN)


def _rms(x, g):
    ms = jnp.mean(x * x, axis=-1, keepdims=True)
    return x * lax.rsqrt(ms + NORM_EPS) * g


def _modulate(y, sc, sh):
    rows, d = y.shape
    ng = sc.shape[0]
    y3 = y.reshape(ng, rows // ng, d)
    return (y3 * (1.0 + sc) + sh).reshape(rows, d)


def _gate_add(x, gt, out):
    rows, d = x.shape
    ng = gt.shape[0]
    return (x.reshape(ng, rows // ng, d) + gt * out.reshape(ng, rows // ng, d)).reshape(rows, d)


def _const_spec(block, index):
    return pl.BlockSpec(block, lambda *_: index, pipeline_mode=pl.Buffered(1))


def _params(sem):
    return pltpu.CompilerParams(dimension_semantics=sem, vmem_limit_bytes=VMEM_LIMIT)


def _mod_kernel(c_ref, w_ref, b_ref, o_ref):
    c = c_ref[...]
    s = c * _sigmoid(c)
    o_ref[0] = _dot1(s, w_ref[0]) + b_ref[0]


def _modulation(c_all, mod_w, mod_b):
    depth, d, d6 = mod_w.shape
    rows = c_all.shape[0]
    nt = d6 // d
    return pl.pallas_call(
        _mod_kernel,
        grid=(depth, nt),
        in_specs=[
            pl.BlockSpec((rows, d), lambda l, j: (0, 0)),
            pl.BlockSpec((1, d, d), lambda l, j: (l, 0, j)),
            pl.BlockSpec((1, 1, d), lambda l, j: (l, 0, j)),
        ],
        out_specs=pl.BlockSpec((1, rows, d), lambda l, j: (l, 0, j)),
        out_shape=jax.ShapeDtypeStruct((depth, rows, d6), F32),
        compiler_params=_params(("arbitrary", "arbitrary")),
    )(c_all, mod_w, mod_b.reshape(depth, 1, d6))


def _ab_in_kernel(x_ref, mod_ref, g_ref, w_ref, cos_ref, sin_ref, lb_ref, o_ref, *, d, dk_scale):
    x = x_ref[...]
    h = _modulate(_rms(x, g_ref[0]), mod_ref[0, :, :, d:2 * d], mod_ref[0, :, :, 0:d])
    hb = h.astype(BF16)
    hw = d // 2

    def proj(a):
        return _dg(hb, w_ref[0, :, a:a + hw], _NN)

    qk = proj(0)
    lane = lax.broadcasted_iota(jnp.int32, (x.shape[0], LANES), 1)
    first_half = (lane % 64) < 32
    cos = cos_ref[...]
    sin = sin_ref[...]
    for s in range(hw // LANES):
        xs = qk[:, LANES * s:LANES * (s + 1)]
        sw = jnp.where(first_half, pltpu.roll(xs, 96, 1), pltpu.roll(xs, 32, 1))
        rot = xs * cos + sw * sin
        if s < hw // (2 * LANES):
            rot = rot * dk_scale
        o_ref[:, LANES * s:LANES * (s + 1)] = rot
    o_ref[:, hw:2 * hw] = proj(hw)
    z = proj(2 * hw)
    o_ref[:, 2 * hw:3 * hw] = z * _sigmoid(z)
    z = proj(3 * hw)
    o_ref[:, 3 * hw:4 * hw] = z * _sigmoid(z)
    z = proj(4 * hw)
    log_lb = lb_ref[0:1, :]
    log1m_lb = lb_ref[1:2, :]
    one_m_lb = lb_ref[2:3, :]
    ls = jnp.minimum(z, 0.0) - jnp.log1p(jnp.exp(-jnp.abs(z)))
    c = log1m_lb + ls
    m = jnp.maximum(log_lb, c)
    o_ref[:, 4 * hw:5 * hw] = m + jnp.log1p(jnp.exp(-jnp.abs(log_lb - c)))
    o_ref[:, 5 * hw:6 * hw] = one_m_lb * (1.0 / (1.0 + jnp.exp(z)))
    o_ref[:, 6 * hw:7 * hw] = proj(5 * hw)
    o_ref[:, 7 * hw:8 * hw] = _sigmoid(proj(6 * hw))


def _ab_in(x, modt, g, w_in, cos_t, sin_t, lbc, layer, m, cfg):
    n, d = x.shape
    ng, npt = cfg["ng"], cfg["npt"]
    nt = n // TM
    zw = 4 * d
    kern = functools.partial(_ab_in_kernel, d=d, dk_scale=float((d // 16) ** -0.5))
    return pl.pallas_call(
        kern,
        grid=(nt,),
        in_specs=[
            pl.BlockSpec((TM, d), lambda i: (i, 0)),
            pl.BlockSpec((1, ng, 1, 6 * d), lambda i: (layer, jnp.maximum(i - npt + 1, 0), 0, 0)),
            _const_spec((1, 1, d), (layer, 0, 0)),
            _const_spec((1, d, w_in.shape[2]), (m, 0, 0)),
            pl.BlockSpec((TM, LANES), lambda i: (i, 0)),
            pl.BlockSpec((TM, LANES), lambda i: (i, 0)),
            _const_spec((8, d // 2), (m, 0)),
        ],
        out_specs=pl.BlockSpec((TM, zw), lambda i: (i, 0)),
        out_shape=jax.ShapeDtypeStruct((n, zw), F32),
        compiler_params=_params(("arbitrary",)),
    )(x, modt, g, w_in, cos_t, sin_t, lbc)


def _ab_rec_kernel(*refs, d, c, sample):
    if sample:
        (z_ref, dmat_ref, rdec_ref, cdec_ref, gc_ref, tri_ref, hgg_ref, ones_ref, sret_in, shg_in,
         o_ref, sret_out, shg_out, b_scr, sret_scr, shg_scr) = refs
    else:
        (z_ref, dmat_ref, rdec_ref, cdec_ref, gc_ref, tri_ref, hgg_ref, ones_ref,
         o_ref, sret_out, shg_out, b_scr, sret_scr, shg_scr) = refs
    hw = d // 2
    rows = z_ref.shape[0]
    n_chunks = rows // c
    dk = d // 16

    if not sample:
        @pl.when(pl.program_id(0) == 0)
        def _():
            sret_scr[...] = jnp.zeros_like(sret_scr)
            shg_scr[...] = jnp.zeros_like(shg_scr)

    lane_c = lax.broadcasted_iota(jnp.int32, (c, LANES), 1)
    row_s = lax.broadcasted_iota(jnp.int32, (SUB, LANES), 0)
    ones = ones_ref[...]
    tri = tri_ref[...]

    def hg_sub(r0, lr0):
        rs = pl.ds(r0, SUB)
        for h in range(HG_HEADS):
            cs = slice(LANES * h, LANES * (h + 1))
            q = z_ref[rs, 3 * hw + LANES * h:3 * hw + LANES * (h + 1)]
            kb = z_ref[rs, 5 * hw + LANES * h:5 * hw + LANES * (h + 1)]
            v = z_ref[rs, 6 * hw + LANES * h:6 * hw + LANES * (h + 1)]
            gate = z_ref[rs, 7 * hw + LANES * h:7 * hw + LANES * (h + 1)]
            bs = b_scr[pl.ds(lr0, SUB), cs]
            bl = bs[SUB - 1:SUB, :]
            st = shg_scr[h]
            cross = _dot3(q * jnp.exp(bs), st, _NT)
            ps = []
            for j in range(SUB):
                e = jnp.exp(bs - bs[j:j + 1, :])
                ps.append(jnp.where(row_s >= j, q * kb[j:j + 1, :] * e, 0.0))
            p = jnp.concatenate(ps, axis=0)
            ph, plo = _split2(p)
            r = _dg(ph, ones, _NN) + _dg(plo, ones, _NN)
            inner = r[0:SUB] * v[0:1, :]
            for j in range(1, SUB):
                inner = inner + r[SUB * j:SUB * (j + 1)] * v[j:j + 1, :]
            out = cross + inner
            shg_scr[h] = st * jnp.exp(bl) + _dot3(v, kb * jnp.exp(bl - bs), _TN)
            ms = jnp.mean(out * out, axis=-1, keepdims=True)
            o_ref[rs, hw + LANES * h:hw + LANES * (h + 1)] = out * lax.rsqrt(ms + NORM_EPS) * hgg_ref[...] * gate

    def ret_chunk(r0):
        rs = pl.ds(r0, c)
        zeros = jnp.zeros((dk, LANES), F32)
        for p in range(RET_HEADS // 2):
            qp = z_ref[rs, LANES * p:LANES * (p + 1)]
            kp = z_ref[rs, hw // 2 + LANES * p:hw // 2 + LANES * (p + 1)]
            for e in range(2):
                h = 2 * p + e
                qh = jnp.where((lane_c // dk) == e, qp, 0.0)
                v = z_ref[rs, hw + LANES * h:hw + LANES * (h + 1)]
                gate = z_ref[rs, 2 * hw + LANES * h:2 * hw + LANES * (h + 1)]
                scores = _dot3(qh, kp, _NT) * dmat_ref[h]
                inner = _dot3(scores, v)
                s = sret_scr[h]
                spad = jnp.concatenate([s, zeros] if e == 0 else [zeros, s], axis=0)
                cross = _dot3(qp, spad) * rdec_ref[h]
                out = inner + cross
                u = _dot3(kp * cdec_ref[h], v, _TN)
                sret_scr[h] = s * gc_ref[h] + u[dk * e:dk * (e + 1)]
                ms = jnp.mean(out * out, axis=-1, keepdims=True)
                o_ref[rs, LANES * h:LANES * (h + 1)] = out * lax.rsqrt(ms + NORM_EPS) * gate

    def chunk_body(ci, carry):
        r0 = pl.multiple_of(ci * c, c)
        if sample:
            sret_scr[...] = sret_in[ci]
            shg_scr[...] = shg_in[ci]
        b_scr[...] = _cumsum_rows(tri, z_ref[pl.ds(r0, c), 4 * hw:5 * hw])

        def sub_body(si, carry2):
            lr0 = pl.multiple_of(si * SUB, SUB)
            hg_sub(r0 + lr0, lr0)
            return carry2

        lax.fori_loop(0, c // SUB, sub_body, 0)
        ret_chunk(r0)
        if sample:
            sret_out[ci] = sret_scr[...]
            shg_out[ci] = shg_scr[...]
        return carry

    lax.fori_loop(0, n_chunks, chunk_body, 0)

    if not sample:
        @pl.when(pl.program_id(0) == pl.num_programs(0) - 1)
        def _():
            sret_out[...] = sret_scr[...]
            shg_out[...] = shg_scr[...]


def _ret_consts(c):
    heads = jnp.arange(RET_HEADS, dtype=F32)
    log_g = jnp.log1p(-jnp.exp2(-5.0 - heads))
    idx = jnp.arange(c, dtype=F32)
    rel = idx[:, None] - idx[None, :]
    dmat = jnp.exp(jnp.where(rel[None] >= 0, rel[None] * log_g[:, None, None], -jnp.inf))
    rdec = jnp.exp((idx[None, :] + 1.0) * log_g[:, None])
    cdec = jnp.exp((c - 1.0 - idx)[None, :] * log_g[:, None])
    gc = jnp.exp(c * log_g)
    bc = lambda a: jnp.broadcast_to(a[:, :, None], (RET_HEADS, c, LANES))
    return dmat, bc(rdec), bc(cdec), gc


def _ab_rec(z, hg_g, sret, shg_t, cfg, sample, d):
    n = z.shape[0]
    ng, npt = cfg["ng"], cfg["npt"]
    c = cfg["ts"] if sample else CHUNK
    dk, dv = d // 16, d // 8
    dmat, rdec, cdec, gc = _ret_consts(c)
    gcb = jnp.broadcast_to(gc[:, None, None], (RET_HEADS, dk, LANES))
    tri = np.kron(np.eye(c // SUB), np.tril(np.ones((SUB, SUB))))
    tri = jnp.asarray(tri, BF16)
    ones = jnp.ones((LANES, LANES), BF16)
    kern = functools.partial(_ab_rec_kernel, d=d, c=c, sample=sample)
    consts = [dmat, rdec, cdec, gcb, tri, hg_g.reshape(1, dv), ones]
    const_specs = [
        _const_spec(dmat.shape, (0, 0, 0)), _const_spec(rdec.shape, (0, 0, 0)),
        _const_spec(cdec.shape, (0, 0, 0)), _const_spec(gcb.shape, (0, 0, 0)),
        _const_spec(tri.shape, (0, 0)), _const_spec((1, dv), (0, 0)), _const_spec(ones.shape, (0, 0)),
    ]
    scratch = [pltpu.VMEM((c, d // 2), F32), pltpu.VMEM((RET_HEADS, dk, dv), F32),
               pltpu.VMEM((HG_HEADS, dv, dv), F32)]
    if sample:
        nb = sret.shape[0]
        nst = nb // ng
        return pl.pallas_call(
            kern,
            grid=(nst,),
            in_specs=[pl.BlockSpec((TM, z.shape[1]), lambda i: (npt + i, 0))] + const_specs + [
                pl.BlockSpec((ng, RET_HEADS, dk, dv), lambda i: (i, 0, 0, 0)),
                pl.BlockSpec((ng, HG_HEADS, dv, dv), lambda i: (i, 0, 0, 0)),
            ],
            out_specs=[
                pl.BlockSpec((TM, d), lambda i: (i, 0)),
                pl.BlockSpec((ng, RET_HEADS, dk, dv), lambda i: (i, 0, 0, 0)),
                pl.BlockSpec((ng, HG_HEADS, dv, dv), lambda i: (i, 0, 0, 0)),
            ],
            out_shape=[
                jax.ShapeDtypeStruct((nst * TM, d), F32),
                jax.ShapeDtypeStruct((nb, RET_HEADS, dk, dv), F32),
                jax.ShapeDtypeStruct((nb, HG_HEADS, dv, dv), F32),
            ],
            scratch_shapes=scratch,
            compiler_params=_params(("arbitrary",)),
        )(z, *consts, sret, shg_t)
    return pl.pallas_call(
        kern,
        grid=(npt,),
        in_specs=[pl.BlockSpec((TM, z.shape[1]), lambda i: (i, 0))] + const_specs,
        out_specs=[
            pl.BlockSpec((TM, d), lambda i: (i, 0)),
            pl.BlockSpec((RET_HEADS, dk, dv), lambda i: (0, 0, 0)),
            pl.BlockSpec((HG_HEADS, dv, dv), lambda i: (0, 0, 0)),
        ],
        out_shape=[
            jax.ShapeDtypeStruct((npt * TM, d), F32),
            jax.ShapeDtypeStruct((RET_HEADS, dk, dv), F32),
            jax.ShapeDtypeStruct((HG_HEADS, dv, dv), F32),
        ],
        scratch_shapes=scratch,
        compiler_params=_params(("arbitrary",)),
    )(z, *consts)


def _post_kernel(*refs, d, rwkv, final):
    it = iter(refs)
    x_ref = next(it)
    if rwkv:
        y_ref, bonus_ref, gate_ref, lng_ref, lnb_ref, seg_ref = (next(it) for _ in range(6))
    else:
        o_ref_in = next(it)
    mod_ref, wout_ref, g2_ref, w1_ref, w2_ref = (next(it) for _ in range(5))
    if final:
        fg_ref = next(it)
    out_ref = next(it)

    x = x_ref[...]
    if rwkv:
        y = jnp.concatenate([y_ref[p] for p in range(y_ref.shape[0])], axis=-1)
        seg = seg_ref[...]
        inv_n = 1.0 / RW_N
        mean = _segsum(y, seg) * inv_n
        yc = y - mean
        var = _segsum(yc * yc, seg) * inv_n
        o = yc * lax.rsqrt(var + RW_LN_EPS) * lng_ref[0] + lnb_ref[0]
        o = (o + bonus_ref[...]) * gate_ref[...]
    else:
        o = o_ref_in[...]
    gt1 = mod_ref[0, :, :, 2 * d:3 * d]
    sh2 = mod_ref[0, :, :, 3 * d:4 * d]
    sc2 = mod_ref[0, :, :, 4 * d:5 * d]
    gt2 = mod_ref[0, :, :, 5 * d:6 * d]
    x1 = _gate_add(x, gt1, _dg(o.astype(BF16), wout_ref[0], _NN))
    hb = _modulate(_rms(x1, g2_ref[0]), sc2, sh2).astype(BF16)
    ff = w1_ref.shape[2]
    acc = jnp.zeros_like(x1)
    for j in range(ff // d):
        u = jnp.maximum(_dg(hb, w1_ref[0, :, j * d:(j + 1) * d], _NN), 0.0)
        acc = acc + _dg((u * u).astype(BF16), w2_ref[0, j * d:(j + 1) * d, :], _NN)
    x2 = _gate_add(x1, gt2, acc)
    if final:
        x2 = _rms(x2, fg_ref[...])
    out_ref[...] = x2


def _post(x, mixer_in, modt, wout, g2, w1, w2, final_g, layer, m, cfg, rwkv, final):
    n, d = x.shape
    ng, npt = cfg["ng"], cfg["npt"]
    nt = n // TM
    kern = functools.partial(_post_kernel, d=d, rwkv=rwkv, final=final)
    tok = pl.BlockSpec((TM, d), lambda i: (i, 0))
    args = [x]
    specs = [tok]
    if rwkv:
        y, bonus, gate, lng, lnb, seg = mixer_in
        args += [y, bonus, gate, lng, lnb, seg]
        specs += [pl.BlockSpec((y.shape[0], TM, LANES), lambda i: (0, i, 0)), tok, tok,
                  _const_spec((1, 1, d), (m, 0, 0)), _const_spec((1, 1, d), (m, 0, 0)),
                  _const_spec((d, d), (0, 0))]
    else:
        args += [mixer_in]
        specs += [tok]
    args += [modt, wout, g2, w1, w2]
    specs += [
        pl.BlockSpec((1, ng, 1, 6 * d), lambda i: (layer, jnp.maximum(i - npt + 1, 0), 0, 0)),
        _const_spec((1, d, d), (m, 0, 0)),
        _const_spec((1, 1, d), (layer, 0, 0)),
        _const_spec((1, d, w1.shape[2]), (layer, 0, 0)),
        _const_spec((1, w2.shape[1], d), (layer, 0, 0)),
    ]
    if final:
        args += [final_g]
        specs += [_const_spec((1, d), (0, 0))]
    return pl.pallas_call(
        kern,
        grid=(nt,),
        in_specs=specs,
        out_specs=tok,
        out_shape=jax.ShapeDtypeStruct((n, d), F32),
        compiler_params=_params(("arbitrary",)),
    )(*args)


def _rw_in_kernel(*refs, d, ts, npt, vres):
    it = iter(refs)
    x_ref, mod_ref, g_ref, shift_ref, mu_ref, wrkv_ref = (next(it) for _ in range(6))
    w0_ref, w1_ref, w2_ref, a0_ref, a1_ref, a2_ref, g1_ref, g2_ref = (next(it) for _ in range(8))
    kk_ref, ka_ref, rk_ref, seg_ref = (next(it) for _ in range(4))
    if vres:
        vf_ref, v0_ref, v1_ref, v2_ref = (next(it) for _ in range(4))
    r_out, ld_out, k_out, v_out, kk_out, kka_out, gate_out, bonus_out, hlast_out = (next(it) for _ in range(9))
    if not vres:
        vtok_out = next(it)
    h_scr, carry_scr = next(it), next(it)

    i = pl.program_id(0)
    rows = x_ref.shape[0]
    ng = rows // ts

    @pl.when(i == 0)
    def _():
        carry_scr[...] = jnp.zeros_like(carry_scr)

    x = x_ref[...]
    h = _modulate(_rms(x, g_ref[0]), mod_ref[0, :, :, d:2 * d], mod_ref[0, :, :, 0:d])
    for p in range(d // LANES):
        h_scr[p] = h[:, LANES * p:LANES * (p + 1)]
        hlast_out[:, LANES * p:LANES * (p + 1)] = h_scr[p, pl.ds(ts - 1, ng, stride=ts), :]
    rolled = pltpu.roll(h, 1, 0)
    row = lax.broadcasted_iota(jnp.int32, (rows, d), 0)
    s_i = jnp.where(i >= npt, 1, 0)
    first = ((row % ts) == 0) & ((row * (1 - s_i)) == 0)
    seq_prev = jnp.broadcast_to(shift_ref[...], (ng, ts, d)).reshape(rows, d)
    prev_first = jnp.where((row * 0 + s_i) == 1, seq_prev, jnp.broadcast_to(carry_scr[0:1, :], (rows, d)))
    prev = jnp.where(first, prev_first, rolled)
    carry_scr[0:1, :] = h[rows - 1:rows, :]

    xx = prev - h
    mix = lambda j: (h + xx * mu_ref[0, j:j + 1, :]).astype(BF16)
    xr, xw, xk, xv, xa, xg = (mix(j) for j in range(6))
    r = _dg(xr, wrkv_ref[0, 0], _NN)
    k = _dg(xk, wrkv_ref[0, 1], _NN)
    v = _dg(xv, wrkv_ref[0, 2], _NN)
    if vres:
        lv = v0_ref[0] + _dot1(_dg(xv, v1_ref[0], _NN), v2_ref[0])
        v = v + (vf_ref[...] - v) * _sigmoid(lv)
    else:
        vtok_out[...] = v
    wl = w0_ref[0] + _dot1(jnp.tanh(_dg(xw, w1_ref[0], _NN)), w2_ref[0])
    nwl = -wl
    w = -(jnp.maximum(nwl, 0.0) + jnp.log1p(jnp.exp(-jnp.abs(nwl)))) - 0.5
    ld = -jnp.exp(w)
    a = _sigmoid(a0_ref[0] + _dot1(_dg(xa, a1_ref[0], _NN), a2_ref[0]))
    gate = _dot1(_sigmoid(_dg(xg, g1_ref[0], _NN)), g2_ref[0])
    seg = seg_ref[...]
    kk = k * kk_ref[0]
    kk = kk / jnp.maximum(jnp.sqrt(_segsum(kk * kk, seg)), 1e-12)
    km = k * (1.0 + (a - 1.0) * ka_ref[0])
    bonus = _segsum(r * km * rk_ref[0], seg) * v
    gate_out[...] = gate
    bonus_out[...] = bonus
    kka = kk * a
    for p in range(d // LANES):
        cs = slice(LANES * p, LANES * (p + 1))
        r_out[p] = r[:, cs]
        ld_out[p] = ld[:, cs]
        k_out[p] = km[:, cs]
        v_out[p] = v[:, cs]
        kk_out[p] = kk[:, cs]
        kka_out[p] = kka[:, cs]


def _rw_in(x, modt, g, shift_rows, W, seg, v_first, layer, m, cfg):
    n, d = x.shape
    ng, npt, ts = cfg["ng"], cfg["npt"], cfg["ts"]
    nt = n // TM
    npair = d // LANES
    vres = v_first is not None
    kern = functools.partial(_rw_in_kernel, d=d, ts=ts, npt=npt, vres=vres)
    tok = pl.BlockSpec((TM, d), lambda i: (i, 0))
    vec = lambda: _const_spec((1, 1, d), (m, 0, 0))
    lora = lambda a: _const_spec((1,) + a.shape[1:], (m, 0, 0))
    args = [x, modt, g, shift_rows, W["rw_mu"], W["rw_w_rkv"],
            W["rw_w0"], W["rw_w1"], W["rw_w2"], W["rw_a0"], W["rw_a1"], W["rw_a2"], W["rw_g1"], W["rw_g2"],
            W["rw_k_k"], W["rw_k_a"], W["rw_r_k"], seg]
    specs = [
        tok,
        pl.BlockSpec((1, ng, 1, 6 * d), lambda i: (layer, jnp.maximum(i - npt + 1, 0), 0, 0)),
        _const_spec((1, 1, d), (layer, 0, 0)),
        pl.BlockSpec((ng, 1, d), lambda i: (jnp.maximum(i - npt, 0), 0, 0)),
        _const_spec((1, 6, d), (m, 0, 0)),
        _const_spec((1, 3, d, d), (m, 0, 0, 0)),
        vec(), lora(W["rw_w1"]), lora(W["rw_w2"]), vec(), lora(W["rw_a1"]), lora(W["rw_a2"]),
        lora(W["rw_g1"]), lora(W["rw_g2"]), vec(), vec(), vec(), _const_spec((d, d), (0, 0)),
    ]
    if vres:
        args += [v_first, W["rw_v0"], W["rw_v1"], W["rw_v2"]]
        specs += [tok, _const_spec((1, 1, d), (m - 1, 0, 0)),
                  _const_spec((1,) + W["rw_v1"].shape[1:], (m - 1, 0, 0)),
                  _const_spec((1,) + W["rw_v2"].shape[1:], (m - 1, 0, 0))]
    pm = pl.BlockSpec((npair, TM, LANES), lambda i: (0, i, 0))
    pm_shape = jax.ShapeDtypeStruct((npair, n, LANES), F32)
    tok_shape = jax.ShapeDtypeStruct((n, d), F32)
    out_specs = [pm] * 6 + [tok, tok, pl.BlockSpec((ng, d), lambda i: (i, 0))]
    out_shape = [pm_shape] * 6 + [tok_shape, tok_shape, jax.ShapeDtypeStruct((nt * ng, d), F32)]
    if not vres:
        out_specs.append(tok)
        out_shape.append(tok_shape)
    return pl.pallas_call(
        kern,
        grid=(nt,),
        in_specs=specs,
        out_specs=out_specs,
        out_shape=out_shape,
        scratch_shapes=[pltpu.VMEM((npair, TM, LANES), F32), pltpu.VMEM((8, d), F32)],
        compiler_params=_params(("arbitrary",)),
    )(*args)


def _rw_rec_kernel(*refs, c, sample):
    if sample:
        (r_ref, ld_ref, k_ref, v_ref, kk_ref, kka_ref, tri_ref, ms_ref, mi_ref, eye_ref, s_in, y_alias,
         y_ref, s_out, s_scr) = refs
    else:
        (r_ref, ld_ref, k_ref, v_ref, kk_ref, kka_ref, tri_ref, ms_ref, mi_ref, eye_ref,
         y_ref, s_out, s_scr) = refs
    rows = r_ref.shape[1]
    n_chunks = rows // c
    n = 2 * c

    if not sample:
        @pl.when(pl.program_id(1) == 0)
        def _():
            s_scr[...] = jnp.zeros_like(s_scr)

    lane = lax.broadcasted_iota(jnp.int32, (c, LANES), 1)
    left = lane < RW_N
    tri = tri_ref[...]
    mask_strict = ms_ref[...]
    mask_incl = mi_ref[...]
    eye = eye_ref[...]

    def stack(xv):
        return jnp.concatenate([jnp.where(left, xv, 0.0), jnp.where(left, 0.0, xv)], axis=0)

    def chunk_body(ci, carry):
        rs = pl.ds(pl.multiple_of(ci * c, c), c)
        if sample:
            s_scr[...] = s_in[0, ci]
        s = s_scr[...]
        r = r_ref[0, rs, :]
        ld = ld_ref[0, rs, :]
        k = k_ref[0, rs, :]
        v = v_ref[0, rs, :]
        kk = kk_ref[0, rs, :]
        kka = kka_ref[0, rs, :]
        lc = _cumsum_rows(tri, ld)
        ltot = lc[c - 1:c, :]
        g_inv = jnp.exp(-lc)
        g_end = jnp.exp(ltot - lc)
        ur = jnp.concatenate([stack(-kk * jnp.exp(lc - ld)), stack(r * jnp.exp(lc))], axis=0)
        bk = jnp.concatenate([stack(kka * g_inv), stack(k * g_inv)], axis=0)
        big = _dot3(ur, bk, _NT)
        m_ub = big[0:n, 0:n] * mask_strict
        m_uk = big[0:n, n:2 * n] * mask_strict
        a_rb = big[n:2 * n, 0:n] * mask_incl
        a_rk = big[n:2 * n, n:2 * n] * mask_incl
        t = eye + m_ub
        mp = m_ub
        steps = int(np.log2(c)) - 1
        for _ in range(steps):
            mp = _dot3(mp, mp)
            t = t + _dot3(t, mp)
        vst = stack(v)
        urs = _dot3(ur, s, _NT)
        e = _dot3(t, urs[0:n] + _dot3(m_uk, vst))
        yst = urs[n:2 * n] + _dot3(a_rb, e) + _dot3(a_rk, vst)
        y_ref[0, rs, :] = yst[0:c] + yst[c:n]
        s_new = s * jnp.exp(ltot) + _dot3(e, stack(kka * g_end), _TN) + _dot3(vst, stack(k * g_end), _TN)
        s_scr[...] = s_new
        if sample:
            s_out[0, ci] = s_new
        return carry

    lax.fori_loop(0, n_chunks, chunk_body, 0)

    if not sample:
        @pl.when(pl.program_id(1) == pl.num_programs(1) - 1)
        def _():
            s_out[0] = s_scr[...]


def _rw_rec(seqs, s_bd, y_prev, cfg, sample):
    npair, n, _ = seqs[0].shape
    ng, npt = cfg["ng"], cfg["npt"]
    c = cfg["ts"] if sample else CHUNK
    tri = jnp.asarray(np.tril(np.ones((c, c))), BF16)
    blk = np.kron(np.eye(2), np.ones((c, c)))
    big_tril = np.tril(np.ones((2 * c, 2 * c)))
    mask_incl = jnp.asarray(blk * big_tril, F32)
    mask_strict = jnp.asarray(blk * (big_tril - np.eye(2 * c)), F32)
    eye = jnp.asarray(np.eye(2 * c), F32)
    consts = [tri, mask_strict, mask_incl, eye]
    const_specs = [_const_spec(a.shape, (0, 0)) for a in consts]
    kern = functools.partial(_rw_rec_kernel, c=c, sample=sample)
    y_shape = jax.ShapeDtypeStruct((npair, n, LANES), F32)
    scratch = [pltpu.VMEM((LANES, LANES), F32)]
    if sample:
        nb = s_bd.shape[1]
        nst = nb // ng
        tokp = pl.BlockSpec((1, TM, LANES), lambda p, i: (p, npt + i, 0))
        sspec = pl.BlockSpec((1, ng, LANES, LANES), lambda p, i: (p, i, 0, 0))
        return pl.pallas_call(
            kern,
            grid=(npair, nst),
            in_specs=[tokp] * 6 + const_specs + [sspec, pl.BlockSpec(memory_space=pl.ANY)],
            out_specs=[tokp, sspec],
            out_shape=[y_shape, jax.ShapeDtypeStruct((npair, nb, LANES, LANES), F32)],
            scratch_shapes=scratch,
            input_output_aliases={len(seqs) + len(consts) + 1: 0},
            compiler_params=_params(("arbitrary", "arbitrary")),
        )(*seqs, *consts, s_bd, y_prev)
    tokp = pl.BlockSpec((1, TM, LANES), lambda p, i: (p, i, 0))
    return pl.pallas_call(
        kern,
        grid=(npair, npt),
        in_specs=[tokp] * 6 + const_specs,
        out_specs=[tokp, pl.BlockSpec((1, LANES, LANES), lambda p, i: (p, 0, 0))],
        out_shape=[y_shape, jax.ShapeDtypeStruct((npair, LANES, LANES), F32)],
        scratch_shapes=scratch,
        compiler_params=_params(("arbitrary", "arbitrary")),
    )(*seqs, *consts)


def kernel(x_prompt, x_sample, state_ret, state_hgrn, state_wkv, state_shift, c_prompt, c_sample, mod_w, mod_b, norm_mix_g, norm_mlp_g, final_g, mlp_w1, mlp_w2, ab_w_in, ab_w_out, hg_lb, hg_norm_g, rw_mu, rw_w_rkv, rw_w0, rw_w1, rw_w2, rw_a0, rw_a1, rw_a2, rw_v0, rw_v1, rw_v2, rw_g1, rw_g2, rw_k_k, rw_k_a, rw_r_k, rw_ln_g, rw_ln_b, rw_w_out):
    bp, tp, d = x_prompt.shape
    bs, ts, _ = x_sample.shape
    depth = mod_w.shape[0]
    n_ab = ab_w_in.shape[0]
    n_c = rw_w_rkv.shape[0]
    assert bp == 1 and d == 1024 and tp % TM == 0 and tp % CHUNK == 0
    assert (bs * ts) % TM == 0 and TM % ts == 0 and ts % SUB == 0 and ts <= CHUNK
    np_rows, ns_rows = bp * tp, bs * ts
    n = np_rows + ns_rows
    ng = TM // ts
    assert bs % ng == 0
    cfg = dict(ng=ng, npt=np_rows // TM, ts=ts)
    dk, dv = d // 16, d // 8
    npair = d // LANES
    nh = d // RW_N

    c_all = jnp.concatenate([c_prompt, c_sample], axis=0)
    pad = (-c_all.shape[0]) % 8
    c_all = jnp.pad(c_all, ((0, pad), (0, 0)))
    mod = _modulation(c_all, mod_w, mod_b)
    modt = jnp.concatenate([jnp.broadcast_to(mod[:, 0:1], (depth, ng, 6 * d)), mod[:, bp:bp + bs]], axis=1)
    modt = modt.reshape(depth, ng + bs, 1, 6 * d)

    x = jnp.concatenate([x_prompt.reshape(np_rows, d), x_sample.reshape(ns_rows, d)], axis=0)

    pos = jnp.concatenate([jnp.arange(tp, dtype=F32), jnp.tile(PAST_LEN + jnp.arange(ts, dtype=F32), bs)])
    half = dk // 2
    inv = ROPE_BASE ** (-jnp.arange(half, dtype=F32) / half)
    ang = pos[:, None] * inv[None, :]
    cos_t = jnp.tile(jnp.cos(ang), (1, 4))
    sin_t = jnp.tile(jnp.concatenate([-jnp.sin(ang), jnp.sin(ang)], axis=1), (1, 2))

    lb_all = jnp.cumsum(jax.nn.softmax(hg_lb.astype(F32), axis=0), axis=0)
    lb_all = lb_all - lb_all[:1]
    lbc = jnp.stack([jnp.log(lb_all), jnp.log1p(-lb_all), 1.0 - lb_all], axis=1)
    lbc = jnp.pad(lbc, ((0, 0), (0, 5), (0, 0))).reshape(n_ab * 8, d // 2)

    bf = lambda a: a.astype(BF16)
    ab_w_in_b, ab_w_out_b = bf(ab_w_in), bf(ab_w_out)
    mlp_w1_b, mlp_w2_b = bf(mlp_w1), bf(mlp_w2)
    rw_w_out_b = bf(rw_w_out)
    vec = lambda a: a.reshape(a.shape[0], 1, d)
    W = dict(rw_mu=rw_mu, rw_w_rkv=bf(rw_w_rkv), rw_w0=vec(rw_w0), rw_w1=bf(rw_w1), rw_w2=bf(rw_w2),
             rw_a0=vec(rw_a0), rw_a1=bf(rw_a1), rw_a2=bf(rw_a2), rw_g1=bf(rw_g1), rw_g2=bf(rw_g2),
             rw_k_k=vec(rw_k_k), rw_k_a=vec(rw_k_a), rw_r_k=vec(rw_r_k), rw_v0=vec(rw_v0), rw_v1=bf(rw_v1),
             rw_v2=bf(rw_v2))
    seg = jnp.asarray(np.kron(np.eye(nh), np.ones((RW_N, RW_N))), BF16)
    g_mix = norm_mix_g.reshape(depth, 1, d)
    g_mlp = norm_mlp_g.reshape(depth, 1, d)
    fin_g = final_g.reshape(1, d)

    ret_p, ret_s, hg_p, hg_s, wkv_p, wkv_s, sh_p, sh_s = ([] for _ in range(8))
    v_first = None
    for layer in range(depth):
        m = layer // 2
        final = layer == depth - 1
        if layer % 2 == 0:
            z = _ab_in(x, modt, g_mix, ab_w_in_b, cos_t, sin_t, lbc, layer, m, cfg)
            o_p, r_p, h_p = _ab_rec(z, hg_norm_g[m], None, None, cfg, False, d)
            o_s, r_s, h_s = _ab_rec(z, hg_norm_g[m], state_ret[m], jnp.swapaxes(state_hgrn[m], -1, -2), cfg, True, d)
            ret_p.append(r_p[None])
            hg_p.append(jnp.swapaxes(h_p, -1, -2)[None])
            ret_s.append(r_s)
            hg_s.append(jnp.swapaxes(h_s, -1, -2))
            o = jnp.concatenate([o_p, o_s], axis=0)
            x = _post(x, o, modt, ab_w_out_b, g_mlp, mlp_w1_b, mlp_w2_b, fin_g, layer, m, cfg, False, final)
        else:
            shift_rows = state_shift[m].reshape(bs, 1, d)
            outs = _rw_in(x, modt, g_mix, shift_rows, W, seg, v_first, layer, m, cfg)
            seqs, gate, bonus, hlast = outs[0:6], outs[6], outs[7], outs[8]
            if v_first is None:
                v_first = outs[9]
            y_p, s_p = _rw_rec(seqs, None, None, cfg, False)
            sw = state_wkv[m].reshape(bs, npair, 2, RW_N, RW_N)
            zero = jnp.zeros_like(sw[:, :, 0])
            s_bd = jnp.concatenate([jnp.concatenate([sw[:, :, 0], zero], axis=-1),
                                    jnp.concatenate([zero, sw[:, :, 1]], axis=-1)], axis=-2)
            y, s_s = _rw_rec(seqs, jnp.swapaxes(s_bd, 0, 1), y_p, cfg, True)
            unbd = lambda a: jnp.stack([a[..., :RW_N, :RW_N], a[..., RW_N:, RW_N:]], axis=-3)
            wkv_p.append(unbd(s_p).reshape(1, 1, nh, RW_N, RW_N))
            wkv_s.append(unbd(jnp.swapaxes(s_s, 0, 1)).reshape(bs, nh, RW_N, RW_N))
            sh_p.append(hlast[cfg["npt"] * ng - 1][None, None])
            sh_s.append(hlast[cfg["npt"] * ng:][None])
            x = _post(x, (y, bonus, gate, vec(rw_ln_g), vec(rw_ln_b), seg), modt, rw_w_out_b, g_mlp, mlp_w1_b, mlp_w2_b,
                      fin_g, layer, m, cfg, True, final)

    y_p = x[:np_rows].reshape(bp, tp, d)
    y_s = x[np_rows:].reshape(bs, ts, d)
    return (y_p, y_s,
            jnp.stack(ret_p), jnp.stack(ret_s), jnp.stack(hg_p), jnp.stack(hg_s),
            jnp.concatenate(wkv_p, axis=0), jnp.stack(wkv_s),
            jnp.concatenate(sh_p, axis=0), jnp.concatenate(sh_s, axis=0))
```

```python
import functools

import numpy as np
import jax
import jax.numpy as jnp
from jax import lax
from jax.experimental import pallas as pl
from jax.experimental.pallas import tpu as pltpu

F32 = jnp.float32
BF16 = jnp.bfloat16

CHUNK = 64
PAST_LEN = 2048
ROPE_BASE = 10000.0
RET_HEADS = 4
HG_HEADS = 4
RW_N = 64
NORM_EPS = 1e-6
RW_LN_EPS = 64e-5

LANES = 128
SUB = 16
TM = 256
VMEM_LIMIT = 56 * 1024 * 1024


def _sigmoid(x):
    return 1.0 / (1.0 + jnp.exp(-x))


def _split2(x):
    hi = x.astype(BF16)
    lo = (x - hi.astype(F32)).astype(BF16)
    return hi, lo


def _split3(x):
    a = x.astype(BF16)
    r = x - a.astype(F32)
    b = r.astype(BF16)
    c = (r - b.astype(F32)).astype(BF16)
    return a, b, c


_NN = (((1,), (0,)), ((), ()))
_NT = (((1,), (1,)), ((), ()))
_TN = (((0,), (0,)), ((), ()))


def _dg(a, b, dims):
    return lax.dot_general(a, b, dims, preferred_element_type=F32)


def _dot1(a, b, dims=_NN):
    return _dg(a.astype(BF16), b.astype(BF16), dims)


def _dot3(a, b, dims=_NN):
    ah, al = _split2(a)
    bh, bl = _split2(b)
    return _dg(ah, bh, dims) + _dg(ah, bl, dims) + _dg(al, bh, dims)


def _dotp(a, b, dims, passes):
    return _dot1(a, b, dims) if passes == 1 else _dot3(a, b, dims)


P_BIG = 1
P_ST = 1


def _cumsum_rows(tri_bf16, x):
    a, b, c = _split3(x)
    return _dg(tri_bf16, a, _NN) + _dg(tri_bf16, b, _NN) + _dg(tri_bf16, c, _NN)


def _segsum(x, seg_bf16):
    hi, lo = _split2(x)
    return _dg(hi, seg_bf16, _NN) + _dg(lo, seg_bf16, _NN)


def _rms(x, g):
    ms = jnp.mean(x * x, axis=-1, keepdims=True)
    return x * lax.rsqrt(ms + NORM_EPS) * g


def _modulate(y, sc, sh):
    rows, d = y.shape
    ng = sc.shape[0]
    y3 = y.reshape(ng, rows // ng, d)
    return (y3 * (1.0 + sc) + sh).reshape(rows, d)


def _gate_add(x, gt, out):
    rows, d = x.shape
    ng = gt.shape[0]
    return (x.reshape(ng, rows // ng, d) + gt * out.reshape(ng, rows // ng, d)).reshape(rows, d)


def _const_spec(block, index):
    return pl.BlockSpec(block, lambda *_: index, pipeline_mode=pl.Buffered(1))


def _params(sem):
    return pltpu.CompilerParams(dimension_semantics=sem, vmem_limit_bytes=VMEM_LIMIT)


def _mod_kernel(c_ref, w_ref, b_ref, o_ref):
    c = c_ref[...]
    s = c * _sigmoid(c)
    o_ref[0] = _dot1(s, w_ref[0]) + b_ref[0]


def _modulation(c_all, mod_w, mod_b):
    depth, d, d6 = mod_w.shape
    rows = c_all.shape[0]
    nt = d6 // d
    return pl.pallas_call(
        _mod_kernel,
        grid=(depth, nt),
        in_specs=[
            pl.BlockSpec((rows, d), lambda l, j: (0, 0)),
            pl.BlockSpec((1, d, d), lambda l, j: (l, 0, j)),
            pl.BlockSpec((1, 1, d), lambda l, j: (l, 0, j)),
        ],
        out_specs=pl.BlockSpec((1, rows, d), lambda l, j: (l, 0, j)),
        out_shape=jax.ShapeDtypeStruct((depth, rows, d6), F32),
        compiler_params=_params(("arbitrary", "arbitrary")),
        name="modulation",
    )(c_all, mod_w, mod_b.reshape(depth, 1, d6))


def _ab_in_kernel(x_ref, mod_ref, g_ref, w_ref, cos_ref, sin_ref, lb_ref, o_ref, *, d, dk_scale):
    x = x_ref[...]
    h = _modulate(_rms(x, g_ref[0]), mod_ref[0, :, :, d:2 * d], mod_ref[0, :, :, 0:d])
    hb = h.astype(BF16)
    hw = d // 2

    def proj(a):
        return _dg(hb, w_ref[0, :, a:a + hw], _NN)

    qk = proj(0)
    lane = lax.broadcasted_iota(jnp.int32, (x.shape[0], LANES), 1)
    first_half = (lane % 64) < 32
    cos = cos_ref[...]
    sin = sin_ref[...]
    for s in range(hw // LANES):
        xs = qk[:, LANES * s:LANES * (s + 1)]
        sw = jnp.where(first_half, pltpu.roll(xs, 96, 1), pltpu.roll(xs, 32, 1))
        rot = xs * cos + sw * sin
        if s < hw // (2 * LANES):
            rot = rot * dk_scale
        o_ref[:, LANES * s:LANES * (s + 1)] = rot
    o_ref[:, hw:2 * hw] = proj(hw)
    z = proj(2 * hw)
    o_ref[:, 2 * hw:3 * hw] = z * _sigmoid(z)
    z = proj(3 * hw)
    o_ref[:, 3 * hw:4 * hw] = z * _sigmoid(z)
    z = proj(4 * hw)
    log_lb = lb_ref[0:1, :]
    log1m_lb = lb_ref[1:2, :]
    one_m_lb = lb_ref[2:3, :]
    ls = jnp.minimum(z, 0.0) - jnp.log1p(jnp.exp(-jnp.abs(z)))
    c = log1m_lb + ls
    m = jnp.maximum(log_lb, c)
    o_ref[:, 4 * hw:5 * hw] = m + jnp.log1p(jnp.exp(-jnp.abs(log_lb - c)))
    o_ref[:, 5 * hw:6 * hw] = one_m_lb * (1.0 / (1.0 + jnp.exp(z)))
    o_ref[:, 6 * hw:7 * hw] = proj(5 * hw)
    o_ref[:, 7 * hw:8 * hw] = _sigmoid(proj(6 * hw))


def _ab_in(x, modt, g, w_in, cos_t, sin_t, lbc, layer, m, cfg):
    n, d = x.shape
    ng, npt = cfg["ng"], cfg["npt"]
    nt = n // TM
    zw = 4 * d
    kern = functools.partial(_ab_in_kernel, d=d, dk_scale=float((d // 16) ** -0.5))
    return pl.pallas_call(
        kern,
        grid=(nt,),
        in_specs=[
            pl.BlockSpec((TM, d), lambda i: (i, 0)),
            pl.BlockSpec((1, ng, 1, 6 * d), lambda i: (layer, jnp.maximum(i - npt + 1, 0), 0, 0)),
            _const_spec((1, 1, d), (layer, 0, 0)),
            _const_spec((1, d, w_in.shape[2]), (m, 0, 0)),
            pl.BlockSpec((TM, LANES), lambda i: (i, 0)),
            pl.BlockSpec((TM, LANES), lambda i: (i, 0)),
            _const_spec((8, d // 2), (m, 0)),
        ],
        out_specs=pl.BlockSpec((TM, zw), lambda i: (i, 0)),
        out_shape=jax.ShapeDtypeStruct((n, zw), F32),
        compiler_params=_params(("arbitrary",)),
        name="ab_in",
    )(x, modt, g, w_in, cos_t, sin_t, lbc)


def _ab_rec_kernel(*refs, d, c, sample):
    if sample:
        (z_ref, dmat_ref, rdec_ref, cdec_ref, gc_ref, tri_ref, hgg_ref, ones_ref, sret_in, shg_in,
         o_ref, sret_out, shg_out, sret_scr, shg_scr) = refs
    else:
        (z_ref, dmat_ref, rdec_ref, cdec_ref, gc_ref, tri_ref, hgg_ref, ones_ref,
         o_ref, sret_out, shg_out, sret_scr, shg_scr) = refs
    hw = d // 2
    rows = z_ref.shape[0]
    n_chunks = rows // c
    dk = d // 16

    if not sample:
        @pl.when(pl.program_id(0) == 0)
        def _():
            sret_scr[...] = jnp.zeros_like(sret_scr)
            shg_scr[...] = jnp.zeros_like(shg_scr)

    lane_c = lax.broadcasted_iota(jnp.int32, (c, LANES), 1)
    row_s = lax.broadcasted_iota(jnp.int32, (SUB, LANES), 0)
    ones = ones_ref[...]
    tri = tri_ref[...]

    def hg_chunk(r0, b):
        col = lambda g, h: slice(g * hw + LANES * h, g * hw + LANES * (h + 1))
        units = [(i, h) for i in range(c // SUB) for h in range(HG_HEADS)]
        rs = {i: pl.ds(r0 + SUB * i, SUB) for i in range(c // SUB)}
        q = {u: z_ref[rs[u[0]], col(3, u[1])] for u in units}
        kb = {u: z_ref[rs[u[0]], col(5, u[1])] for u in units}
        v = {u: z_ref[rs[u[0]], col(6, u[1])] for u in units}
        bs = {(i, h): b[SUB * i:SUB * (i + 1), LANES * h:LANES * (h + 1)] for (i, h) in units}
        bl = {u: bs[u][SUB - 1:SUB, :] for u in units}
        upd = {u: _dot1(v[u], kb[u] * jnp.exp(bl[u] - bs[u]), _TN) for u in units}
        r = {}
        for u in units:
            ps = []
            for j in range(SUB):
                e = jnp.exp(bs[u] - bs[u][j:j + 1, :])
                ps.append(jnp.where(row_s >= j, q[u] * kb[u][j:j + 1, :] * e, 0.0).astype(BF16))
            r[u] = _dg(jnp.concatenate(ps, axis=0), ones, _NN)
        inner = {}
        for u in units:
            acc = r[u][0:SUB] * v[u][0:1, :]
            for j in range(1, SUB):
                acc = acc + r[u][SUB * j:SUB * (j + 1)] * v[u][j:j + 1, :]
            inner[u] = acc
        st = {h: shg_scr[h] for h in range(HG_HEADS)}
        for u in units:
            i, h = u
            out = inner[u] + _dot1(q[u] * jnp.exp(bs[u]), st[h], _NT)
            st[h] = st[h] * jnp.exp(bl[u]) + upd[u]
            ms = jnp.mean(out * out, axis=-1, keepdims=True)
            o_ref[rs[i], col(1, h)] = out * lax.rsqrt(ms + NORM_EPS) * hgg_ref[...] * z_ref[rs[i], col(7, h)]
        for h in range(HG_HEADS):
            shg_scr[h] = st[h]

    def ret_chunk(r0):
        rs = pl.ds(r0, c)
        heads = range(RET_HEADS)
        zeros = jnp.zeros((dk, LANES), F32)
        qp = [z_ref[rs, LANES * p:LANES * (p + 1)] for p in range(RET_HEADS // 2)]
        kp = [z_ref[rs, hw // 2 + LANES * p:hw // 2 + LANES * (p + 1)] for p in range(RET_HEADS // 2)]
        v = [z_ref[rs, hw + LANES * h:hw + LANES * (h + 1)] for h in heads]
        s = [sret_scr[h] for h in heads]
        qh = [jnp.where((lane_c // dk) == (h % 2), qp[h // 2], 0.0) for h in heads]
        scores = [_dot1(qh[h], kp[h // 2], _NT) * dmat_ref[h] for h in heads]
        spad = [jnp.concatenate([s[h], zeros] if h % 2 == 0 else [zeros, s[h]], axis=0) for h in heads]
        cross = [_dot1(qp[h // 2], spad[h]) * rdec_ref[h] for h in heads]
        u = [_dot1(kp[h // 2] * cdec_ref[h], v[h], _TN) for h in heads]
        inner = [_dot1(scores[h], v[h]) for h in heads]
        for h in heads:
            e = h % 2
            out = inner[h] + cross[h]
            sret_scr[h] = s[h] * gc_ref[h] + u[h][dk * e:dk * (e + 1)]
            ms = jnp.mean(out * out, axis=-1, keepdims=True)
            o_ref[rs, LANES * h:LANES * (h + 1)] = (out * lax.rsqrt(ms + NORM_EPS)
                                                    * z_ref[rs, 2 * hw + LANES * h:2 * hw + LANES * (h + 1)])

    def chunk_body(ci, carry):
        r0 = pl.multiple_of(ci * c, c)
        if sample:
            sret_scr[...] = sret_in[ci]
            shg_scr[...] = shg_in[ci]
        hg_chunk(r0, _cumsum_rows(tri, z_ref[pl.ds(r0, c), 4 * hw:5 * hw]))
        ret_chunk(r0)
        if sample:
            sret_out[ci] = sret_scr[...]
            shg_out[ci] = shg_scr[...]
        return carry

    lax.fori_loop(0, n_chunks, chunk_body, 0)

    if not sample:
        @pl.when(pl.program_id(0) == pl.num_programs(0) - 1)
        def _():
            sret_out[...] = sret_scr[...]
            shg_out[...] = shg_scr[...]


def _ret_consts(c):
    heads = jnp.arange(RET_HEADS, dtype=F32)
    log_g = jnp.log1p(-jnp.exp2(-5.0 - heads))
    idx = jnp.arange(c, dtype=F32)
    rel = idx[:, None] - idx[None, :]
    dmat = jnp.exp(jnp.where(rel[None] >= 0, rel[None] * log_g[:, None, None], -jnp.inf))
    rdec = jnp.exp((idx[None, :] + 1.0) * log_g[:, None])
    cdec = jnp.exp((c - 1.0 - idx)[None, :] * log_g[:, None])
    gc = jnp.exp(c * log_g)
    bc = lambda a: jnp.broadcast_to(a[:, :, None], (RET_HEADS, c, LANES))
    return dmat, bc(rdec), bc(cdec), gc


def _ab_rec(z, hg_g, sret, shg_t, cfg, sample, d):
    n = z.shape[0]
    ng, npt = cfg["ng"], cfg["npt"]
    c = cfg["ts"] if sample else CHUNK
    dk, dv = d // 16, d // 8
    dmat, rdec, cdec, gc = _ret_consts(c)
    gcb = jnp.broadcast_to(gc[:, None, None], (RET_HEADS, dk, LANES))
    tri = np.kron(np.eye(c // SUB), np.tril(np.ones((SUB, SUB))))
    tri = jnp.asarray(tri, BF16)
    ones = jnp.ones((LANES, LANES), BF16)
    kern = functools.partial(_ab_rec_kernel, d=d, c=c, sample=sample)
    consts = [dmat, rdec, cdec, gcb, tri, hg_g.reshape(1, dv), ones]
    const_specs = [
        _const_spec(dmat.shape, (0, 0, 0)), _const_spec(rdec.shape, (0, 0, 0)),
        _const_spec(cdec.shape, (0, 0, 0)), _const_spec(gcb.shape, (0, 0, 0)),
        _const_spec(tri.shape, (0, 0)), _const_spec((1, dv), (0, 0)), _const_spec(ones.shape, (0, 0)),
    ]
    scratch = [pltpu.VMEM((RET_HEADS, dk, dv), F32), pltpu.VMEM((HG_HEADS, dv, dv), F32)]
    if sample:
        nb = sret.shape[0]
        nst = nb // ng
        return pl.pallas_call(
            kern,
            grid=(nst,),
            in_specs=[pl.BlockSpec((TM, z.shape[1]), lambda i: (npt + i, 0))] + const_specs + [
                pl.BlockSpec((ng, RET_HEADS, dk, dv), lambda i: (i, 0, 0, 0)),
                pl.BlockSpec((ng, HG_HEADS, dv, dv), lambda i: (i, 0, 0, 0)),
            ],
            out_specs=[
                pl.BlockSpec((TM, d), lambda i: (i, 0)),
                pl.BlockSpec((ng, RET_HEADS, dk, dv), lambda i: (i, 0, 0, 0)),
                pl.BlockSpec((ng, HG_HEADS, dv, dv), lambda i: (i, 0, 0, 0)),
            ],
            out_shape=[
                jax.ShapeDtypeStruct((nst * TM, d), F32),
                jax.ShapeDtypeStruct((nb, RET_HEADS, dk, dv), F32),
                jax.ShapeDtypeStruct((nb, HG_HEADS, dv, dv), F32),
            ],
            scratch_shapes=scratch,
            compiler_params=_params(("arbitrary",)),
            name="ab_rec_sample",
        )(z, *consts, sret, shg_t)
    return pl.pallas_call(
        kern,
        grid=(npt,),
        in_specs=[pl.BlockSpec((TM, z.shape[1]), lambda i: (i, 0))] + const_specs,
        out_specs=[
            pl.BlockSpec((TM, d), lambda i: (i, 0)),
            pl.BlockSpec((RET_HEADS, dk, dv), lambda i: (0, 0, 0)),
            pl.BlockSpec((HG_HEADS, dv, dv), lambda i: (0, 0, 0)),
        ],
        out_shape=[
            jax.ShapeDtypeStruct((npt * TM, d), F32),
            jax.ShapeDtypeStruct((RET_HEADS, dk, dv), F32),
            jax.ShapeDtypeStruct((HG_HEADS, dv, dv), F32),
        ],
        scratch_shapes=scratch,
        compiler_params=_params(("arbitrary",)),
        name="ab_rec_prompt",
    )(z, *consts)


def _post_kernel(*refs, d, rwkv, final):
    it = iter(refs)
    x_ref = next(it)
    if rwkv:
        y_ref, bonus_ref, gate_ref, lng_ref, lnb_ref, seg_ref = (next(it) for _ in range(6))
    else:
        o_ref_in = next(it)
    mod_ref, wout_ref, g2_ref, w1_ref, w2_ref = (next(it) for _ in range(5))
    if final:
        fg_ref = next(it)
    out_ref = next(it)

    x = x_ref[...]
    if rwkv:
        y = jnp.concatenate([y_ref[p] for p in range(y_ref.shape[0])], axis=-1)
        seg = seg_ref[...]
        inv_n = 1.0 / RW_N
        mean = _segsum(y, seg) * inv_n
        yc = y - mean
        var = _segsum(yc * yc, seg) * inv_n
        o = yc * lax.rsqrt(var + RW_LN_EPS) * lng_ref[0] + lnb_ref[0]
        o = (o + bonus_ref[...]) * gate_ref[...]
    else:
        o = o_ref_in[...]
    gt1 = mod_ref[0, :, :, 2 * d:3 * d]
    sh2 = mod_ref[0, :, :, 3 * d:4 * d]
    sc2 = mod_ref[0, :, :, 4 * d:5 * d]
    gt2 = mod_ref[0, :, :, 5 * d:6 * d]
    x1 = _gate_add(x, gt1, _dg(o.astype(BF16), wout_ref[0], _NN))
    hb = _modulate(_rms(x1, g2_ref[0]), sc2, sh2).astype(BF16)
    ff = w1_ref.shape[2]
    acc = jnp.zeros_like(x1)
    for j in range(ff // d):
        u = jnp.maximum(_dg(hb, w1_ref[0, :, j * d:(j + 1) * d], _NN), 0.0)
        acc = acc + _dg((u * u).astype(BF16), w2_ref[0, j * d:(j + 1) * d, :], _NN)
    x2 = _gate_add(x1, gt2, acc)
    if final:
        x2 = _rms(x2, fg_ref[...])
    out_ref[...] = x2


def _post(x, mixer_in, modt, wout, g2, w1, w2, final_g, layer, m, cfg, rwkv, final):
    n, d = x.shape
    ng, npt = cfg["ng"], cfg["npt"]
    nt = n // TM
    kern = functools.partial(_post_kernel, d=d, rwkv=rwkv, final=final)
    tok = pl.BlockSpec((TM, d), lambda i: (i, 0))
    args = [x]
    specs = [tok]
    if rwkv:
        y, bonus, gate, lng, lnb, seg = mixer_in
        args += [y, bonus, gate, lng, lnb, seg]
        specs += [pl.BlockSpec((y.shape[0], TM, LANES), lambda i: (0, i, 0)), tok, tok,
                  _const_spec((1, 1, d), (m, 0, 0)), _const_spec((1, 1, d), (m, 0, 0)),
                  _const_spec((d, d), (0, 0))]
    else:
        args += [mixer_in]
        specs += [tok]
    args += [modt, wout, g2, w1, w2]
    specs += [
        pl.BlockSpec((1, ng, 1, 6 * d), lambda i: (layer, jnp.maximum(i - npt + 1, 0), 0, 0)),
        _const_spec((1, d, d), (m, 0, 0)),
        _const_spec((1, 1, d), (layer, 0, 0)),
        _const_spec((1, d, w1.shape[2]), (layer, 0, 0)),
        _const_spec((1, w2.shape[1], d), (layer, 0, 0)),
    ]
    if final:
        args += [final_g]
        specs += [_const_spec((1, d), (0, 0))]
    return pl.pallas_call(
        kern,
        grid=(nt,),
        in_specs=specs,
        out_specs=tok,
        out_shape=jax.ShapeDtypeStruct((n, d), F32),
        compiler_params=_params(("arbitrary",)),
        name="post_rwkv" if rwkv else "post_ab",
    )(*args)


def _rw_in_kernel(*refs, d, ts, npt, vres):
    it = iter(refs)
    x_ref, mod_ref, g_ref, shift_ref, mu_ref, wrkv_ref = (next(it) for _ in range(6))
    w0_ref, w1_ref, w2_ref, a0_ref, a1_ref, a2_ref, g1_ref, g2_ref = (next(it) for _ in range(8))
    kk_ref, ka_ref, rk_ref, seg_ref = (next(it) for _ in range(4))
    if vres:
        vf_ref, v0_ref, v1_ref, v2_ref = (next(it) for _ in range(4))
    r_out, ld_out, k_out, v_out, kk_out, kka_out, gate_out, bonus_out, hlast_out = (next(it) for _ in range(9))
    if not vres:
        vtok_out = next(it)
    h_scr, carry_scr = next(it), next(it)

    i = pl.program_id(0)
    rows = x_ref.shape[0]
    ng = rows // ts

    @pl.when(i == 0)
    def _():
        carry_scr[...] = jnp.zeros_like(carry_scr)

    x = x_ref[...]
    h = _modulate(_rms(x, g_ref[0]), mod_ref[0, :, :, d:2 * d], mod_ref[0, :, :, 0:d])
    for p in range(d // LANES):
        h_scr[p] = h[:, LANES * p:LANES * (p + 1)]
        hlast_out[:, LANES * p:LANES * (p + 1)] = h_scr[p, pl.ds(ts - 1, ng, stride=ts), :]
    rolled = pltpu.roll(h, 1, 0)
    row = lax.broadcasted_iota(jnp.int32, (rows, d), 0)
    s_i = jnp.where(i >= npt, 1, 0)
    first = ((row % ts) == 0) & ((row * (1 - s_i)) == 0)
    seq_prev = jnp.broadcast_to(shift_ref[...], (ng, ts, d)).reshape(rows, d)
    prev_first = jnp.where((row * 0 + s_i) == 1, seq_prev, jnp.broadcast_to(carry_scr[0:1, :], (rows, d)))
    prev = jnp.where(first, prev_first, rolled)
    carry_scr[0:1, :] = h[rows - 1:rows, :]

    xx = prev - h
    mix = lambda j: (h + xx * mu_ref[0, j:j + 1, :]).astype(BF16)
    xr, xw, xk, xv, xa, xg = (mix(j) for j in range(6))
    r = _dg(xr, wrkv_ref[0, 0], _NN)
    k = _dg(xk, wrkv_ref[0, 1], _NN)
    v = _dg(xv, wrkv_ref[0, 2], _NN)
    if vres:
        lv = v0_ref[0] + _dot1(_dg(xv, v1_ref[0], _NN), v2_ref[0])
        v = v + (vf_ref[...] - v) * _sigmoid(lv)
    else:
        vtok_out[...] = v
    wl = w0_ref[0] + _dot1(jnp.tanh(_dg(xw, w1_ref[0], _NN)), w2_ref[0])
    nwl = -wl
    w = -(jnp.maximum(nwl, 0.0) + jnp.log1p(jnp.exp(-jnp.abs(nwl)))) - 0.5
    ld = -jnp.exp(w)
    a = _sigmoid(a0_ref[0] + _dot1(_dg(xa, a1_ref[0], _NN), a2_ref[0]))
    gate = _dot1(_sigmoid(_dg(xg, g1_ref[0], _NN)), g2_ref[0])
    seg = seg_ref[...]
    kk = k * kk_ref[0]
    kk = kk / jnp.maximum(jnp.sqrt(_segsum(kk * kk, seg)), 1e-12)
    km = k * (1.0 + (a - 1.0) * ka_ref[0])
    bonus = _segsum(r * km * rk_ref[0], seg) * v
    gate_out[...] = gate
    bonus_out[...] = bonus
    kka = kk * a
    for p in range(d // LANES):
        cs = slice(LANES * p, LANES * (p + 1))
        r_out[p] = r[:, cs]
        ld_out[p] = ld[:, cs]
        k_out[p] = km[:, cs]
        v_out[p] = v[:, cs]
        kk_out[p] = kk[:, cs]
        kka_out[p] = kka[:, cs]


def _rw_in(x, modt, g, shift_rows, W, seg, v_first, layer, m, cfg):
    n, d = x.shape
    ng, npt, ts = cfg["ng"], cfg["npt"], cfg["ts"]
    nt = n // TM
    npair = d // LANES
    vres = v_first is not None
    kern = functools.partial(_rw_in_kernel, d=d, ts=ts, npt=npt, vres=vres)
    tok = pl.BlockSpec((TM, d), lambda i: (i, 0))
    vec = lambda: _const_spec((1, 1, d), (m, 0, 0))
    lora = lambda a: _const_spec((1,) + a.shape[1:], (m, 0, 0))
    args = [x, modt, g, shift_rows, W["rw_mu"], W["rw_w_rkv"],
            W["rw_w0"], W["rw_w1"], W["rw_w2"], W["rw_a0"], W["rw_a1"], W["rw_a2"], W["rw_g1"], W["rw_g2"],
            W["rw_k_k"], W["rw_k_a"], W["rw_r_k"], seg]
    specs = [
        tok,
        pl.BlockSpec((1, ng, 1, 6 * d), lambda i: (layer, jnp.maximum(i - npt + 1, 0), 0, 0)),
        _const_spec((1, 1, d), (layer, 0, 0)),
        pl.BlockSpec((ng, 1, d), lambda i: (jnp.maximum(i - npt, 0), 0, 0)),
        _const_spec((1, 6, d), (m, 0, 0)),
        _const_spec((1, 3, d, d), (m, 0, 0, 0)),
        vec(), lora(W["rw_w1"]), lora(W["rw_w2"]), vec(), lora(W["rw_a1"]), lora(W["rw_a2"]),
        lora(W["rw_g1"]), lora(W["rw_g2"]), vec(), vec(), vec(), _const_spec((d, d), (0, 0)),
    ]
    if vres:
        args += [v_first, W["rw_v0"], W["rw_v1"], W["rw_v2"]]
        specs += [tok, _const_spec((1, 1, d), (m - 1, 0, 0)),
                  _const_spec((1,) + W["rw_v1"].shape[1:], (m - 1, 0, 0)),
                  _const_spec((1,) + W["rw_v2"].shape[1:], (m - 1, 0, 0))]
    pm = pl.BlockSpec((npair, TM, LANES), lambda i: (0, i, 0))
    pm_shape = jax.ShapeDtypeStruct((npair, n, LANES), F32)
    tok_shape = jax.ShapeDtypeStruct((n, d), F32)
    out_specs = [pm] * 6 + [tok, tok, pl.BlockSpec((ng, d), lambda i: (i, 0))]
    out_shape = [pm_shape] * 6 + [tok_shape, tok_shape, jax.ShapeDtypeStruct((nt * ng, d), F32)]
    if not vres:
        out_specs.append(tok)
        out_shape.append(tok_shape)
    return pl.pallas_call(
        kern,
        grid=(nt,),
        in_specs=specs,
        out_specs=out_specs,
        out_shape=out_shape,
        scratch_shapes=[pltpu.VMEM((npair, TM, LANES), F32), pltpu.VMEM((8, d), F32)],
        compiler_params=_params(("arbitrary",)),
        name="rw_in",
    )(*args)


def _rw_rec_kernel(*refs, c, sample):
    if sample:
        (r_ref, ld_ref, k_ref, v_ref, kk_ref, kka_ref, tri_ref, ms_ref, mi_ref, eye_ref, s_in, y_alias,
         y_ref, s_out, s_scr) = refs
    else:
        (r_ref, ld_ref, k_ref, v_ref, kk_ref, kka_ref, tri_ref, ms_ref, mi_ref, eye_ref,
         y_ref, s_out, s_scr) = refs
    npair, rows, _ = r_ref.shape
    n_chunks = rows // c
    n = 2 * c

    if not sample:
        @pl.when(pl.program_id(0) == 0)
        def _():
            s_scr[...] = jnp.zeros_like(s_scr)

    lane = lax.broadcasted_iota(jnp.int32, (c, LANES), 1)
    left = lane < RW_N
    tri = tri_ref[...]
    mask_strict = ms_ref[...]
    mask_incl = mi_ref[...]
    eye = eye_ref[...]
    steps = int(np.log2(c)) - 1

    def stack(xv):
        return jnp.concatenate([jnp.where(left, xv, 0.0), jnp.where(left, 0.0, xv)], axis=0)

    pairs = range(npair)

    def chunk_body(ci, carry):
        rs = pl.ds(pl.multiple_of(ci * c, c), c)
        s = [s_in[p, ci] if sample else s_scr[p] for p in pairs]
        ld_all = jnp.concatenate([ld_ref[p, rs, :] for p in pairs], axis=1)
        lc_all = _cumsum_rows(tri, ld_all)
        ur, bk, bk_end, vst, ltot = [], [], [], [], []
        for p in pairs:
            lc = lc_all[:, LANES * p:LANES * (p + 1)]
            ld = ld_all[:, LANES * p:LANES * (p + 1)]
            lt = lc[c - 1:c, :]
            g_inv = jnp.exp(-lc)
            g_end = jnp.exp(lt - lc)
            k = k_ref[p, rs, :]
            kka = kka_ref[p, rs, :]
            ur.append(jnp.concatenate([stack(-kk_ref[p, rs, :] * jnp.exp(lc - ld)),
                                       stack(r_ref[p, rs, :] * jnp.exp(lc))], axis=0))
            bk.append(jnp.concatenate([stack(kka * g_inv), stack(k * g_inv)], axis=0))
            bk_end.append(jnp.concatenate([stack(kka * g_end), stack(k * g_end)], axis=0))
            vst.append(stack(v_ref[p, rs, :]))
            ltot.append(lt)
        big = [_dotp(ur[p], bk[p], _NT, P_BIG) for p in pairs]
        m_ub = [big[p][0:n, 0:n] * mask_strict for p in pairs]
        m_uk = [big[p][0:n, n:2 * n] * mask_strict for p in pairs]
        a_r = [jnp.concatenate([big[p][n:2 * n, 0:n] * mask_incl, big[p][n:2 * n, n:2 * n] * mask_incl], axis=1)
               for p in pairs]
        t = [eye + m_ub[p] for p in pairs]
        mp = m_ub
        for _ in range(steps):
            mp = [_dot1(mp[p], mp[p]) for p in pairs]
            t = [t[p] + _dot1(t[p], mp[p]) for p in pairs]
        res = [eye - t[p] + _dot3(m_ub[p], t[p]) for p in pairs]
        t = [t[p] + _dot1(t[p], res[p]) for p in pairs]
        urs = [_dotp(ur[p], s[p], _NT, P_ST) for p in pairs]
        mv = [_dotp(m_uk[p], vst[p], _NN, P_ST) for p in pairs]
        e = [_dotp(t[p], urs[p][0:n] + mv[p], _NN, P_ST) for p in pairs]
        ev = [jnp.concatenate([e[p], vst[p]], axis=0) for p in pairs]
        yst = [urs[p][n:2 * n] + _dotp(a_r[p], ev[p], _NN, P_ST) for p in pairs]
        upd = [_dotp(ev[p], bk_end[p], _TN, P_ST) for p in pairs]
        for p in pairs:
            y_ref[p, rs, :] = yst[p][0:c] + yst[p][c:n]
            s_new = s[p] * jnp.exp(ltot[p]) + upd[p]
            if sample:
                s_out[p, ci] = s_new
            else:
                s_scr[p] = s_new
        return carry

    lax.fori_loop(0, n_chunks, chunk_body, 0)

    if not sample:
        @pl.when(pl.program_id(0) == pl.num_programs(0) - 1)
        def _():
            s_out[...] = s_scr[...]


def _rw_rec(seqs, s_bd, y_prev, cfg, sample):
    npair, n, _ = seqs[0].shape
    ng, npt = cfg["ng"], cfg["npt"]
    c = cfg["ts"] if sample else CHUNK
    tri = jnp.asarray(np.tril(np.ones((c, c))), BF16)
    blk = np.kron(np.eye(2), np.ones((c, c)))
    big_tril = np.tril(np.ones((2 * c, 2 * c)))
    mask_incl = jnp.asarray(blk * big_tril, F32)
    mask_strict = jnp.asarray(blk * (big_tril - np.eye(2 * c)), F32)
    eye = jnp.asarray(np.eye(2 * c), F32)
    consts = [tri, mask_strict, mask_incl, eye]
    const_specs = [_const_spec(a.shape, (0, 0)) for a in consts]
    kern = functools.partial(_rw_rec_kernel, c=c, sample=sample)
    y_shape = jax.ShapeDtypeStruct((npair, n, LANES), F32)
    scratch = [pltpu.VMEM((npair, LANES, LANES), F32)]
    if sample:
        nb = s_bd.shape[1]
        nst = nb // ng
        tokp = pl.BlockSpec((npair, TM, LANES), lambda i: (0, npt + i, 0))
        sspec = pl.BlockSpec((npair, ng, LANES, LANES), lambda i: (0, i, 0, 0))
        return pl.pallas_call(
            kern,
            grid=(nst,),
            in_specs=[tokp] * 6 + const_specs + [sspec, pl.BlockSpec(memory_space=pl.ANY)],
            out_specs=[tokp, sspec],
            out_shape=[y_shape, jax.ShapeDtypeStruct((npair, nb, LANES, LANES), F32)],
            scratch_shapes=scratch,
            input_output_aliases={len(seqs) + len(consts) + 1: 0},
            compiler_params=_params(("arbitrary",)),
            name="rw_rec_sample",
        )(*seqs, *consts, s_bd, y_prev)
    tokp = pl.BlockSpec((npair, TM, LANES), lambda i: (0, i, 0))
    return pl.pallas_call(
        kern,
        grid=(npt,),
        in_specs=[tokp] * 6 + const_specs,
        out_specs=[tokp, pl.BlockSpec((npair, LANES, LANES), lambda i: (0, 0, 0))],
        out_shape=[y_shape, jax.ShapeDtypeStruct((npair, LANES, LANES), F32)],
        scratch_shapes=scratch,
        compiler_params=_params(("arbitrary",)),
        name="rw_rec_prompt",
    )(*seqs, *consts)


def kernel(x_prompt, x_sample, state_ret, state_hgrn, state_wkv, state_shift, c_prompt, c_sample, mod_w, mod_b, norm_mix_g, norm_mlp_g, final_g, mlp_w1, mlp_w2, ab_w_in, ab_w_out, hg_lb, hg_norm_g, rw_mu, rw_w_rkv, rw_w0, rw_w1, rw_w2, rw_a0, rw_a1, rw_a2, rw_v0, rw_v1, rw_v2, rw_g1, rw_g2, rw_k_k, rw_k_a, rw_r_k, rw_ln_g, rw_ln_b, rw_w_out):
    bp, tp, d = x_prompt.shape
    bs, ts, _ = x_sample.shape
    depth = mod_w.shape[0]
    n_ab = ab_w_in.shape[0]
    n_c = rw_w_rkv.shape[0]
    assert bp == 1 and d == 1024 and tp % TM == 0 and tp % CHUNK == 0
    assert (bs * ts) % TM == 0 and TM % ts == 0 and ts % SUB == 0 and ts <= CHUNK
    np_rows, ns_rows = bp * tp, bs * ts
    n = np_rows + ns_rows
    ng = TM // ts
    assert bs % ng == 0
    cfg = dict(ng=ng, npt=np_rows // TM, ts=ts)
    dk, dv = d // 16, d // 8
    npair = d // LANES
    nh = d // RW_N

    c_all = jnp.concatenate([c_prompt, c_sample], axis=0)
    pad = (-c_all.shape[0]) % 8
    c_all = jnp.pad(c_all, ((0, pad), (0, 0)))
    mod = _modulation(c_all, mod_w, mod_b)
    modt = jnp.concatenate([jnp.broadcast_to(mod[:, 0:1], (depth, ng, 6 * d)), mod[:, bp:bp + bs]], axis=1)
    modt = modt.reshape(depth, ng + bs, 1, 6 * d)

    x = jnp.concatenate([x_prompt.reshape(np_rows, d), x_sample.reshape(ns_rows, d)], axis=0)

    pos = jnp.concatenate([jnp.arange(tp, dtype=F32), jnp.tile(PAST_LEN + jnp.arange(ts, dtype=F32), bs)])
    half = dk // 2
    inv = ROPE_BASE ** (-jnp.arange(half, dtype=F32) / half)
    ang = pos[:, None] * inv[None, :]
    cos_t = jnp.tile(jnp.cos(ang), (1, 4))
    sin_t = jnp.tile(jnp.concatenate([-jnp.sin(ang), jnp.sin(ang)], axis=1), (1, 2))

    lb_all = jnp.cumsum(jax.nn.softmax(hg_lb.astype(F32), axis=0), axis=0)
    lb_all = lb_all - lb_all[:1]
    lbc = jnp.stack([jnp.log(lb_all), jnp.log1p(-lb_all), 1.0 - lb_all], axis=1)
    lbc = jnp.pad(lbc, ((0, 0), (0, 5), (0, 0))).reshape(n_ab * 8, d // 2)

    bf = lambda a: a.astype(BF16)
    ab_w_in_b, ab_w_out_b = bf(ab_w_in), bf(ab_w_out)
    mlp_w1_b, mlp_w2_b = bf(mlp_w1), bf(mlp_w2)
    rw_w_out_b = bf(rw_w_out)
    vec = lambda a: a.reshape(a.shape[0], 1, d)
    W = dict(rw_mu=rw_mu, rw_w_rkv=bf(rw_w_rkv), rw_w0=vec(rw_w0), rw_w1=bf(rw_w1), rw_w2=bf(rw_w2),
             rw_a0=vec(rw_a0), rw_a1=bf(rw_a1), rw_a2=bf(rw_a2), rw_g1=bf(rw_g1), rw_g2=bf(rw_g2),
             rw_k_k=vec(rw_k_k), rw_k_a=vec(rw_k_a), rw_r_k=vec(rw_r_k), rw_v0=vec(rw_v0), rw_v1=bf(rw_v1),
             rw_v2=bf(rw_v2))
    seg = jnp.asarray(np.kron(np.eye(nh), np.ones((RW_N, RW_N))), BF16)
    g_mix = norm_mix_g.reshape(depth, 1, d)
    g_mlp = norm_mlp_g.reshape(depth, 1, d)
    fin_g = final_g.reshape(1, d)

    ret_p, ret_s, hg_p, hg_s, wkv_p, wkv_s, sh_p, sh_s = ([] for _ in range(8))
    v_first = None
    for layer in range(depth):
        m = layer // 2
        final = layer == depth - 1
        if layer % 2 == 0:
            z = _ab_in(x, modt, g_mix, ab_w_in_b, cos_t, sin_t, lbc, layer, m, cfg)
            o_p, r_p, h_p = _ab_rec(z, hg_norm_g[m], None, None, cfg, False, d)
            o_s, r_s, h_s = _ab_rec(z, hg_norm_g[m], state_ret[m], jnp.swapaxes(state_hgrn[m], -1, -2), cfg, True, d)
            ret_p.append(r_p[None])
            hg_p.append(jnp.swapaxes(h_p, -1, -2)[None])
            ret_s.append(r_s)
            hg_s.append(jnp.swapaxes(h_s, -1, -2))
            o = jnp.concatenate([o_p, o_s], axis=0)
            x = _post(x, o, modt, ab_w_out_b, g_mlp, mlp_w1_b, mlp_w2_b, fin_g, layer, m, cfg, False, final)
        else:
            shift_rows = state_shift[m].reshape(bs, 1, d)
            outs = _rw_in(x, modt, g_mix, shift_rows, W, seg, v_first, layer, m, cfg)
            seqs, gate, bonus, hlast = outs[0:6], outs[6], outs[7], outs[8]
            if v_first is None:
                v_first = outs[9]
            y_p, s_p = _rw_rec(seqs, None, None, cfg, False)
            sw = state_wkv[m].reshape(bs, npair, 2, RW_N, RW_N)
            zero = jnp.zeros_like(sw[:, :, 0])
            s_bd = jnp.concatenate([jnp.concatenate([sw[:, :, 0], zero], axis=-1),
                                    jnp.concatenate([zero, sw[:, :, 1]], axis=-1)], axis=-2)
            y, s_s = _rw_rec(seqs, jnp.swapaxes(s_bd, 0, 1), y_p, cfg, True)
            unbd = lambda a: jnp.stack([a[..., :RW_N, :RW_N], a[..., RW_N:, RW_N:]], axis=-3)
            wkv_p.append(unbd(s_p).reshape(1, 1, nh, RW_N, RW_N))
            wkv_s.append(unbd(jnp.swapaxes(s_s, 0, 1)).reshape(bs, nh, RW_N, RW_N))
            sh_p.append(hlast[cfg["npt"] * ng - 1][None, None])
            sh_s.append(hlast[cfg["npt"] * ng:][None])
            x = _post(x, (y, bonus, gate, vec(rw_ln_g), vec(rw_ln_b), seg), modt, rw_w_out_b, g_mlp, mlp_w1_b, mlp_w2_b,
                      fin_g, layer, m, cfg, True, final)

    y_p = x[:np_rows].reshape(bp, tp, d)
    y_s = x[np_rows:].reshape(bs, ts, d)
    return (y_p, y_s,
            jnp.stack(ret_p), jnp.stack(ret_s), jnp.stack(hg_p), jnp.stack(hg_s),
            jnp.concatenate(wkv_p, axis=0), jnp.stack(wkv_s),
            jnp.concatenate(sh_p, axis=0), jnp.concatenate(sh_s, axis=0))
```

```python
import functools

import numpy as np
import jax
import jax.numpy as jnp
from jax import lax
from jax.experimental import pallas as pl
from jax.experimental.pallas import tpu as pltpu

F32 = jnp.float32
BF16 = jnp.bfloat16

CHUNK = 64
PAST_LEN = 2048
ROPE_BASE = 10000.0
RET_HEADS = 4
HG_HEADS = 4
RW_N = 64
NORM_EPS = 1e-6
RW_LN_EPS = 64e-5

LANES = 128
SUB = 16
TM = 256
VMEM_LIMIT = 56 * 1024 * 1024


def _sigmoid(x):
    return 1.0 / (1.0 + jnp.exp(-x))


def _split2(x):
    hi = x.astype(BF16)
    lo = (x - hi.astype(F32)).astype(BF16)
    return hi, lo


def _split3(x):
    a = x.astype(BF16)
    r = x - a.astype(F32)
    b = r.astype(BF16)
    c = (r - b.astype(F32)).astype(BF16)
    return a, b, c


_NN = (((1,), (0,)), ((), ()))
_NT = (((1,), (1,)), ((), ()))
_TN = (((0,), (0,)), ((), ()))


def _dg(a, b, dims):
    return lax.dot_general(a, b, dims, preferred_element_type=F32)


def _dot1(a, b, dims=_NN):
    return _dg(a.astype(BF16), b.astype(BF16), dims)


def _dot3(a, b, dims=_NN):
    ah, al = _split2(a)
    bh, bl = _split2(b)
    return _dg(ah, bh, dims) + _dg(ah, bl, dims) + _dg(al, bh, dims)


def _dotp(a, b, dims, passes):
    return _dot1(a, b, dims) if passes == 1 else _dot3(a, b, dims)


P_BIG = 1
P_ST = 1


def _cumsum_rows(tri_bf16, x):
    a, b, c = _split3(x)
    return _dg(tri_bf16, a, _NN) + _dg(tri_bf16, b, _NN) + _dg(tri_bf16, c, _NN)


def _segsum(x, seg_bf16):
    return _dg(x.astype(BF16), seg_bf16, _NN)


def _rms(x, g):
    ms = jnp.mean(x * x, axis=-1, keepdims=True)
    return x * lax.rsqrt(ms + NORM_EPS) * g


def _modulate(y, sc, sh):
    rows, d = y.shape
    ng = sc.shape[0]
    y3 = y.reshape(ng, rows // ng, d)
    return (y3 * (1.0 + sc) + sh).reshape(rows, d)


def _gate_add(x, gt, out):
    rows, d = x.shape
    ng = gt.shape[0]
    return (x.reshape(ng, rows // ng, d) + gt * out.reshape(ng, rows // ng, d)).reshape(rows, d)


def _rows_specs(x, npt):
    if isinstance(x, tuple):
        d = x[0].shape[1]
        return list(x), [pl.BlockSpec((TM, d), lambda i: (jnp.minimum(i, npt - 1), 0)),
                         pl.BlockSpec((TM, d), lambda i: (jnp.maximum(i - npt, 0), 0))]
    return [x], [pl.BlockSpec((TM, x.shape[1]), lambda i: (i, 0))]


def _rows_value(refs, npt):
    if len(refs) == 1:
        return refs[0][...]
    p_ref, s_ref = refs
    rows, d = p_ref.shape
    s_i = jnp.where(pl.program_id(0) >= npt, 1, 0)
    is_sample = (lax.broadcasted_iota(jnp.int32, (rows, LANES), 0) * 0 + s_i) == 1
    return jnp.concatenate([jnp.where(is_sample, s_ref[:, LANES * p:LANES * (p + 1)],
                                      p_ref[:, LANES * p:LANES * (p + 1)]) for p in range(d // LANES)], axis=-1)


def _const_spec(block, index):
    return pl.BlockSpec(block, lambda *_: index, pipeline_mode=pl.Buffered(1))


def _params(sem):
    return pltpu.CompilerParams(dimension_semantics=sem, vmem_limit_bytes=VMEM_LIMIT)


def _mod_kernel(c_ref, w_ref, b_ref, o_ref):
    c = c_ref[...]
    s = c * _sigmoid(c)
    o_ref[0] = _dot1(s, w_ref[0]) + b_ref[0]


def _modulation(c_all, mod_w, mod_b):
    depth, d, d6 = mod_w.shape
    rows = c_all.shape[0]
    nt = d6 // d
    return pl.pallas_call(
        _mod_kernel,
        grid=(depth, nt),
        in_specs=[
            pl.BlockSpec((rows, d), lambda l, j: (0, 0)),
            pl.BlockSpec((1, d, d), lambda l, j: (l, 0, j)),
            pl.BlockSpec((1, 1, d), lambda l, j: (l, 0, j)),
        ],
        out_specs=pl.BlockSpec((1, rows, d), lambda l, j: (l, 0, j)),
        out_shape=jax.ShapeDtypeStruct((depth, rows, d6), F32),
        compiler_params=_params(("arbitrary", "arbitrary")),
        name="modulation",
    )(c_all, mod_w, mod_b.reshape(depth, 1, d6))


def _ab_in_kernel(*refs, d, npt, dk_scale):
    mod_ref, g_ref, w_ref, cos_ref, sin_ref, lb_ref, o_ref = refs[-7:]
    x = _rows_value(refs[:-7], npt)
    h = _modulate(_rms(x, g_ref[0]), mod_ref[0, :, :, d:2 * d], mod_ref[0, :, :, 0:d])
    hb = h.astype(BF16)
    hw = d // 2

    def proj(a):
        return _dg(hb, w_ref[0, :, a:a + hw], _NN)

    qk = proj(0)
    lane = lax.broadcasted_iota(jnp.int32, (x.shape[0], LANES), 1)
    first_half = (lane % 64) < 32
    cos = cos_ref[...]
    sin = sin_ref[...]
    for s in range(hw // LANES):
        xs = qk[:, LANES * s:LANES * (s + 1)]
        sw = jnp.where(first_half, pltpu.roll(xs, 96, 1), pltpu.roll(xs, 32, 1))
        rot = xs * cos + sw * sin
        if s < hw // (2 * LANES):
            rot = rot * dk_scale
        o_ref[:, LANES * s:LANES * (s + 1)] = rot
    o_ref[:, hw:2 * hw] = proj(hw)
    z = proj(2 * hw)
    o_ref[:, 2 * hw:3 * hw] = z * _sigmoid(z)
    z = proj(3 * hw)
    o_ref[:, 3 * hw:4 * hw] = z * _sigmoid(z)
    z = proj(4 * hw)
    log_lb = lb_ref[0:1, :]
    log1m_lb = lb_ref[1:2, :]
    ls = jnp.minimum(z, 0.0) - jnp.log1p(jnp.exp(-jnp.abs(z)))
    c = log1m_lb + ls
    m = jnp.maximum(log_lb, c)
    o_ref[:, 4 * hw:5 * hw] = m + jnp.log1p(jnp.exp(-jnp.abs(log_lb - c)))
    o_ref[:, 5 * hw:6 * hw] = c - z
    o_ref[:, 6 * hw:7 * hw] = proj(5 * hw)
    o_ref[:, 7 * hw:8 * hw] = _sigmoid(proj(6 * hw))


def _ab_in(x, modt, g, w_in, cos_t, sin_t, lbc, layer, m, cfg):
    ng, npt, n = cfg["ng"], cfg["npt"], cfg["n"]
    x_args, x_specs = _rows_specs(x, npt)
    d = x_args[0].shape[1]
    nt = n // TM
    zw = 4 * d
    kern = functools.partial(_ab_in_kernel, d=d, npt=npt, dk_scale=float((d // 16) ** -0.5))
    return pl.pallas_call(
        kern,
        grid=(nt,),
        in_specs=x_specs + [
            pl.BlockSpec((1, ng, 1, 6 * d), lambda i: (layer, jnp.maximum(i - npt + 1, 0), 0, 0)),
            _const_spec((1, 1, d), (layer, 0, 0)),
            _const_spec((1, d, w_in.shape[2]), (m, 0, 0)),
            pl.BlockSpec((TM, LANES), lambda i: (i, 0)),
            pl.BlockSpec((TM, LANES), lambda i: (i, 0)),
            _const_spec((8, d // 2), (m, 0)),
        ],
        out_specs=pl.BlockSpec((TM, zw), lambda i: (i, 0)),
        out_shape=jax.ShapeDtypeStruct((n, zw), F32),
        compiler_params=_params(("arbitrary",)),
        name="ab_in",
    )(*x_args, modt, g, w_in, cos_t, sin_t, lbc)


def _ab_rec_kernel(*refs, d, c, sample):
    if sample:
        (z_ref, dmat_ref, rdec_ref, cdec_ref, gc_ref, tri_ref, hgg_ref, ones_ref, sret_in, shg_in,
         o_ref, sret_out, shg_out, sret_scr, shg_scr) = refs
    else:
        (z_ref, dmat_ref, rdec_ref, cdec_ref, gc_ref, tri_ref, hgg_ref, ones_ref,
         o_ref, sret_out, shg_out, sret_scr, shg_scr) = refs
    hw = d // 2
    rows = z_ref.shape[0]
    n_chunks = rows // c
    dk = d // 16

    if not sample:
        @pl.when(pl.program_id(0) == 0)
        def _():
            sret_scr[...] = jnp.zeros_like(sret_scr)
            shg_scr[...] = jnp.zeros_like(shg_scr)

    lane_c = lax.broadcasted_iota(jnp.int32, (c, LANES), 1)
    row_h = lax.broadcasted_iota(jnp.int32, (SUB // 2, LANES), 0)
    ones = ones_ref[...]
    tri = tri_ref[...]

    def hg_chunk(r0, b):
        col = lambda g, h: slice(g * hw + LANES * h, g * hw + LANES * (h + 1))
        units = [(i, h) for i in range(c // SUB) for h in range(HG_HEADS)]
        rs = {i: pl.ds(r0 + SUB * i, SUB) for i in range(c // SUB)}
        q = {u: z_ref[rs[u[0]], col(3, u[1])] for u in units}
        v = {u: z_ref[rs[u[0]], col(6, u[1])] for u in units}
        bs = {(i, h): b[SUB * i:SUB * (i + 1), LANES * h:LANES * (h + 1)] for (i, h) in units}
        bl = {u: bs[u][SUB - 1:SUB, :] for u in units}
        w = {u: bs[u] - z_ref[rs[u[0]], col(5, u[1])] for u in units}
        upd = {u: _dot1(v[u], jnp.exp(bl[u] - w[u]), _TN) for u in units}
        hs = SUB // 2
        r = {}
        for u in units:
            q_lo, q_hi = q[u][0:hs], q[u][hs:SUB]
            b_lo, b_hi = bs[u][0:hs], bs[u][hs:SUB]
            p_lo, p_hi = [], []
            for j in range(SUB):
                wj = w[u][j:j + 1, :]
                if j < hs:
                    p_lo.append(jnp.where(row_h >= j, q_lo * jnp.exp(b_lo - wj), 0.0))
                    p_hi.append(q_hi * jnp.exp(b_hi - wj))
                else:
                    p_hi.append(jnp.where(row_h >= j - hs, q_hi * jnp.exp(b_hi - wj), 0.0))
            r[u] = _dot1(jnp.concatenate(p_lo + p_hi, axis=0), ones)
        inner = {}
        for u in units:
            acc_lo = r[u][0:hs] * v[u][0:1, :]
            for j in range(1, hs):
                acc_lo = acc_lo + r[u][hs * j:hs * (j + 1)] * v[u][j:j + 1, :]
            base = hs * hs
            acc_hi = r[u][base:base + hs] * v[u][0:1, :]
            for j in range(1, SUB):
                acc_hi = acc_hi + r[u][base + hs * j:base + hs * (j + 1)] * v[u][j:j + 1, :]
            inner[u] = jnp.concatenate([acc_lo, acc_hi], axis=0)
        st = {h: shg_scr[h] for h in range(HG_HEADS)}
        for u in units:
            i, h = u
            out = inner[u] + _dot1(q[u] * jnp.exp(bs[u]), st[h], _NT)
            st[h] = st[h] * jnp.exp(bl[u]) + upd[u]
            ms = jnp.mean(out * out, axis=-1, keepdims=True)
            o_ref[rs[i], col(1, h)] = out * lax.rsqrt(ms + NORM_EPS) * hgg_ref[...] * z_ref[rs[i], col(7, h)]
        for h in range(HG_HEADS):
            shg_scr[h] = st[h]

    def ret_chunk(r0):
        rs = pl.ds(r0, c)
        heads = range(RET_HEADS)
        zeros = jnp.zeros((dk, LANES), F32)
        qp = [z_ref[rs, LANES * p:LANES * (p + 1)] for p in range(RET_HEADS // 2)]
        kp = [z_ref[rs, hw // 2 + LANES * p:hw // 2 + LANES * (p + 1)] for p in range(RET_HEADS // 2)]
        v = [z_ref[rs, hw + LANES * h:hw + LANES * (h + 1)] for h in heads]
        s = [sret_scr[h] for h in heads]
        qh = [jnp.where((lane_c // dk) == (h % 2), qp[h // 2], 0.0) for h in heads]
        scores = [_dot1(qh[h], kp[h // 2], _NT) * dmat_ref[h] for h in heads]
        spad = [jnp.concatenate([s[h], zeros] if h % 2 == 0 else [zeros, s[h]], axis=0) for h in heads]
        cross = [_dot1(qp[h // 2], spad[h]) * rdec_ref[h] for h in heads]
        u = [_dot1(kp[h // 2] * cdec_ref[h], v[h], _TN) for h in heads]
        inner = [_dot1(scores[h], v[h]) for h in heads]
        for h in heads:
            e = h % 2
            out = inner[h] + cross[h]
            sret_scr[h] = s[h] * gc_ref[h] + u[h][dk * e:dk * (e + 1)]
            ms = jnp.mean(out * out, axis=-1, keepdims=True)
            o_ref[rs, LANES * h:LANES * (h + 1)] = (out * lax.rsqrt(ms + NORM_EPS)
                                                    * z_ref[rs, 2 * hw + LANES * h:2 * hw + LANES * (h + 1)])

    def chunk_body(ci, carry):
        r0 = pl.multiple_of(ci * c, c)
        if sample:
            sret_scr[...] = sret_in[ci]
            shg_scr[...] = shg_in[ci]
        hg_chunk(r0, _cumsum_rows(tri, z_ref[pl.ds(r0, c), 4 * hw:5 * hw]))
        ret_chunk(r0)
        if sample:
            sret_out[ci] = sret_scr[...]
            shg_out[ci] = shg_scr[...]
        return carry

    lax.fori_loop(0, n_chunks, chunk_body, 0)

    if not sample:
        @pl.when(pl.program_id(0) == pl.num_programs(0) - 1)
        def _():
            sret_out[...] = sret_scr[...]
            shg_out[...] = shg_scr[...]


def _ret_consts(c):
    heads = jnp.arange(RET_HEADS, dtype=F32)
    log_g = jnp.log1p(-jnp.exp2(-5.0 - heads))
    idx = jnp.arange(c, dtype=F32)
    rel = idx[:, None] - idx[None, :]
    dmat = jnp.exp(jnp.where(rel[None] >= 0, rel[None] * log_g[:, None, None], -jnp.inf))
    rdec = jnp.exp((idx[None, :] + 1.0) * log_g[:, None])
    cdec = jnp.exp((c - 1.0 - idx)[None, :] * log_g[:, None])
    gc = jnp.exp(c * log_g)
    bc = lambda a: jnp.broadcast_to(a[:, :, None], (RET_HEADS, c, LANES))
    return dmat, bc(rdec), bc(cdec), gc


def _ab_rec(z, hg_g, sret, shg_t, cfg, sample, d):
    n = z.shape[0]
    ng, npt = cfg["ng"], cfg["npt"]
    c = cfg["ts"] if sample else CHUNK
    dk, dv = d // 16, d // 8
    dmat, rdec, cdec, gc = _ret_consts(c)
    gcb = jnp.broadcast_to(gc[:, None, None], (RET_HEADS, dk, LANES))
    tri = np.kron(np.eye(c // SUB), np.tril(np.ones((SUB, SUB))))
    tri = jnp.asarray(tri, BF16)
    ones = jnp.ones((LANES, LANES), BF16)
    kern = functools.partial(_ab_rec_kernel, d=d, c=c, sample=sample)
    consts = [dmat, rdec, cdec, gcb, tri, hg_g.reshape(1, dv), ones]
    const_specs = [
        _const_spec(dmat.shape, (0, 0, 0)), _const_spec(rdec.shape, (0, 0, 0)),
        _const_spec(cdec.shape, (0, 0, 0)), _const_spec(gcb.shape, (0, 0, 0)),
        _const_spec(tri.shape, (0, 0)), _const_spec((1, dv), (0, 0)), _const_spec(ones.shape, (0, 0)),
    ]
    scratch = [pltpu.VMEM((RET_HEADS, dk, dv), F32), pltpu.VMEM((HG_HEADS, dv, dv), F32)]
    if sample:
        nb = sret.shape[0]
        nst = nb // ng
        return pl.pallas_call(
            kern,
            grid=(nst,),
            in_specs=[pl.BlockSpec((TM, z.shape[1]), lambda i: (npt + i, 0))] + const_specs + [
                pl.BlockSpec((ng, RET_HEADS, dk, dv), lambda i: (i, 0, 0, 0)),
                pl.BlockSpec((ng, HG_HEADS, dv, dv), lambda i: (i, 0, 0, 0)),
            ],
            out_specs=[
                pl.BlockSpec((TM, d), lambda i: (i, 0)),
                pl.BlockSpec((ng, RET_HEADS, dk, dv), lambda i: (i, 0, 0, 0)),
                pl.BlockSpec((ng, HG_HEADS, dv, dv), lambda i: (i, 0, 0, 0)),
            ],
            out_shape=[
                jax.ShapeDtypeStruct((nst * TM, d), F32),
                jax.ShapeDtypeStruct((nb, RET_HEADS, dk, dv), F32),
                jax.ShapeDtypeStruct((nb, HG_HEADS, dv, dv), F32),
            ],
            scratch_shapes=scratch,
            compiler_params=_params(("arbitrary",)),
            name="ab_rec_sample",
        )(z, *consts, sret, shg_t)
    return pl.pallas_call(
        kern,
        grid=(npt,),
        in_specs=[pl.BlockSpec((TM, z.shape[1]), lambda i: (i, 0))] + const_specs,
        out_specs=[
            pl.BlockSpec((TM, d), lambda i: (i, 0)),
            pl.BlockSpec((RET_HEADS, dk, dv), lambda i: (0, 0, 0)),
            pl.BlockSpec((HG_HEADS, dv, dv), lambda i: (0, 0, 0)),
        ],
        out_shape=[
            jax.ShapeDtypeStruct((npt * TM, d), F32),
            jax.ShapeDtypeStruct((RET_HEADS, dk, dv), F32),
            jax.ShapeDtypeStruct((HG_HEADS, dv, dv), F32),
        ],
        scratch_shapes=scratch,
        compiler_params=_params(("arbitrary",)),
        name="ab_rec_prompt",
    )(z, *consts)


def _post_kernel(*refs, d, npt, nx, rwkv, final):
    x = _rows_value(refs[:nx], npt)
    it = iter(refs[nx:])
    if rwkv:
        yp_ref, ys_ref, bonus_ref, gate_ref, lng_ref, lnb_ref, seg_ref = (next(it) for _ in range(7))
    else:
        op_ref, os_ref = next(it), next(it)
    mod_ref, wout_ref, g2_ref, w1_ref, w2_ref = (next(it) for _ in range(5))
    if final:
        fg_ref = next(it)
        outp_ref, outs_ref = next(it), next(it)
    else:
        out_ref = next(it)

    i = pl.program_id(0)
    if rwkv:
        s_i = jnp.where(i >= npt, 1, 0)
        is_sample = (lax.broadcasted_iota(jnp.int32, (x.shape[0], LANES), 0) * 0 + s_i) == 1
        y = jnp.concatenate([jnp.where(is_sample, ys_ref[p], yp_ref[p]) for p in range(yp_ref.shape[0])],
                            axis=-1)
        seg = seg_ref[...]
        inv_n = 1.0 / RW_N
        mean = _segsum(y, seg) * inv_n
        yc = y - mean
        var = _segsum(yc * yc, seg) * inv_n
        o = yc * lax.rsqrt(var + RW_LN_EPS) * lng_ref[0] + lnb_ref[0]
        o = (o + bonus_ref[...]) * gate_ref[...]
    else:
        o = _rows_value((op_ref, os_ref), npt)
    gt1 = mod_ref[0, :, :, 2 * d:3 * d]
    sh2 = mod_ref[0, :, :, 3 * d:4 * d]
    sc2 = mod_ref[0, :, :, 4 * d:5 * d]
    gt2 = mod_ref[0, :, :, 5 * d:6 * d]
    x1 = _gate_add(x, gt1, _dg(o.astype(BF16), wout_ref[0], _NN))
    hb = _modulate(_rms(x1, g2_ref[0]), sc2, sh2).astype(BF16)
    ff = w1_ref.shape[2]
    acc = jnp.zeros_like(x1)
    for j in range(ff // d):
        u = jnp.maximum(_dg(hb, w1_ref[0, :, j * d:(j + 1) * d], _NN), 0.0)
        acc = acc + _dg((u * u).astype(BF16), w2_ref[0, j * d:(j + 1) * d, :], _NN)
    x2 = _gate_add(x1, gt2, acc)
    if final:
        x2 = _rms(x2, fg_ref[...])

        @pl.when(i < npt)
        def _():
            outp_ref[...] = x2

        @pl.when(i >= npt)
        def _():
            outs_ref[...] = x2
    else:
        out_ref[...] = x2


def _post(x, mixer_in, modt, wout, g2, w1, w2, final_g, layer, m, cfg, rwkv, final):
    ng, npt, n = cfg["ng"], cfg["npt"], cfg["n"]
    args, specs = _rows_specs(x, npt)
    d = args[0].shape[1]
    nt = n // TM
    kern = functools.partial(_post_kernel, d=d, npt=npt, nx=len(args), rwkv=rwkv, final=final)
    tok = pl.BlockSpec((TM, d), lambda i: (i, 0))
    tok_p = pl.BlockSpec((TM, d), lambda i: (jnp.minimum(i, npt - 1), 0))
    tok_s = pl.BlockSpec((TM, d), lambda i: (jnp.maximum(i - npt, 0), 0))
    if rwkv:
        y_p, y_s, bonus, gate, lng, lnb, seg = mixer_in
        args += [y_p, y_s, bonus, gate, lng, lnb, seg]
        specs += [pl.BlockSpec((y_p.shape[0], TM, LANES), lambda i: (0, jnp.minimum(i, npt - 1), 0)),
                  pl.BlockSpec((y_s.shape[0], TM, LANES), lambda i: (0, jnp.maximum(i - npt, 0), 0)), tok, tok,
                  _const_spec((1, 1, d), (m, 0, 0)), _const_spec((1, 1, d), (m, 0, 0)),
                  _const_spec((d, d), (0, 0))]
    else:
        args += list(mixer_in)
        specs += [tok_p, tok_s]
    args += [modt, wout, g2, w1, w2]
    specs += [
        pl.BlockSpec((1, ng, 1, 6 * d), lambda i: (layer, jnp.maximum(i - npt + 1, 0), 0, 0)),
        _const_spec((1, d, d), (m, 0, 0)),
        _const_spec((1, 1, d), (layer, 0, 0)),
        _const_spec((1, d, w1.shape[2]), (layer, 0, 0)),
        _const_spec((1, w2.shape[1], d), (layer, 0, 0)),
    ]
    if final:
        args += [final_g]
        specs += [_const_spec((1, d), (0, 0))]
    if final:
        out_specs = [tok_p, tok_s]
        out_shape = [jax.ShapeDtypeStruct((npt * TM, d), F32), jax.ShapeDtypeStruct((n - npt * TM, d), F32)]
    else:
        out_specs = tok
        out_shape = jax.ShapeDtypeStruct((n, d), F32)
    return pl.pallas_call(
        kern,
        grid=(nt,),
        in_specs=specs,
        out_specs=out_specs,
        out_shape=out_shape,
        compiler_params=_params(("arbitrary",)),
        name="post_rwkv" if rwkv else "post_ab",
    )(*args)


def _rw_in_kernel(*refs, d, ts, npt, vres):
    it = iter(refs)
    x_ref, mod_ref, g_ref, shift_ref, mu_ref, wrkv_ref = (next(it) for _ in range(6))
    w0_ref, w1_ref, w2_ref, a0_ref, a1_ref, a2_ref, g1_ref, g2_ref = (next(it) for _ in range(8))
    kk_ref, ka_ref, rk_ref, seg_ref = (next(it) for _ in range(4))
    if vres:
        vf_ref, v0_ref, v1_ref, v2_ref = (next(it) for _ in range(4))
    r_out, ld_out, k_out, v_out, kk_out, kka_out, gate_out, bonus_out, hlast_out = (next(it) for _ in range(9))
    if not vres:
        vtok_out = next(it)
    h_scr, carry_scr = next(it), next(it)

    i = pl.program_id(0)
    rows = x_ref.shape[0]
    ng = rows // ts

    @pl.when(i == 0)
    def _():
        carry_scr[...] = jnp.zeros_like(carry_scr)

    x = x_ref[...]
    h = _modulate(_rms(x, g_ref[0]), mod_ref[0, :, :, d:2 * d], mod_ref[0, :, :, 0:d])
    for p in range(d // LANES):
        h_scr[p] = h[:, LANES * p:LANES * (p + 1)]
        hlast_out[:, LANES * p:LANES * (p + 1)] = h_scr[p, pl.ds(ts - 1, ng, stride=ts), :]
    rolled = pltpu.roll(h, 1, 0)
    row = lax.broadcasted_iota(jnp.int32, (rows, d), 0)
    s_i = jnp.where(i >= npt, 1, 0)
    first = ((row % ts) == 0) & ((row * (1 - s_i)) == 0)
    seq_prev = jnp.broadcast_to(shift_ref[...], (ng, ts, d)).reshape(rows, d)
    prev_first = jnp.where((row * 0 + s_i) == 1, seq_prev, jnp.broadcast_to(carry_scr[0:1, :], (rows, d)))
    prev = jnp.where(first, prev_first, rolled)
    carry_scr[0:1, :] = h[rows - 1:rows, :]

    xx = prev - h
    mix = lambda j: (h + xx * mu_ref[0, j:j + 1, :]).astype(BF16)
    xr, xw, xk, xv, xa, xg = (mix(j) for j in range(6))
    r = _dg(xr, wrkv_ref[0, 0], _NN)
    k = _dg(xk, wrkv_ref[0, 1], _NN)
    v = _dg(xv, wrkv_ref[0, 2], _NN)
    if vres:
        lv = v0_ref[0] + _dot1(_dg(xv, v1_ref[0], _NN), v2_ref[0])
        v = v + (vf_ref[...] - v) * _sigmoid(lv)
    else:
        vtok_out[...] = v
    wl = w0_ref[0] + _dot1(jnp.tanh(_dg(xw, w1_ref[0], _NN)), w2_ref[0])
    nwl = -wl
    w = -(jnp.maximum(nwl, 0.0) + jnp.log1p(jnp.exp(-jnp.abs(nwl)))) - 0.5
    ld = -jnp.exp(w)
    a = _sigmoid(a0_ref[0] + _dot1(_dg(xa, a1_ref[0], _NN), a2_ref[0]))
    gate = _dot1(_sigmoid(_dg(xg, g1_ref[0], _NN)), g2_ref[0])
    seg = seg_ref[...]
    kk = k * kk_ref[0]
    kk = kk / jnp.maximum(jnp.sqrt(_segsum(kk * kk, seg)), 1e-12)
    km = k * (1.0 + (a - 1.0) * ka_ref[0])
    bonus = _segsum(r * km * rk_ref[0], seg) * v
    gate_out[...] = gate
    bonus_out[...] = bonus
    kka = kk * a
    for p in range(d // LANES):
        cs = slice(LANES * p, LANES * (p + 1))
        r_out[p] = r[:, cs]
        ld_out[p] = ld[:, cs]
        k_out[p] = km[:, cs]
        v_out[p] = v[:, cs]
        kk_out[p] = kk[:, cs]
        kka_out[p] = kka[:, cs]


def _rw_in(x, modt, g, shift_rows, W, seg, v_first, layer, m, cfg):
    n, d = x.shape
    ng, npt, ts = cfg["ng"], cfg["npt"], cfg["ts"]
    nt = n // TM
    npair = d // LANES
    vres = v_first is not None
    kern = functools.partial(_rw_in_kernel, d=d, ts=ts, npt=npt, vres=vres)
    tok = pl.BlockSpec((TM, d), lambda i: (i, 0))
    vec = lambda: _const_spec((1, 1, d), (m, 0, 0))
    lora = lambda a: _const_spec((1,) + a.shape[1:], (m, 0, 0))
    args = [x, modt, g, shift_rows, W["rw_mu"], W["rw_w_rkv"],
            W["rw_w0"], W["rw_w1"], W["rw_w2"], W["rw_a0"], W["rw_a1"], W["rw_a2"], W["rw_g1"], W["rw_g2"],
            W["rw_k_k"], W["rw_k_a"], W["rw_r_k"], seg]
    specs = [
        tok,
        pl.BlockSpec((1, ng, 1, 6 * d), lambda i: (layer, jnp.maximum(i - npt + 1, 0), 0, 0)),
        _const_spec((1, 1, d), (layer, 0, 0)),
        pl.BlockSpec((ng, 1, d), lambda i: (jnp.maximum(i - npt, 0), 0, 0)),
        _const_spec((1, 6, d), (m, 0, 0)),
        _const_spec((1, 3, d, d), (m, 0, 0, 0)),
        vec(), lora(W["rw_w1"]), lora(W["rw_w2"]), vec(), lora(W["rw_a1"]), lora(W["rw_a2"]),
        lora(W["rw_g1"]), lora(W["rw_g2"]), vec(), vec(), vec(), _const_spec((d, d), (0, 0)),
    ]
    if vres:
        args += [v_first, W["rw_v0"], W["rw_v1"], W["rw_v2"]]
        specs += [tok, _const_spec((1, 1, d), (m - 1, 0, 0)),
                  _const_spec((1,) + W["rw_v1"].shape[1:], (m - 1, 0, 0)),
                  _const_spec((1,) + W["rw_v2"].shape[1:], (m - 1, 0, 0))]
    pm = pl.BlockSpec((npair, TM, LANES), lambda i: (0, i, 0))
    pm_shape = jax.ShapeDtypeStruct((npair, n, LANES), F32)
    tok_shape = jax.ShapeDtypeStruct((n, d), F32)
    out_specs = [pm] * 6 + [tok, tok, pl.BlockSpec((ng, d), lambda i: (i, 0))]
    out_shape = [pm_shape] * 6 + [tok_shape, tok_shape, jax.ShapeDtypeStruct((nt * ng, d), F32)]
    if not vres:
        out_specs.append(tok)
        out_shape.append(tok_shape)
    return pl.pallas_call(
        kern,
        grid=(nt,),
        in_specs=specs,
        out_specs=out_specs,
        out_shape=out_shape,
        scratch_shapes=[pltpu.VMEM((npair, TM, LANES), F32), pltpu.VMEM((8, d), F32)],
        compiler_params=_params(("arbitrary",)),
        name="rw_in",
    )(*args)


def _rw_rec_kernel(*refs, c, sample):
    if sample:
        (r_ref, ld_ref, k_ref, v_ref, kk_ref, kka_ref, tri_ref, ms_ref, mi_ref, eye_ref, s_in,
         y_ref, s_out, s_scr) = refs
    else:
        (r_ref, ld_ref, k_ref, v_ref, kk_ref, kka_ref, tri_ref, ms_ref, mi_ref, eye_ref,
         y_ref, s_out, s_scr) = refs
    npair, rows, _ = r_ref.shape
    n_chunks = rows // c
    n = 2 * c

    if not sample:
        @pl.when(pl.program_id(0) == 0)
        def _():
            s_scr[...] = jnp.zeros_like(s_scr)

    lane = lax.broadcasted_iota(jnp.int32, (c, LANES), 1)
    left = lane < RW_N
    tri = tri_ref[...]
    mask_strict = ms_ref[...]
    mask_incl = mi_ref[...]
    eye = eye_ref[...]
    steps = int(np.log2(c)) - 1

    def stack(xv):
        return jnp.concatenate([jnp.where(left, xv, 0.0), jnp.where(left, 0.0, xv)], axis=0)

    pairs = range(npair)

    def chunk_body(ci, carry):
        rs = pl.ds(pl.multiple_of(ci * c, c), c)
        s = [s_in[p, ci] if sample else s_scr[p] for p in pairs]
        ld_all = jnp.concatenate([ld_ref[p, rs, :] for p in pairs], axis=1)
        lc_all = _cumsum_rows(tri, ld_all)
        ur, bk, bk_end, vst, ltot = [], [], [], [], []
        for p in pairs:
            lc = lc_all[:, LANES * p:LANES * (p + 1)]
            ld = ld_all[:, LANES * p:LANES * (p + 1)]
            lt = lc[c - 1:c, :]
            g_inv = jnp.exp(-lc)
            g_end = jnp.exp(lt - lc)
            k = k_ref[p, rs, :]
            kka = kka_ref[p, rs, :]
            ur.append(jnp.concatenate([stack(-kk_ref[p, rs, :] * jnp.exp(lc - ld)),
                                       stack(r_ref[p, rs, :] * jnp.exp(lc))], axis=0))
            bk.append(jnp.concatenate([stack(kka * g_inv), stack(k * g_inv)], axis=0))
            bk_end.append(jnp.concatenate([stack(kka * g_end), stack(k * g_end)], axis=0))
            vst.append(stack(v_ref[p, rs, :]))
            ltot.append(lt)
        big = [_dotp(ur[p], bk[p], _NT, P_BIG) for p in pairs]
        m_ub = [big[p][0:n, 0:n] * mask_strict for p in pairs]
        m_uk = [big[p][0:n, n:2 * n] * mask_strict for p in pairs]
        a_r = [jnp.concatenate([big[p][n:2 * n, 0:n] * mask_incl, big[p][n:2 * n, n:2 * n] * mask_incl], axis=1)
               for p in pairs]
        t = [eye + m_ub[p] for p in pairs]
        mp = m_ub
        for _ in range(steps):
            mp = [_dot1(mp[p], mp[p]) for p in pairs]
            t = [t[p] + _dot1(t[p], mp[p]) for p in pairs]
        res = [eye - t[p] + _dot3(m_ub[p], t[p]) for p in pairs]
        t = [t[p] + _dot1(t[p], res[p]) for p in pairs]
        urs = [_dotp(ur[p], s[p], _NT, P_ST) for p in pairs]
        mv = [_dotp(m_uk[p], vst[p], _NN, P_ST) for p in pairs]
        e = [_dotp(t[p], urs[p][0:n] + mv[p], _NN, P_ST) for p in pairs]
        ev = [jnp.concatenate([e[p], vst[p]], axis=0) for p in pairs]
        yst = [urs[p][n:2 * n] + _dotp(a_r[p], ev[p], _NN, P_ST) for p in pairs]
        upd = [_dotp(ev[p], bk_end[p], _TN, P_ST) for p in pairs]
        for p in pairs:
            y_ref[p, rs, :] = yst[p][0:c] + yst[p][c:n]
            s_new = s[p] * jnp.exp(ltot[p]) + upd[p]
            if sample:
                s_out[p, ci] = s_new
            else:
                s_scr[p] = s_new
        return carry

    lax.fori_loop(0, n_chunks, chunk_body, 0)

    if not sample:
        @pl.when(pl.program_id(0) == pl.num_programs(0) - 1)
        def _():
            s_out[...] = s_scr[...]


def _rw_rec(seqs, s_bd, cfg, sample):
    npair, n, _ = seqs[0].shape
    ng, npt = cfg["ng"], cfg["npt"]
    c = cfg["ts"] if sample else CHUNK
    tri = jnp.asarray(np.tril(np.ones((c, c))), BF16)
    blk = np.kron(np.eye(2), np.ones((c, c)))
    big_tril = np.tril(np.ones((2 * c, 2 * c)))
    mask_incl = jnp.asarray(blk * big_tril, F32)
    mask_strict = jnp.asarray(blk * (big_tril - np.eye(2 * c)), F32)
    eye = jnp.asarray(np.eye(2 * c), F32)
    consts = [tri, mask_strict, mask_incl, eye]
    const_specs = [_const_spec(a.shape, (0, 0)) for a in consts]
    kern = functools.partial(_rw_rec_kernel, c=c, sample=sample)
    scratch = [pltpu.VMEM((npair, LANES, LANES), F32)]
    tokp = pl.BlockSpec((npair, TM, LANES), lambda i: (0, i, 0))
    if sample:
        nb = s_bd.shape[1]
        nst = nb // ng
        tok_in = pl.BlockSpec((npair, TM, LANES), lambda i: (0, npt + i, 0))
        sspec = pl.BlockSpec((npair, ng, LANES, LANES), lambda i: (0, i, 0, 0))
        return pl.pallas_call(
            kern,
            grid=(nst,),
            in_specs=[tok_in] * 6 + const_specs + [sspec],
            out_specs=[tokp, sspec],
            out_shape=[jax.ShapeDtypeStruct((npair, nst * TM, LANES), F32),
                       jax.ShapeDtypeStruct((npair, nb, LANES, LANES), F32)],
            scratch_shapes=scratch,
            compiler_params=_params(("arbitrary",)),
            name="rw_rec_sample",
        )(*seqs, *consts, s_bd)
    return pl.pallas_call(
        kern,
        grid=(npt,),
        in_specs=[tokp] * 6 + const_specs,
        out_specs=[tokp, pl.BlockSpec((npair, LANES, LANES), lambda i: (0, 0, 0))],
        out_shape=[jax.ShapeDtypeStruct((npair, npt * TM, LANES), F32),
                   jax.ShapeDtypeStruct((npair, LANES, LANES), F32)],
        scratch_shapes=scratch,
        compiler_params=_params(("arbitrary",)),
        name="rw_rec_prompt",
    )(*seqs, *consts)


def kernel(x_prompt, x_sample, state_ret, state_hgrn, state_wkv, state_shift, c_prompt, c_sample, mod_w, mod_b, norm_mix_g, norm_mlp_g, final_g, mlp_w1, mlp_w2, ab_w_in, ab_w_out, hg_lb, hg_norm_g, rw_mu, rw_w_rkv, rw_w0, rw_w1, rw_w2, rw_a0, rw_a1, rw_a2, rw_v0, rw_v1, rw_v2, rw_g1, rw_g2, rw_k_k, rw_k_a, rw_r_k, rw_ln_g, rw_ln_b, rw_w_out):
    bp, tp, d = x_prompt.shape
    bs, ts, _ = x_sample.shape
    depth = mod_w.shape[0]
    n_ab = ab_w_in.shape[0]
    n_c = rw_w_rkv.shape[0]
    assert bp == 1 and d == 1024 and tp % TM == 0 and tp % CHUNK == 0
    assert (bs * ts) % TM == 0 and TM % ts == 0 and ts % SUB == 0 and ts <= CHUNK
    np_rows, ns_rows = bp * tp, bs * ts
    n = np_rows + ns_rows
    ng = TM // ts
    assert bs % ng == 0
    cfg = dict(ng=ng, npt=np_rows // TM, ts=ts, n=n)
    dk, dv = d // 16, d // 8
    npair = d // LANES
    nh = d // RW_N

    c_all = jnp.concatenate([c_prompt, c_sample], axis=0)
    pad = (-c_all.shape[0]) % 8
    c_all = jnp.pad(c_all, ((0, pad), (0, 0)))
    mod = _modulation(c_all, mod_w, mod_b)
    modt = jnp.concatenate([jnp.broadcast_to(mod[:, 0:1], (depth, ng, 6 * d)), mod[:, bp:bp + bs]], axis=1)
    modt = modt.reshape(depth, ng + bs, 1, 6 * d)

    x = (x_prompt.reshape(np_rows, d), x_sample.reshape(ns_rows, d))

    pos = jnp.concatenate([jnp.arange(tp, dtype=F32), jnp.tile(PAST_LEN + jnp.arange(ts, dtype=F32), bs)])
    half = dk // 2
    inv = ROPE_BASE ** (-jnp.arange(half, dtype=F32) / half)
    ang = pos[:, None] * inv[None, :]
    cos_t = jnp.tile(jnp.cos(ang), (1, 4))
    sin_t = jnp.tile(jnp.concatenate([-jnp.sin(ang), jnp.sin(ang)], axis=1), (1, 2))

    lb_all = jnp.cumsum(jax.nn.softmax(hg_lb.astype(F32), axis=0), axis=0)
    lb_all = lb_all - lb_all[:1]
    lbc = jnp.stack([jnp.log(lb_all), jnp.log1p(-lb_all), 1.0 - lb_all], axis=1)
    lbc = jnp.pad(lbc, ((0, 0), (0, 5), (0, 0))).reshape(n_ab * 8, d // 2)

    bf = lambda a: a.astype(BF16)
    ab_w_in_b, ab_w_out_b = bf(ab_w_in), bf(ab_w_out)
    mlp_w1_b, mlp_w2_b = bf(mlp_w1), bf(mlp_w2)
    rw_w_out_b = bf(rw_w_out)
    vec = lambda a: a.reshape(a.shape[0], 1, d)
    W = dict(rw_mu=rw_mu, rw_w_rkv=bf(rw_w_rkv), rw_w0=vec(rw_w0), rw_w1=bf(rw_w1), rw_w2=bf(rw_w2),
             rw_a0=vec(rw_a0), rw_a1=bf(rw_a1), rw_a2=bf(rw_a2), rw_g1=bf(rw_g1), rw_g2=bf(rw_g2),
             rw_k_k=vec(rw_k_k), rw_k_a=vec(rw_k_a), rw_r_k=vec(rw_r_k), rw_v0=vec(rw_v0), rw_v1=bf(rw_v1),
             rw_v2=bf(rw_v2))
    seg = jnp.asarray(np.kron(np.eye(nh), np.ones((RW_N, RW_N))), BF16)
    g_mix = norm_mix_g.reshape(depth, 1, d)
    g_mlp = norm_mlp_g.reshape(depth, 1, d)
    fin_g = final_g.reshape(1, d)

    ret_p, ret_s, hg_p, hg_s, wkv_p, wkv_s, sh_p, sh_s = ([] for _ in range(8))
    v_first = None
    for layer in range(depth):
        m = layer // 2
        final = layer == depth - 1
        if layer % 2 == 0:
            z = _ab_in(x, modt, g_mix, ab_w_in_b, cos_t, sin_t, lbc, layer, m, cfg)
            o_p, r_p, h_p = _ab_rec(z, hg_norm_g[m], None, None, cfg, False, d)
            o_s, r_s, h_s = _ab_rec(z, hg_norm_g[m], state_ret[m], jnp.swapaxes(state_hgrn[m], -1, -2), cfg, True, d)
            ret_p.append(r_p[None])
            hg_p.append(jnp.swapaxes(h_p, -1, -2)[None])
            ret_s.append(r_s)
            hg_s.append(jnp.swapaxes(h_s, -1, -2))
            x = _post(x, (o_p, o_s), modt, ab_w_out_b, g_mlp, mlp_w1_b, mlp_w2_b, fin_g, layer, m, cfg, False, final)
        else:
            shift_rows = state_shift[m].reshape(bs, 1, d)
            outs = _rw_in(x, modt, g_mix, shift_rows, W, seg, v_first, layer, m, cfg)
            seqs, gate, bonus, hlast = outs[0:6], outs[6], outs[7], outs[8]
            if v_first is None:
                v_first = outs[9]
            y_p, s_p = _rw_rec(seqs, None, cfg, False)
            sw = state_wkv[m].reshape(bs, npair, 2, RW_N, RW_N)
            zero = jnp.zeros_like(sw[:, :, 0])
            s_bd = jnp.concatenate([jnp.concatenate([sw[:, :, 0], zero], axis=-1),
                                    jnp.concatenate([zero, sw[:, :, 1]], axis=-1)], axis=-2)
            y_s, s_s = _rw_rec(seqs, jnp.swapaxes(s_bd, 0, 1), cfg, True)
            unbd = lambda a: jnp.stack([a[..., :RW_N, :RW_N], a[..., RW_N:, RW_N:]], axis=-3)
            wkv_p.append(unbd(s_p).reshape(1, 1, nh, RW_N, RW_N))
            wkv_s.append(unbd(jnp.swapaxes(s_s, 0, 1)).reshape(bs, nh, RW_N, RW_N))
            sh_p.append(hlast[cfg["npt"] * ng - 1][None, None])
            sh_s.append(hlast[cfg["npt"] * ng:][None])
            x = _post(x, (y_p, y_s, bonus, gate, vec(rw_ln_g), vec(rw_ln_b), seg), modt, rw_w_out_b, g_mlp,
                      mlp_w1_b, mlp_w2_b, fin_g, layer, m, cfg, True, final)

    out_p, out_s = x
    return (out_p.reshape(bp, tp, d), out_s.reshape(bs, ts, d),
            jnp.stack(ret_p), jnp.stack(ret_s), jnp.stack(hg_p), jnp.stack(hg_s),
            jnp.concatenate(wkv_p, axis=0), jnp.stack(wkv_s),
            jnp.concatenate(sh_p, axis=0), jnp.concatenate(sh_s, axis=0))
```

```python
import functools

import numpy as np
import jax
import jax.numpy as jnp
from jax import lax
from jax.experimental import pallas as pl
from jax.experimental.pallas import tpu as pltpu

F32 = jnp.float32
BF16 = jnp.bfloat16

CHUNK = 64
PAST_LEN = 2048
ROPE_BASE = 10000.0
RET_HEADS = 4
HG_HEADS = 4
RW_N = 64
NORM_EPS = 1e-6
RW_LN_EPS = 64e-5

LANES = 128
SUB = 16
RW_GROUP = 2
ROW_BLOCK = 256
TM = 256
VMEM_LIMIT = 56 * 1024 * 1024


def _sigmoid(x):
    return 1.0 / (1.0 + jnp.exp(-x))


def _split2(x):
    hi = x.astype(BF16)
    lo = (x - hi.astype(F32)).astype(BF16)
    return hi, lo


def _split3(x):
    a = x.astype(BF16)
    r = x - a.astype(F32)
    b = r.astype(BF16)
    c = (r - b.astype(F32)).astype(BF16)
    return a, b, c


_NN = (((1,), (0,)), ((), ()))
_NT = (((1,), (1,)), ((), ()))
_TN = (((0,), (0,)), ((), ()))


def _dg(a, b, dims):
    return lax.dot_general(a, b, dims, preferred_element_type=F32)


def _dot1(a, b, dims=_NN):
    return _dg(a.astype(BF16), b.astype(BF16), dims)


def _dot3(a, b, dims=_NN):
    ah, al = _split2(a)
    bh, bl = _split2(b)
    return _dg(ah, bh, dims) + _dg(ah, bl, dims) + _dg(al, bh, dims)


def _dotp(a, b, dims, passes):
    return _dot1(a, b, dims) if passes == 1 else _dot3(a, b, dims)


P_BIG = 1
P_ST = 1


def _cumsum_rows(tri_bf16, x):
    a, b, c = _split3(x)
    return _dg(tri_bf16, a, _NN) + _dg(tri_bf16, b, _NN) + _dg(tri_bf16, c, _NN)


def _segsum(x, seg_bf16):
    return _dg(x.astype(BF16), seg_bf16, _NN)


def _rms(x, g):
    ms = jnp.mean(x * x, axis=-1, keepdims=True)
    return x * lax.rsqrt(ms + NORM_EPS) * g


def _modulate(y, sc, sh):
    rows, d = y.shape
    ng = sc.shape[0]
    y3 = y.reshape(ng, rows // ng, d)
    return (y3 * (1.0 + sc) + sh).reshape(rows, d)


def _gate_add(x, gt, out):
    rows, d = x.shape
    ng = gt.shape[0]
    return (x.reshape(ng, rows // ng, d) + gt * out.reshape(ng, rows // ng, d)).reshape(rows, d)


def _rows_specs(x, npt):
    if isinstance(x, tuple):
        d = x[0].shape[1]
        return list(x), [pl.BlockSpec((TM, d), lambda i: (jnp.minimum(i, npt - 1), 0)),
                         pl.BlockSpec((TM, d), lambda i: (jnp.maximum(i - npt, 0), 0))]
    return [x], [pl.BlockSpec((TM, x.shape[1]), lambda i: (i, 0))]


def _rows_value(refs, npt):
    if len(refs) == 1:
        return refs[0][...]
    p_ref, s_ref = refs
    rows, d = p_ref.shape
    s_i = jnp.where(pl.program_id(0) >= npt, 1, 0)
    is_sample = (lax.broadcasted_iota(jnp.int32, (rows, LANES), 0) * 0 + s_i) == 1
    return jnp.concatenate([jnp.where(is_sample, s_ref[:, LANES * p:LANES * (p + 1)],
                                      p_ref[:, LANES * p:LANES * (p + 1)]) for p in range(d // LANES)], axis=-1)


def _const_spec(block, index):
    return pl.BlockSpec(block, lambda *_: index, pipeline_mode=pl.Buffered(1))


def _params(sem):
    return pltpu.CompilerParams(dimension_semantics=sem, vmem_limit_bytes=VMEM_LIMIT)


def _mod_kernel(c_ref, w_ref, b_ref, o_ref):
    c = c_ref[...]
    s = c * _sigmoid(c)
    o_ref[0] = _dot1(s, w_ref[0]) + b_ref[0]


def _modulation(c_all, mod_w, mod_b):
    depth, d, d6 = mod_w.shape
    rows = c_all.shape[0]
    nt = d6 // d
    return pl.pallas_call(
        _mod_kernel,
        grid=(depth, nt),
        in_specs=[
            pl.BlockSpec((rows, d), lambda l, j: (0, 0)),
            pl.BlockSpec((1, d, d), lambda l, j: (l, 0, j)),
            pl.BlockSpec((1, 1, d), lambda l, j: (l, 0, j)),
        ],
        out_specs=pl.BlockSpec((1, rows, d), lambda l, j: (l, 0, j)),
        out_shape=jax.ShapeDtypeStruct((depth, rows, d6), F32),
        compiler_params=_params(("arbitrary", "arbitrary")),
        name="modulation",
    )(c_all, mod_w, mod_b.reshape(depth, 1, d6))


def _ab_in_kernel(*refs, d, npt, dk_scale):
    mod_ref, g_ref, w_ref, cos_ref, sin_ref, lb_ref, o_ref = refs[-7:]
    x = _rows_value(refs[:-7], npt)
    h = _modulate(_rms(x, g_ref[0]), mod_ref[0, :, :, d:2 * d], mod_ref[0, :, :, 0:d])
    hb = h.astype(BF16)
    hw = d // 2

    def proj(a):
        return _dg(hb, w_ref[0, :, a:a + hw], _NN)

    qk = proj(0)
    lane = lax.broadcasted_iota(jnp.int32, (x.shape[0], LANES), 1)
    first_half = (lane % 64) < 32
    cos = cos_ref[...]
    sin = sin_ref[...]
    for s in range(hw // LANES):
        xs = qk[:, LANES * s:LANES * (s + 1)]
        sw = jnp.where(first_half, pltpu.roll(xs, 96, 1), pltpu.roll(xs, 32, 1))
        rot = xs * cos + sw * sin
        if s < hw // (2 * LANES):
            rot = rot * dk_scale
        o_ref[:, LANES * s:LANES * (s + 1)] = rot
    o_ref[:, hw:2 * hw] = proj(hw)
    z = proj(2 * hw)
    o_ref[:, 2 * hw:3 * hw] = z * _sigmoid(z)
    z = proj(3 * hw)
    o_ref[:, 3 * hw:4 * hw] = z * _sigmoid(z)
    z = proj(4 * hw)
    log_lb = lb_ref[0:1, :]
    log1m_lb = lb_ref[1:2, :]
    ls = jnp.minimum(z, 0.0) - jnp.log1p(jnp.exp(-jnp.abs(z)))
    c = log1m_lb + ls
    m = jnp.maximum(log_lb, c)
    o_ref[:, 4 * hw:5 * hw] = m + jnp.log1p(jnp.exp(-jnp.abs(log_lb - c)))
    o_ref[:, 5 * hw:6 * hw] = c - z
    o_ref[:, 6 * hw:7 * hw] = proj(5 * hw)
    o_ref[:, 7 * hw:8 * hw] = _sigmoid(proj(6 * hw))


def _ab_in(x, modt, g, w_in, cos_t, sin_t, lbc, layer, m, cfg):
    ng, npt, n = cfg["ng"], cfg["npt"], cfg["n"]
    x_args, x_specs = _rows_specs(x, npt)
    d = x_args[0].shape[1]
    nt = n // TM
    zw = 4 * d
    kern = functools.partial(_ab_in_kernel, d=d, npt=npt, dk_scale=float((d // 16) ** -0.5))
    return pl.pallas_call(
        kern,
        grid=(nt,),
        in_specs=x_specs + [
            pl.BlockSpec((1, ng, 1, 6 * d), lambda i: (layer, jnp.maximum(i - npt + 1, 0), 0, 0)),
            _const_spec((1, 1, d), (layer, 0, 0)),
            _const_spec((1, d, w_in.shape[2]), (m, 0, 0)),
            pl.BlockSpec((TM, LANES), lambda i: (i, 0)),
            pl.BlockSpec((TM, LANES), lambda i: (i, 0)),
            _const_spec((8, d // 2), (m, 0)),
        ],
        out_specs=pl.BlockSpec((TM, zw), lambda i: (i, 0)),
        out_shape=jax.ShapeDtypeStruct((n, zw), F32),
        compiler_params=_params(("arbitrary",)),
        name="ab_in",
    )(*x_args, modt, g, w_in, cos_t, sin_t, lbc)


def _ab_rec_kernel(*refs, d, c, sample):
    if sample:
        (z_ref, dmat_ref, rdec_ref, cdec_ref, gc_ref, tri_ref, hgg_ref, ones_ref, sret_in, shg_in,
         o_ref, sret_out, shg_out, sret_scr, shg_scr) = refs
    else:
        (z_ref, dmat_ref, rdec_ref, cdec_ref, gc_ref, tri_ref, hgg_ref, ones_ref,
         o_ref, sret_out, shg_out, sret_scr, shg_scr) = refs
    hw = d // 2
    rows = z_ref.shape[0]
    n_chunks = rows // c
    dk = d // 16

    if not sample:
        @pl.when(pl.program_id(0) == 0)
        def _():
            sret_scr[...] = jnp.zeros_like(sret_scr)
            shg_scr[...] = jnp.zeros_like(shg_scr)

    lane_c = lax.broadcasted_iota(jnp.int32, (c, LANES), 1)
    row_h = lax.broadcasted_iota(jnp.int32, (SUB // 2, LANES), 0)
    ones = ones_ref[...]
    tri = tri_ref[...]

    def hg_chunk(r0, b):
        col = lambda g, h: slice(g * hw + LANES * h, g * hw + LANES * (h + 1))
        units = [(i, h) for i in range(c // SUB) for h in range(HG_HEADS)]
        rs = {i: pl.ds(r0 + SUB * i, SUB) for i in range(c // SUB)}
        q = {u: z_ref[rs[u[0]], col(3, u[1])] for u in units}
        v = {u: z_ref[rs[u[0]], col(6, u[1])] for u in units}
        bs = {(i, h): b[SUB * i:SUB * (i + 1), LANES * h:LANES * (h + 1)] for (i, h) in units}
        bl = {u: bs[u][SUB - 1:SUB, :] for u in units}
        w = {u: bs[u] - z_ref[rs[u[0]], col(5, u[1])] for u in units}
        upd = {u: _dot1(v[u], jnp.exp(bl[u] - w[u]), _TN) for u in units}
        hs = SUB // 2
        r = {}
        for u in units:
            q_lo, q_hi = q[u][0:hs], q[u][hs:SUB]
            b_lo, b_hi = bs[u][0:hs], bs[u][hs:SUB]
            p_lo, p_hi = [], []
            for j in range(SUB):
                wj = w[u][j:j + 1, :]
                if j < hs:
                    p_lo.append(jnp.where(row_h >= j, q_lo * jnp.exp(b_lo - wj), 0.0))
                    p_hi.append(q_hi * jnp.exp(b_hi - wj))
                else:
                    p_hi.append(jnp.where(row_h >= j - hs, q_hi * jnp.exp(b_hi - wj), 0.0))
            r[u] = _dot1(jnp.concatenate(p_lo + p_hi, axis=0), ones)
        inner = {}
        for u in units:
            acc_lo = r[u][0:hs] * v[u][0:1, :]
            for j in range(1, hs):
                acc_lo = acc_lo + r[u][hs * j:hs * (j + 1)] * v[u][j:j + 1, :]
            base = hs * hs
            acc_hi = r[u][base:base + hs] * v[u][0:1, :]
            for j in range(1, SUB):
                acc_hi = acc_hi + r[u][base + hs * j:base + hs * (j + 1)] * v[u][j:j + 1, :]
            inner[u] = jnp.concatenate([acc_lo, acc_hi], axis=0)
        st = {h: shg_scr[h] for h in range(HG_HEADS)}
        for u in units:
            i, h = u
            out = inner[u] + _dot1(q[u] * jnp.exp(bs[u]), st[h], _NT)
            st[h] = st[h] * jnp.exp(bl[u]) + upd[u]
            ms = jnp.mean(out * out, axis=-1, keepdims=True)
            o_ref[rs[i], col(1, h)] = out * lax.rsqrt(ms + NORM_EPS) * hgg_ref[...] * z_ref[rs[i], col(7, h)]
        for h in range(HG_HEADS):
            shg_scr[h] = st[h]

    def ret_chunk(r0):
        rs = pl.ds(r0, c)
        heads = range(RET_HEADS)
        zeros = jnp.zeros((dk, LANES), F32)
        qp = [z_ref[rs, LANES * p:LANES * (p + 1)] for p in range(RET_HEADS // 2)]
        kp = [z_ref[rs, hw // 2 + LANES * p:hw // 2 + LANES * (p + 1)] for p in range(RET_HEADS // 2)]
        v = [z_ref[rs, hw + LANES * h:hw + LANES * (h + 1)] for h in heads]
        s = [sret_scr[h] for h in heads]
        qh = [jnp.where((lane_c // dk) == (h % 2), qp[h // 2], 0.0) for h in heads]
        scores = [_dot1(qh[h], kp[h // 2], _NT) * dmat_ref[h] for h in heads]
        spad = [jnp.concatenate([s[h], zeros] if h % 2 == 0 else [zeros, s[h]], axis=0) for h in heads]
        cross = [_dot1(qp[h // 2], spad[h]) * rdec_ref[h] for h in heads]
        u = [_dot1(kp[h // 2] * cdec_ref[h], v[h], _TN) for h in heads]
        inner = [_dot1(scores[h], v[h]) for h in heads]
        for h in heads:
            e = h % 2
            out = inner[h] + cross[h]
            sret_scr[h] = s[h] * gc_ref[h] + u[h][dk * e:dk * (e + 1)]
            ms = jnp.mean(out * out, axis=-1, keepdims=True)
            o_ref[rs, LANES * h:LANES * (h + 1)] = (out * lax.rsqrt(ms + NORM_EPS)
                                                    * z_ref[rs, 2 * hw + LANES * h:2 * hw + LANES * (h + 1)])

    def chunk_body(ci, carry):
        r0 = pl.multiple_of(ci * c, c)
        if sample:
            sret_scr[...] = sret_in[ci]
            shg_scr[...] = shg_in[ci]
        hg_chunk(r0, _cumsum_rows(tri, z_ref[pl.ds(r0, c), 4 * hw:5 * hw]))
        ret_chunk(r0)
        if sample:
            sret_out[ci] = sret_scr[...]
            shg_out[ci] = shg_scr[...]
        return carry

    lax.fori_loop(0, n_chunks, chunk_body, 0, unroll=2)

    if not sample:
        @pl.when(pl.program_id(0) == pl.num_programs(0) - 1)
        def _():
            sret_out[...] = sret_scr[...]
            shg_out[...] = shg_scr[...]


def _ret_consts(c):
    heads = jnp.arange(RET_HEADS, dtype=F32)
    log_g = jnp.log1p(-jnp.exp2(-5.0 - heads))
    idx = jnp.arange(c, dtype=F32)
    rel = idx[:, None] - idx[None, :]
    dmat = jnp.exp(jnp.where(rel[None] >= 0, rel[None] * log_g[:, None, None], -jnp.inf))
    rdec = jnp.exp((idx[None, :] + 1.0) * log_g[:, None])
    cdec = jnp.exp((c - 1.0 - idx)[None, :] * log_g[:, None])
    gc = jnp.exp(c * log_g)
    bc = lambda a: jnp.broadcast_to(a[:, :, None], (RET_HEADS, c, LANES))
    return dmat, bc(rdec), bc(cdec), gc


def _ab_rec(z, hg_g, sret, shg_t, cfg, sample, d):
    n = z.shape[0]
    ng, npt = cfg["ng"], cfg["npt"]
    c = cfg["ts"] if sample else CHUNK
    dk, dv = d // 16, d // 8
    dmat, rdec, cdec, gc = _ret_consts(c)
    gcb = jnp.broadcast_to(gc[:, None, None], (RET_HEADS, dk, LANES))
    tri = np.kron(np.eye(c // SUB), np.tril(np.ones((SUB, SUB))))
    tri = jnp.asarray(tri, BF16)
    ones = jnp.ones((LANES, LANES), BF16)
    kern = functools.partial(_ab_rec_kernel, d=d, c=c, sample=sample)
    consts = [dmat, rdec, cdec, gcb, tri, hg_g.reshape(1, dv), ones]
    const_specs = [
        _const_spec(dmat.shape, (0, 0, 0)), _const_spec(rdec.shape, (0, 0, 0)),
        _const_spec(cdec.shape, (0, 0, 0)), _const_spec(gcb.shape, (0, 0, 0)),
        _const_spec(tri.shape, (0, 0)), _const_spec((1, dv), (0, 0)), _const_spec(ones.shape, (0, 0)),
    ]
    scratch = [pltpu.VMEM((RET_HEADS, dk, dv), F32), pltpu.VMEM((HG_HEADS, dv, dv), F32)]
    if sample:
        nb = sret.shape[0]
        nst = nb // ng
        return pl.pallas_call(
            kern,
            grid=(nst,),
            in_specs=[pl.BlockSpec((TM, z.shape[1]), lambda i: (npt + i, 0))] + const_specs + [
                pl.BlockSpec((ng, RET_HEADS, dk, dv), lambda i: (i, 0, 0, 0)),
                pl.BlockSpec((ng, HG_HEADS, dv, dv), lambda i: (i, 0, 0, 0)),
            ],
            out_specs=[
                pl.BlockSpec((TM, d), lambda i: (i, 0)),
                pl.BlockSpec((ng, RET_HEADS, dk, dv), lambda i: (i, 0, 0, 0)),
                pl.BlockSpec((ng, HG_HEADS, dv, dv), lambda i: (i, 0, 0, 0)),
            ],
            out_shape=[
                jax.ShapeDtypeStruct((nst * TM, d), F32),
                jax.ShapeDtypeStruct((nb, RET_HEADS, dk, dv), F32),
                jax.ShapeDtypeStruct((nb, HG_HEADS, dv, dv), F32),
            ],
            scratch_shapes=scratch,
            compiler_params=_params(("arbitrary",)),
            name="ab_rec_sample",
        )(z, *consts, sret, shg_t)
    return pl.pallas_call(
        kern,
        grid=(npt,),
        in_specs=[pl.BlockSpec((TM, z.shape[1]), lambda i: (i, 0))] + const_specs,
        out_specs=[
            pl.BlockSpec((TM, d), lambda i: (i, 0)),
            pl.BlockSpec((RET_HEADS, dk, dv), lambda i: (0, 0, 0)),
            pl.BlockSpec((HG_HEADS, dv, dv), lambda i: (0, 0, 0)),
        ],
        out_shape=[
            jax.ShapeDtypeStruct((npt * TM, d), F32),
            jax.ShapeDtypeStruct((RET_HEADS, dk, dv), F32),
            jax.ShapeDtypeStruct((HG_HEADS, dv, dv), F32),
        ],
        scratch_shapes=scratch,
        compiler_params=_params(("arbitrary",)),
        name="ab_rec_prompt",
    )(z, *consts)


def _post_kernel(*refs, d, npt, nx, rwkv, final):
    x = _rows_value(refs[:nx], npt)
    it = iter(refs[nx:])
    if rwkv:
        yp_ref, ys_ref, bonus_ref, gate_ref, lng_ref, lnb_ref, seg_ref = (next(it) for _ in range(7))
    else:
        op_ref, os_ref = next(it), next(it)
    mod_ref, wout_ref, g2_ref, w1_ref, w2_ref = (next(it) for _ in range(5))
    if final:
        fg_ref = next(it)
        outp_ref, outs_ref = next(it), next(it)
    else:
        out_ref = next(it)

    i = pl.program_id(0)
    if rwkv:
        s_i = jnp.where(i >= npt, 1, 0)
        is_sample = (lax.broadcasted_iota(jnp.int32, (x.shape[0], LANES), 0) * 0 + s_i) == 1
        y = jnp.concatenate([jnp.where(is_sample, ys_ref[p], yp_ref[p]) for p in range(yp_ref.shape[0])],
                            axis=-1)
        seg = seg_ref[...]
        inv_n = 1.0 / RW_N
        mean = _segsum(y, seg) * inv_n
        yc = y - mean
        var = _segsum(yc * yc, seg) * inv_n
        o = yc * lax.rsqrt(var + RW_LN_EPS) * lng_ref[0] + lnb_ref[0]
        o = (o + bonus_ref[...]) * gate_ref[...]
    else:
        o = _rows_value((op_ref, os_ref), npt)
    gt1 = mod_ref[0, :, :, 2 * d:3 * d]
    sh2 = mod_ref[0, :, :, 3 * d:4 * d]
    sc2 = mod_ref[0, :, :, 4 * d:5 * d]
    gt2 = mod_ref[0, :, :, 5 * d:6 * d]
    x1 = _gate_add(x, gt1, _dg(o.astype(BF16), wout_ref[0], _NN))
    hb = _modulate(_rms(x1, g2_ref[0]), sc2, sh2).astype(BF16)
    ff = w1_ref.shape[2]
    acc = jnp.zeros_like(x1)
    for j in range(ff // d):
        u = jnp.maximum(_dg(hb, w1_ref[0, :, j * d:(j + 1) * d], _NN), 0.0)
        acc = acc + _dg((u * u).astype(BF16), w2_ref[0, j * d:(j + 1) * d, :], _NN)
    x2 = _gate_add(x1, gt2, acc)
    if final:
        x2 = _rms(x2, fg_ref[...])

        @pl.when(i < npt)
        def _():
            outp_ref[...] = x2

        @pl.when(i >= npt)
        def _():
            outs_ref[...] = x2
    else:
        out_ref[...] = x2


def _post(x, mixer_in, modt, wout, g2, w1, w2, final_g, layer, m, cfg, rwkv, final):
    ng, npt, n = cfg["ng"], cfg["npt"], cfg["n"]
    args, specs = _rows_specs(x, npt)
    d = args[0].shape[1]
    nt = n // TM
    kern = functools.partial(_post_kernel, d=d, npt=npt, nx=len(args), rwkv=rwkv, final=final)
    tok = pl.BlockSpec((TM, d), lambda i: (i, 0))
    tok_p = pl.BlockSpec((TM, d), lambda i: (jnp.minimum(i, npt - 1), 0))
    tok_s = pl.BlockSpec((TM, d), lambda i: (jnp.maximum(i - npt, 0), 0))
    if rwkv:
        y_p, y_s, bonus, gate, lng, lnb, seg = mixer_in
        args += [y_p, y_s, bonus, gate, lng, lnb, seg]
        specs += [pl.BlockSpec((y_p.shape[0], TM, LANES), lambda i: (0, jnp.minimum(i, npt - 1), 0)),
                  pl.BlockSpec((y_s.shape[0], TM, LANES), lambda i: (0, jnp.maximum(i - npt, 0), 0)), tok, tok,
                  _const_spec((1, 1, d), (m, 0, 0)), _const_spec((1, 1, d), (m, 0, 0)),
                  _const_spec((d, d), (0, 0))]
    else:
        args += list(mixer_in)
        specs += [tok_p, tok_s]
    args += [modt, wout, g2, w1, w2]
    specs += [
        pl.BlockSpec((1, ng, 1, 6 * d), lambda i: (layer, jnp.maximum(i - npt + 1, 0), 0, 0)),
        _const_spec((1, d, d), (m, 0, 0)),
        _const_spec((1, 1, d), (layer, 0, 0)),
        _const_spec((1, d, w1.shape[2]), (layer, 0, 0)),
        _const_spec((1, w2.shape[1], d), (layer, 0, 0)),
    ]
    if final:
        args += [final_g]
        specs += [_const_spec((1, d), (0, 0))]
    if final:
        out_specs = [tok_p, tok_s]
        out_shape = [jax.ShapeDtypeStruct((npt * TM, d), F32), jax.ShapeDtypeStruct((n - npt * TM, d), F32)]
    else:
        out_specs = tok
        out_shape = jax.ShapeDtypeStruct((n, d), F32)
    return pl.pallas_call(
        kern,
        grid=(nt,),
        in_specs=specs,
        out_specs=out_specs,
        out_shape=out_shape,
        compiler_params=_params(("arbitrary",)),
        name="post_rwkv" if rwkv else "post_ab",
    )(*args)


def _rw_in_kernel(*refs, d, ts, npt, vres):
    it = iter(refs)
    x_ref, mod_ref, g_ref, shift_ref, mu_ref, wrkv_ref = (next(it) for _ in range(6))
    w0_ref, w1_ref, w2_ref, a0_ref, a1_ref, a2_ref, g1_ref, g2_ref = (next(it) for _ in range(8))
    kk_ref, ka_ref, rk_ref, seg_ref = (next(it) for _ in range(4))
    if vres:
        vf_ref, v0_ref, v1_ref, v2_ref = (next(it) for _ in range(4))
    r_out, ld_out, k_out, v_out, kk_out, kka_out, gate_out, bonus_out, hlast_out = (next(it) for _ in range(9))
    if not vres:
        vtok_out = next(it)
    h_scr, carry_scr = next(it), next(it)

    i = pl.program_id(0)
    rows = x_ref.shape[0]
    ng = rows // ts

    @pl.when(i == 0)
    def _():
        carry_scr[...] = jnp.zeros_like(carry_scr)

    x = x_ref[...]
    h = _modulate(_rms(x, g_ref[0]), mod_ref[0, :, :, d:2 * d], mod_ref[0, :, :, 0:d])
    for p in range(d // LANES):
        h_scr[p] = h[:, LANES * p:LANES * (p + 1)]
        hlast_out[:, LANES * p:LANES * (p + 1)] = h_scr[p, pl.ds(ts - 1, ng, stride=ts), :]
    rolled = pltpu.roll(h, 1, 0)
    row = lax.broadcasted_iota(jnp.int32, (rows, d), 0)
    s_i = jnp.where(i >= npt, 1, 0)
    first = ((row % ts) == 0) & ((row * (1 - s_i)) == 0)
    seq_prev = jnp.broadcast_to(shift_ref[...], (ng, ts, d)).reshape(rows, d)
    prev_first = jnp.where((row * 0 + s_i) == 1, seq_prev, jnp.broadcast_to(carry_scr[0:1, :], (rows, d)))
    prev = jnp.where(first, prev_first, rolled)
    carry_scr[0:1, :] = h[rows - 1:rows, :]

    seg = seg_ref[...]
    for b0 in range(0, rows, ROW_BLOCK):
        rb = slice(b0, b0 + ROW_BLOCK)
        hb = h[rb]
        xx = prev[rb] - hb
        mix = lambda j: (hb + xx * mu_ref[0, j:j + 1, :]).astype(BF16)
        xr, xw, xk, xv, xa, xg = (mix(j) for j in range(6))
        r = _dg(xr, wrkv_ref[0, 0], _NN)
        k = _dg(xk, wrkv_ref[0, 1], _NN)
        v = _dg(xv, wrkv_ref[0, 2], _NN)
        if vres:
            lv = v0_ref[0] + _dot1(_dg(xv, v1_ref[0], _NN), v2_ref[0])
            v = v + (vf_ref[rb, :] - v) * _sigmoid(lv)
        else:
            vtok_out[rb, :] = v
        wl = w0_ref[0] + _dot1(jnp.tanh(_dg(xw, w1_ref[0], _NN)), w2_ref[0])
        nwl = -wl
        w = -(jnp.maximum(nwl, 0.0) + jnp.log1p(jnp.exp(-jnp.abs(nwl)))) - 0.5
        ld = -jnp.exp(w)
        a = _sigmoid(a0_ref[0] + _dot1(_dg(xa, a1_ref[0], _NN), a2_ref[0]))
        gate_out[rb, :] = _dot1(_sigmoid(_dg(xg, g1_ref[0], _NN)), g2_ref[0])
        kk = k * kk_ref[0]
        kk = kk / jnp.maximum(jnp.sqrt(_segsum(kk * kk, seg)), 1e-12)
        km = k * (1.0 + (a - 1.0) * ka_ref[0])
        bonus_out[rb, :] = _segsum(r * km * rk_ref[0], seg) * v
        kka = kk * a
        for p in range(d // LANES):
            cs = slice(LANES * p, LANES * (p + 1))
            r_out[p, rb, :] = r[:, cs]
            ld_out[p, rb, :] = ld[:, cs]
            k_out[p, rb, :] = km[:, cs]
            v_out[p, rb, :] = v[:, cs]
            kk_out[p, rb, :] = kk[:, cs]
            kka_out[p, rb, :] = kka[:, cs]


def _rw_in(x, modt, g, shift_rows, W, seg, v_first, layer, m, cfg):
    n, d = x.shape
    ng, npt, ts = cfg["ng"], cfg["npt"], cfg["ts"]
    nt = n // TM
    npair = d // LANES
    vres = v_first is not None
    kern = functools.partial(_rw_in_kernel, d=d, ts=ts, npt=npt, vres=vres)
    tok = pl.BlockSpec((TM, d), lambda i: (i, 0))
    vec = lambda: _const_spec((1, 1, d), (m, 0, 0))
    lora = lambda a: _const_spec((1,) + a.shape[1:], (m, 0, 0))
    args = [x, modt, g, shift_rows, W["rw_mu"], W["rw_w_rkv"],
            W["rw_w0"], W["rw_w1"], W["rw_w2"], W["rw_a0"], W["rw_a1"], W["rw_a2"], W["rw_g1"], W["rw_g2"],
            W["rw_k_k"], W["rw_k_a"], W["rw_r_k"], seg]
    specs = [
        tok,
        pl.BlockSpec((1, ng, 1, 6 * d), lambda i: (layer, jnp.maximum(i - npt + 1, 0), 0, 0)),
        _const_spec((1, 1, d), (layer, 0, 0)),
        pl.BlockSpec((ng, 1, d), lambda i: (jnp.maximum(i - npt, 0), 0, 0)),
        _const_spec((1, 6, d), (m, 0, 0)),
        _const_spec((1, 3, d, d), (m, 0, 0, 0)),
        vec(), lora(W["rw_w1"]), lora(W["rw_w2"]), vec(), lora(W["rw_a1"]), lora(W["rw_a2"]),
        lora(W["rw_g1"]), lora(W["rw_g2"]), vec(), vec(), vec(), _const_spec((d, d), (0, 0)),
    ]
    if vres:
        args += [v_first, W["rw_v0"], W["rw_v1"], W["rw_v2"]]
        specs += [tok, _const_spec((1, 1, d), (m - 1, 0, 0)),
                  _const_spec((1,) + W["rw_v1"].shape[1:], (m - 1, 0, 0)),
                  _const_spec((1,) + W["rw_v2"].shape[1:], (m - 1, 0, 0))]
    pm = pl.BlockSpec((npair, TM, LANES), lambda i: (0, i, 0))
    pm_shape = jax.ShapeDtypeStruct((npair, n, LANES), F32)
    tok_shape = jax.ShapeDtypeStruct((n, d), F32)
    out_specs = [pm] * 6 + [tok, tok, pl.BlockSpec((ng, d), lambda i: (i, 0))]
    out_shape = [pm_shape] * 6 + [tok_shape, tok_shape, jax.ShapeDtypeStruct((nt * ng, d), F32)]
    if not vres:
        out_specs.append(tok)
        out_shape.append(tok_shape)
    return pl.pallas_call(
        kern,
        grid=(nt,),
        in_specs=specs,
        out_specs=out_specs,
        out_shape=out_shape,
        scratch_shapes=[pltpu.VMEM((npair, TM, LANES), F32), pltpu.VMEM((8, d), F32)],
        compiler_params=_params(("arbitrary",)),
        name="rw_in",
    )(*args)


def _rw_rec_kernel(*refs, c, sample):
    if sample:
        (r_ref, ld_ref, k_ref, v_ref, kk_ref, kka_ref, tri_ref, ms_ref, mi_ref, eye_ref, s_in,
         y_ref, s_out, s_scr) = refs
    else:
        (r_ref, ld_ref, k_ref, v_ref, kk_ref, kka_ref, tri_ref, ms_ref, mi_ref, eye_ref,
         y_ref, s_out, s_scr) = refs
    npair, rows, _ = r_ref.shape
    n_chunks = rows // c
    n = 2 * c

    if not sample:
        @pl.when(pl.program_id(0) == 0)
        def _():
            s_scr[...] = jnp.zeros_like(s_scr)

    left_k = lax.broadcasted_iota(jnp.int32, (c, LANES), 1) < RW_N
    left_t = lax.broadcasted_iota(jnp.int32, (c, n), 1) < c
    tri = tri_ref[...]
    mask_strict = ms_ref[...]
    mask_incl = mi_ref[...]
    eye = eye_ref[...]
    steps = int(np.log2(c)) - 1

    def stack(xv, left):
        return jnp.concatenate([jnp.where(left, xv, 0.0), jnp.where(left, 0.0, xv)], axis=0)

    stack_k = lambda xv: stack(xv, left_k)
    stack_t = lambda xv: stack(xv, left_t)
    pairs = range(npair)

    nu = min(RW_GROUP, n_chunks)
    units = [(j, p) for j in range(nu) for p in pairs]

    def group_body(gi, carry):
        ci = {j: gi * nu + j for j in range(nu)}
        rs = {j: pl.ds(pl.multiple_of(ci[j] * c, c), c) for j in range(nu)}
        ur, bk, bk_end, vst, ltot = {}, {}, {}, {}, {}
        for j in range(nu):
            ld_all = jnp.concatenate([ld_ref[p, rs[j], :] for p in pairs], axis=1)
            lc_all = _cumsum_rows(tri, ld_all)
            for p in pairs:
                lc = lc_all[:, LANES * p:LANES * (p + 1)]
                ld = ld_all[:, LANES * p:LANES * (p + 1)]
                lt = lc[c - 1:c, :]
                g_inv = jnp.exp(-lc)
                g_end = jnp.exp(lt - lc)
                k = k_ref[p, rs[j], :]
                kka = kka_ref[p, rs[j], :]
                ur[j, p] = jnp.concatenate([-kk_ref[p, rs[j], :] * jnp.exp(lc - ld),
                                            r_ref[p, rs[j], :] * jnp.exp(lc)], axis=0)
                bk[j, p] = jnp.concatenate([stack_k(kka * g_inv), stack_k(k * g_inv)], axis=0)
                bk_end[j, p] = jnp.concatenate([stack_k(kka * g_end), stack_k(k * g_end)], axis=0)
                vst[j, p] = stack_k(v_ref[p, rs[j], :])
                ltot[j, p] = lt
        big = {u: _dotp(ur[u], bk[u], _NT, P_BIG) for u in units}
        m_ub = {u: big[u][0:c, 0:n] * mask_strict for u in units}
        m_uk = {u: big[u][0:c, n:2 * n] * mask_strict for u in units}
        a_r = {u: jnp.concatenate([big[u][c:n, 0:n] * mask_incl, big[u][c:n, n:2 * n] * mask_incl], axis=1)
               for u in units}
        t = {u: eye + m_ub[u] for u in units}
        mp = {u: _dot1(m_ub[u], stack_t(m_ub[u])) for u in units}
        for _ in range(steps - 1):
            both = {u: _dot1(jnp.concatenate([t[u], mp[u]], axis=0), stack_t(mp[u])) for u in units}
            t = {u: t[u] + both[u][0:c] for u in units}
            mp = {u: both[u][c:n] for u in units}
        t = {u: t[u] + _dot1(t[u], stack_t(mp[u])) for u in units}
        res = {}
        for u in units:
            mh, ml = _split2(m_ub[u])
            th, tl = _split2(stack_t(t[u]))
            hl = _dg(jnp.concatenate([mh, ml], axis=0), th, _NN)
            res[u] = eye - t[u] + (hl[0:c] + hl[c:n] + _dg(mh, tl, _NN))
        t = {u: t[u] + _dot1(t[u], stack_t(res[u])) for u in units}
        mv = {u: _dotp(m_uk[u], vst[u], _NN, P_ST) for u in units}
        s = None if sample else [s_scr[p] for p in pairs]
        for j in range(nu):
            sj = [s_in[p, ci[j]] for p in pairs] if sample else s
            urs = [_dotp(ur[j, p], sj[p], _NT, P_ST) for p in pairs]
            e = [_dotp(t[j, p], stack_k(urs[p][0:c] + mv[j, p]), _NN, P_ST) for p in pairs]
            ev = [jnp.concatenate([stack_k(e[p]), vst[j, p]], axis=0) for p in pairs]
            upd = [_dotp(ev[p], bk_end[j, p], _TN, P_ST) for p in pairs]
            y = [urs[p][c:n] + _dotp(a_r[j, p], ev[p], _NN, P_ST) for p in pairs]
            s_new = [sj[p] * jnp.exp(ltot[j, p]) + upd[p] for p in pairs]
            for p in pairs:
                y_ref[p, rs[j], :] = y[p]
                if sample:
                    s_out[p, ci[j]] = s_new[p]
            s = s_new
        if not sample:
            for p in pairs:
                s_scr[p] = s[p]
        return carry

    lax.fori_loop(0, n_chunks // nu, group_body, 0)

    if not sample:
        @pl.when(pl.program_id(0) == pl.num_programs(0) - 1)
        def _():
            s_out[...] = s_scr[...]


def _rw_rec(seqs, s_bd, cfg, sample):
    npair, n, _ = seqs[0].shape
    ng, npt = cfg["ng"], cfg["npt"]
    c = cfg["ts"] if sample else CHUNK
    tri = jnp.asarray(np.tril(np.ones((c, c))), BF16)
    side = lambda a: jnp.asarray(np.concatenate([a, a], axis=1), F32)
    mask_incl = side(np.tril(np.ones((c, c))))
    mask_strict = side(np.tril(np.ones((c, c)), -1))
    eye = side(np.eye(c))
    consts = [tri, mask_strict, mask_incl, eye]
    const_specs = [_const_spec(a.shape, (0, 0)) for a in consts]
    kern = functools.partial(_rw_rec_kernel, c=c, sample=sample)
    scratch = [pltpu.VMEM((npair, LANES, LANES), F32)]
    tokp = pl.BlockSpec((npair, TM, LANES), lambda i: (0, i, 0))
    if sample:
        nb = s_bd.shape[1]
        nst = nb // ng
        tok_in = pl.BlockSpec((npair, TM, LANES), lambda i: (0, npt + i, 0))
        sspec = pl.BlockSpec((npair, ng, LANES, LANES), lambda i: (0, i, 0, 0))
        return pl.pallas_call(
            kern,
            grid=(nst,),
            in_specs=[tok_in] * 6 + const_specs + [sspec],
            out_specs=[tokp, sspec],
            out_shape=[jax.ShapeDtypeStruct((npair, nst * TM, LANES), F32),
                       jax.ShapeDtypeStruct((npair, nb, LANES, LANES), F32)],
            scratch_shapes=scratch,
            compiler_params=_params(("arbitrary",)),
            name="rw_rec_sample",
        )(*seqs, *consts, s_bd)
    return pl.pallas_call(
        kern,
        grid=(npt,),
        in_specs=[tokp] * 6 + const_specs,
        out_specs=[tokp, pl.BlockSpec((npair, LANES, LANES), lambda i: (0, 0, 0))],
        out_shape=[jax.ShapeDtypeStruct((npair, npt * TM, LANES), F32),
                   jax.ShapeDtypeStruct((npair, LANES, LANES), F32)],
        scratch_shapes=scratch,
        compiler_params=_params(("arbitrary",)),
        name="rw_rec_prompt",
    )(*seqs, *consts)


def kernel(x_prompt, x_sample, state_ret, state_hgrn, state_wkv, state_shift, c_prompt, c_sample, mod_w, mod_b, norm_mix_g, norm_mlp_g, final_g, mlp_w1, mlp_w2, ab_w_in, ab_w_out, hg_lb, hg_norm_g, rw_mu, rw_w_rkv, rw_w0, rw_w1, rw_w2, rw_a0, rw_a1, rw_a2, rw_v0, rw_v1, rw_v2, rw_g1, rw_g2, rw_k_k, rw_k_a, rw_r_k, rw_ln_g, rw_ln_b, rw_w_out):
    bp, tp, d = x_prompt.shape
    bs, ts, _ = x_sample.shape
    depth = mod_w.shape[0]
    n_ab = ab_w_in.shape[0]
    n_c = rw_w_rkv.shape[0]
    assert bp == 1 and d == 1024 and tp % TM == 0 and tp % CHUNK == 0
    assert (bs * ts) % TM == 0 and TM % ts == 0 and ts % SUB == 0 and ts <= CHUNK
    np_rows, ns_rows = bp * tp, bs * ts
    n = np_rows + ns_rows
    ng = TM // ts
    assert bs % ng == 0
    cfg = dict(ng=ng, npt=np_rows // TM, ts=ts, n=n)
    dk, dv = d // 16, d // 8
    npair = d // LANES
    nh = d // RW_N

    c_all = jnp.concatenate([c_prompt, c_sample], axis=0)
    pad = (-c_all.shape[0]) % 8
    c_all = jnp.pad(c_all, ((0, pad), (0, 0)))
    mod = _modulation(c_all, mod_w, mod_b)
    modt = jnp.concatenate([jnp.broadcast_to(mod[:, 0:1], (depth, ng, 6 * d)), mod[:, bp:bp + bs]], axis=1)
    modt = modt.reshape(depth, ng + bs, 1, 6 * d)

    x = (x_prompt.reshape(np_rows, d), x_sample.reshape(ns_rows, d))

    pos = jnp.concatenate([jnp.arange(tp, dtype=F32), jnp.tile(PAST_LEN + jnp.arange(ts, dtype=F32), bs)])
    half = dk // 2
    inv = ROPE_BASE ** (-jnp.arange(half, dtype=F32) / half)
    ang = pos[:, None] * inv[None, :]
    cos_t = jnp.tile(jnp.cos(ang), (1, 4))
    sin_t = jnp.tile(jnp.concatenate([-jnp.sin(ang), jnp.sin(ang)], axis=1), (1, 2))

    lb_all = jnp.cumsum(jax.nn.softmax(hg_lb.astype(F32), axis=0), axis=0)
    lb_all = lb_all - lb_all[:1]
    lbc = jnp.stack([jnp.log(lb_all), jnp.log1p(-lb_all), 1.0 - lb_all], axis=1)
    lbc = jnp.pad(lbc, ((0, 0), (0, 5), (0, 0))).reshape(n_ab * 8, d // 2)

    bf = lambda a: a.astype(BF16)
    ab_w_in_b, ab_w_out_b = bf(ab_w_in), bf(ab_w_out)
    mlp_w1_b, mlp_w2_b = bf(mlp_w1), bf(mlp_w2)
    rw_w_out_b = bf(rw_w_out)
    vec = lambda a: a.reshape(a.shape[0], 1, d)
    W = dict(rw_mu=rw_mu, rw_w_rkv=bf(rw_w_rkv), rw_w0=vec(rw_w0), rw_w1=bf(rw_w1), rw_w2=bf(rw_w2),
             rw_a0=vec(rw_a0), rw_a1=bf(rw_a1), rw_a2=bf(rw_a2), rw_g1=bf(rw_g1), rw_g2=bf(rw_g2),
             rw_k_k=vec(rw_k_k), rw_k_a=vec(rw_k_a), rw_r_k=vec(rw_r_k), rw_v0=vec(rw_v0), rw_v1=bf(rw_v1),
             rw_v2=bf(rw_v2))
    seg = jnp.asarray(np.kron(np.eye(nh), np.ones((RW_N, RW_N))), BF16)
    g_mix = norm_mix_g.reshape(depth, 1, d)
    g_mlp = norm_mlp_g.reshape(depth, 1, d)
    fin_g = final_g.reshape(1, d)

    ret_p, ret_s, hg_p, hg_s, wkv_p, wkv_s, sh_p, sh_s = ([] for _ in range(8))
    v_first = None
    for layer in range(depth):
        m = layer // 2
        final = layer == depth - 1
        if layer % 2 == 0:
            z = _ab_in(x, modt, g_mix, ab_w_in_b, cos_t, sin_t, lbc, layer, m, cfg)
            o_p, r_p, h_p = _ab_rec(z, hg_norm_g[m], None, None, cfg, False, d)
            o_s, r_s, h_s = _ab_rec(z, hg_norm_g[m], state_ret[m], jnp.swapaxes(state_hgrn[m], -1, -2), cfg, True, d)
            ret_p.append(r_p[None])
            hg_p.append(jnp.swapaxes(h_p, -1, -2)[None])
            ret_s.append(r_s)
            hg_s.append(jnp.swapaxes(h_s, -1, -2))
            x = _post(x, (o_p, o_s), modt, ab_w_out_b, g_mlp, mlp_w1_b, mlp_w2_b, fin_g, layer, m, cfg, False, final)
        else:
            shift_rows = state_shift[m].reshape(bs, 1, d)
            outs = _rw_in(x, modt, g_mix, shift_rows, W, seg, v_first, layer, m, cfg)
            seqs, gate, bonus, hlast = outs[0:6], outs[6], outs[7], outs[8]
            if v_first is None:
                v_first = outs[9]
            y_p, s_p = _rw_rec(seqs, None, cfg, False)
            sw = state_wkv[m].reshape(bs, npair, 2, RW_N, RW_N)
            zero = jnp.zeros_like(sw[:, :, 0])
            s_bd = jnp.concatenate([jnp.concatenate([sw[:, :, 0], zero], axis=-1),
                                    jnp.concatenate([zero, sw[:, :, 1]], axis=-1)], axis=-2)
            y_s, s_s = _rw_rec(seqs, jnp.swapaxes(s_bd, 0, 1), cfg, True)
            unbd = lambda a: jnp.stack([a[..., :RW_N, :RW_N], a[..., RW_N:, RW_N:]], axis=-3)
            wkv_p.append(unbd(s_p).reshape(1, 1, nh, RW_N, RW_N))
            wkv_s.append(unbd(jnp.swapaxes(s_s, 0, 1)).reshape(bs, nh, RW_N, RW_N))
            sh_p.append(hlast[cfg["npt"] * ng - 1][None, None])
            sh_s.append(hlast[cfg["npt"] * ng:][None])
            x = _post(x, (y_p, y_s, bonus, gate, vec(rw_ln_g), vec(rw_ln_b), seg), modt, rw_w_out_b, g_mlp,
                      mlp_w1_b, mlp_w2_b, fin_g, layer, m, cfg, True, final)

    out_p, out_s = x
    return (out_p.reshape(bp, tp, d), out_s.reshape(bs, ts, d),
            jnp.stack(ret_p), jnp.stack(ret_s), jnp.stack(hg_p), jnp.stack(hg_s),
            jnp.concatenate(wkv_p, axis=0), jnp.stack(wkv_s),
            jnp.concatenate(sh_p, axis=0), jnp.concatenate(sh_s, axis=0))
```

```python
import functools

import numpy as np
import jax
import jax.numpy as jnp
from jax import lax
from jax.experimental import pallas as pl
from jax.experimental.pallas import tpu as pltpu

F32 = jnp.float32
BF16 = jnp.bfloat16

CHUNK = 64
PAST_LEN = 2048
ROPE_BASE = 10000.0
RET_HEADS = 4
HG_HEADS = 4
RW_N = 64
NORM_EPS = 1e-6
RW_LN_EPS = 64e-5

LANES = 128
SUB = 16
RW_GROUP = 2
RW_WAVE = 2
TM_RW = 256
ROW_BLOCK = 256
TM_IN = 512
VMEM_LIMIT_IN = 60 * 1024 * 1024
TM = 256
VMEM_LIMIT = 56 * 1024 * 1024


def _sigmoid(x):
    return 1.0 / (1.0 + jnp.exp(-x))


def _split2(x):
    hi = x.astype(BF16)
    lo = (x - hi.astype(F32)).astype(BF16)
    return hi, lo


def _split3(x):
    a = x.astype(BF16)
    r = x - a.astype(F32)
    b = r.astype(BF16)
    c = (r - b.astype(F32)).astype(BF16)
    return a, b, c


_NN = (((1,), (0,)), ((), ()))
_NT = (((1,), (1,)), ((), ()))
_TN = (((0,), (0,)), ((), ()))


def _dg(a, b, dims):
    return lax.dot_general(a, b, dims, preferred_element_type=F32)


def _dot1(a, b, dims=_NN):
    return _dg(a.astype(BF16), b.astype(BF16), dims)


def _dot3(a, b, dims=_NN):
    ah, al = _split2(a)
    bh, bl = _split2(b)
    return _dg(ah, bh, dims) + _dg(ah, bl, dims) + _dg(al, bh, dims)


def _dotp(a, b, dims, passes):
    return _dot1(a, b, dims) if passes == 1 else _dot3(a, b, dims)


P_BIG = 1
P_ST = 1


def _cumsum_rows(tri_bf16, x):
    a, b, c = _split3(x)
    return _dg(tri_bf16, a, _NN) + _dg(tri_bf16, b, _NN) + _dg(tri_bf16, c, _NN)


def _segsum(x, seg_bf16):
    per_head = _dg(x.astype(BF16), seg_bf16, _NN)
    return _dg(per_head.astype(BF16), seg_bf16, _NT)


def _rms(x, g):
    ms = jnp.mean(x * x, axis=-1, keepdims=True)
    return x * lax.rsqrt(ms + NORM_EPS) * g


def _modulate(y, sc, sh):
    rows, d = y.shape
    ng = sc.shape[0]
    y3 = y.reshape(ng, rows // ng, d)
    return (y3 * (1.0 + sc) + sh).reshape(rows, d)


def _gate_add(x, gt, out):
    rows, d = x.shape
    ng = gt.shape[0]
    return (x.reshape(ng, rows // ng, d) + gt * out.reshape(ng, rows // ng, d)).reshape(rows, d)


def _rows_specs(x, npt):
    if isinstance(x, tuple):
        d = x[0].shape[1]
        return list(x), [pl.BlockSpec((TM, d), lambda i: (jnp.minimum(i, npt - 1), 0)),
                         pl.BlockSpec((TM, d), lambda i: (jnp.maximum(i - npt, 0), 0))]
    return [x], [pl.BlockSpec((TM, x.shape[1]), lambda i: (i, 0))]


def _rows_value(refs, npt):
    if len(refs) == 1:
        return refs[0][...]
    p_ref, s_ref = refs
    rows, d = p_ref.shape
    s_i = jnp.where(pl.program_id(0) >= npt, 1, 0)
    is_sample = (lax.broadcasted_iota(jnp.int32, (rows, LANES), 0) * 0 + s_i) == 1
    return jnp.concatenate([jnp.where(is_sample, s_ref[:, LANES * p:LANES * (p + 1)],
                                      p_ref[:, LANES * p:LANES * (p + 1)]) for p in range(d // LANES)], axis=-1)


def _const_spec(block, index):
    return pl.BlockSpec(block, lambda *_: index, pipeline_mode=pl.Buffered(1))


def _params(sem):
    return pltpu.CompilerParams(dimension_semantics=sem, vmem_limit_bytes=VMEM_LIMIT)


def _mod_kernel(c_ref, w_ref, b_ref, o_ref):
    c = c_ref[...]
    s = c * _sigmoid(c)
    o_ref[0] = _dot1(s, w_ref[0]) + b_ref[0]


def _modulation(c_all, mod_w, mod_b):
    depth, d, d6 = mod_w.shape
    rows = c_all.shape[0]
    nt = d6 // d
    return pl.pallas_call(
        _mod_kernel,
        grid=(depth, nt),
        in_specs=[
            pl.BlockSpec((rows, d), lambda l, j: (0, 0)),
            pl.BlockSpec((1, d, d), lambda l, j: (l, 0, j)),
            pl.BlockSpec((1, 1, d), lambda l, j: (l, 0, j)),
        ],
        out_specs=pl.BlockSpec((1, rows, d), lambda l, j: (l, 0, j)),
        out_shape=jax.ShapeDtypeStruct((depth, rows, d6), F32),
        compiler_params=_params(("arbitrary", "arbitrary")),
        name="modulation",
    )(c_all, mod_w, mod_b.reshape(depth, 1, d6))


def _ab_in_kernel(*refs, d, npt, dk_scale):
    mod_ref, g_ref, w_ref, cos_ref, sin_ref, lb_ref, o_ref = refs[-7:]
    x = _rows_value(refs[:-7], npt)
    h = _modulate(_rms(x, g_ref[0]), mod_ref[0, :, :, d:2 * d], mod_ref[0, :, :, 0:d])
    hb = h.astype(BF16)
    hw = d // 2

    def proj(a):
        return _dg(hb, w_ref[0, :, a:a + hw], _NN)

    z = proj(4 * hw)
    log_lb = lb_ref[0:1, :]
    log1m_lb = lb_ref[1:2, :]
    ls = jnp.minimum(z, 0.0) - jnp.log1p(jnp.exp(-jnp.abs(z)))
    c = log1m_lb + ls
    m = jnp.maximum(log_lb, c)
    o_ref[:, 4 * hw:5 * hw] = m + jnp.log1p(jnp.exp(-jnp.abs(log_lb - c)))
    o_ref[:, 5 * hw:6 * hw] = c - z
    qk = proj(0)
    lane = lax.broadcasted_iota(jnp.int32, (x.shape[0], LANES), 1)
    first_half = (lane % 64) < 32
    cos = cos_ref[...]
    sin = sin_ref[...]
    for s in range(hw // LANES):
        xs = qk[:, LANES * s:LANES * (s + 1)]
        sw = jnp.where(first_half, pltpu.roll(xs, 96, 1), pltpu.roll(xs, 32, 1))
        rot = xs * cos + sw * sin
        if s < hw // (2 * LANES):
            rot = rot * dk_scale
        o_ref[:, LANES * s:LANES * (s + 1)] = rot
    o_ref[:, hw:2 * hw] = proj(hw)
    z = proj(2 * hw)
    o_ref[:, 2 * hw:3 * hw] = z * _sigmoid(z)
    z = proj(3 * hw)
    o_ref[:, 3 * hw:4 * hw] = z * _sigmoid(z)
    o_ref[:, 6 * hw:7 * hw] = proj(5 * hw)
    o_ref[:, 7 * hw:8 * hw] = _sigmoid(proj(6 * hw))


def _ab_in(x, modt, g, w_in, cos_t, sin_t, lbc, layer, m, cfg):
    ng, npt, n = cfg["ng"], cfg["npt"], cfg["n"]
    x_args, x_specs = _rows_specs(x, npt)
    d = x_args[0].shape[1]
    nt = n // TM
    zw = 4 * d
    kern = functools.partial(_ab_in_kernel, d=d, npt=npt, dk_scale=float((d // 16) ** -0.5))
    return pl.pallas_call(
        kern,
        grid=(nt,),
        in_specs=x_specs + [
            pl.BlockSpec((1, ng, 1, 6 * d), lambda i: (layer, jnp.maximum(i - npt + cfg["mod_off"], 0), 0, 0)),
            _const_spec((1, 1, d), (layer, 0, 0)),
            _const_spec((1, d, w_in.shape[2]), (m, 0, 0)),
            pl.BlockSpec((TM, LANES), lambda i: (i, 0)),
            pl.BlockSpec((TM, LANES), lambda i: (i, 0)),
            _const_spec((8, d // 2), (m, 0)),
        ],
        out_specs=pl.BlockSpec((TM, zw), lambda i: (i, 0)),
        out_shape=jax.ShapeDtypeStruct((n, zw), F32),
        compiler_params=_params(("arbitrary",)),
        name="ab_in",
    )(*x_args, modt, g, w_in, cos_t, sin_t, lbc)


def _ab_rec_kernel(*refs, d, c, sample):
    if sample:
        (z_ref, dmat_ref, rdec_ref, cdec_ref, gc_ref, tri_ref, hgg_ref, ones_ref, sret_in, shg_in,
         o_ref, sret_out, shg_out, sret_scr, shg_scr) = refs
    else:
        (z_ref, dmat_ref, rdec_ref, cdec_ref, gc_ref, tri_ref, hgg_ref, ones_ref,
         o_ref, sret_out, shg_out, sret_scr, shg_scr) = refs
    hw = d // 2
    rows = z_ref.shape[0]
    n_chunks = rows // c
    dk = d // 16

    if not sample:
        @pl.when(pl.program_id(0) == 0)
        def _():
            sret_scr[...] = jnp.zeros_like(sret_scr)
            shg_scr[...] = jnp.zeros_like(shg_scr)

    lane_c = lax.broadcasted_iota(jnp.int32, (c, LANES), 1)
    row_h = lax.broadcasted_iota(jnp.int32, (SUB // 2, LANES), 0)
    ones = ones_ref[...]
    tri = tri_ref[...]

    def hg_chunk(r0, b):
        col = lambda g, h: slice(g * hw + LANES * h, g * hw + LANES * (h + 1))
        units = [(i, h) for i in range(c // SUB) for h in range(HG_HEADS)]
        rs = {i: pl.ds(r0 + SUB * i, SUB) for i in range(c // SUB)}
        q = {u: z_ref[rs[u[0]], col(3, u[1])] for u in units}
        v = {u: z_ref[rs[u[0]], col(6, u[1])] for u in units}
        bs = {(i, h): b[SUB * i:SUB * (i + 1), LANES * h:LANES * (h + 1)] for (i, h) in units}
        bl = {u: bs[u][SUB - 1:SUB, :] for u in units}
        w = {u: bs[u] - z_ref[rs[u[0]], col(5, u[1])] for u in units}
        upd = {u: _dot1(v[u], jnp.exp(bl[u] - w[u]), _TN) for u in units}
        hs = SUB // 2
        r = {}
        for u in units:
            q_lo, q_hi = q[u][0:hs], q[u][hs:SUB]
            b_lo, b_hi = bs[u][0:hs], bs[u][hs:SUB]
            p_lo, p_hi = [], []
            for j in range(SUB):
                wj = w[u][j:j + 1, :]
                if j < hs:
                    p_lo.append(jnp.where(row_h >= j, q_lo * jnp.exp(b_lo - wj), 0.0))
                    p_hi.append(q_hi * jnp.exp(b_hi - wj))
                else:
                    p_hi.append(jnp.where(row_h >= j - hs, q_hi * jnp.exp(b_hi - wj), 0.0))
            r[u] = _dot1(jnp.concatenate(p_lo + p_hi, axis=0), ones)
        inner = {}
        for u in units:
            acc_lo = r[u][0:hs] * v[u][0:1, :]
            for j in range(1, hs):
                acc_lo = acc_lo + r[u][hs * j:hs * (j + 1)] * v[u][j:j + 1, :]
            base = hs * hs
            acc_hi = r[u][base:base + hs] * v[u][0:1, :]
            for j in range(1, SUB):
                acc_hi = acc_hi + r[u][base + hs * j:base + hs * (j + 1)] * v[u][j:j + 1, :]
            inner[u] = jnp.concatenate([acc_lo, acc_hi], axis=0)
        st = {h: shg_scr[h] for h in range(HG_HEADS)}
        for u in units:
            i, h = u
            out = inner[u] + _dot1(q[u] * jnp.exp(bs[u]), st[h], _NT)
            st[h] = st[h] * jnp.exp(bl[u]) + upd[u]
            ms = jnp.mean(out * out, axis=-1, keepdims=True)
            o_ref[rs[i], col(1, h)] = out * lax.rsqrt(ms + NORM_EPS) * hgg_ref[...] * z_ref[rs[i], col(7, h)]
        for h in range(HG_HEADS):
            shg_scr[h] = st[h]

    def ret_chunk(r0):
        rs = pl.ds(r0, c)
        heads = range(RET_HEADS)
        zeros = jnp.zeros((dk, LANES), F32)
        qp = [z_ref[rs, LANES * p:LANES * (p + 1)] for p in range(RET_HEADS // 2)]
        kp = [z_ref[rs, hw // 2 + LANES * p:hw // 2 + LANES * (p + 1)] for p in range(RET_HEADS // 2)]
        v = [z_ref[rs, hw + LANES * h:hw + LANES * (h + 1)] for h in heads]
        s = [sret_scr[h] for h in heads]
        qh = [jnp.where((lane_c // dk) == (h % 2), qp[h // 2], 0.0) for h in heads]
        scores = [_dot1(qh[h], kp[h // 2], _NT) * dmat_ref[h] for h in heads]
        spad = [jnp.concatenate([s[h], zeros] if h % 2 == 0 else [zeros, s[h]], axis=0) for h in heads]
        cross = [_dot1(qp[h // 2], spad[h]) * rdec_ref[h] for h in heads]
        u = [_dot1(kp[h // 2] * cdec_ref[h], v[h], _TN) for h in heads]
        inner = [_dot1(scores[h], v[h]) for h in heads]
        for h in heads:
            e = h % 2
            out = inner[h] + cross[h]
            sret_scr[h] = s[h] * gc_ref[h] + u[h][dk * e:dk * (e + 1)]
            ms = jnp.mean(out * out, axis=-1, keepdims=True)
            o_ref[rs, LANES * h:LANES * (h + 1)] = (out * lax.rsqrt(ms + NORM_EPS)
                                                    * z_ref[rs, 2 * hw + LANES * h:2 * hw + LANES * (h + 1)])

    def chunk_body(ci, carry):
        r0 = pl.multiple_of(ci * c, c)
        if sample:
            sret_scr[...] = sret_in[ci]
            shg_scr[...] = shg_in[ci]
        hg_chunk(r0, _cumsum_rows(tri, z_ref[pl.ds(r0, c), 4 * hw:5 * hw]))
        ret_chunk(r0)
        if sample:
            sret_out[ci] = sret_scr[...]
            shg_out[ci] = shg_scr[...]
        return carry

    lax.fori_loop(0, n_chunks, chunk_body, 0, unroll=2)

    if not sample:
        @pl.when(pl.program_id(0) == pl.num_programs(0) - 1)
        def _():
            sret_out[...] = sret_scr[...]
            shg_out[...] = shg_scr[...]


def _ret_consts(c):
    heads = jnp.arange(RET_HEADS, dtype=F32)
    log_g = jnp.log1p(-jnp.exp2(-5.0 - heads))
    idx = jnp.arange(c, dtype=F32)
    rel = idx[:, None] - idx[None, :]
    dmat = jnp.exp(jnp.where(rel[None] >= 0, rel[None] * log_g[:, None, None], -jnp.inf))
    rdec = jnp.exp((idx[None, :] + 1.0) * log_g[:, None])
    cdec = jnp.exp((c - 1.0 - idx)[None, :] * log_g[:, None])
    gc = jnp.exp(c * log_g)
    bc = lambda a: jnp.broadcast_to(a[:, :, None], (RET_HEADS, c, LANES))
    return dmat, bc(rdec), bc(cdec), gc


def _ab_rec(z, hg_g, sret, shg_t, cfg, sample, d):
    n = z.shape[0]
    ng, npt = cfg["ng"], cfg["npt"]
    c = cfg["ts"] if sample else CHUNK
    dk, dv = d // 16, d // 8
    dmat, rdec, cdec, gc = _ret_consts(c)
    gcb = jnp.broadcast_to(gc[:, None, None], (RET_HEADS, dk, LANES))
    tri = np.kron(np.eye(c // SUB), np.tril(np.ones((SUB, SUB))))
    tri = jnp.asarray(tri, BF16)
    ones = jnp.ones((LANES, LANES), BF16)
    kern = functools.partial(_ab_rec_kernel, d=d, c=c, sample=sample)
    consts = [dmat, rdec, cdec, gcb, tri, hg_g.reshape(1, dv), ones]
    const_specs = [
        _const_spec(dmat.shape, (0, 0, 0)), _const_spec(rdec.shape, (0, 0, 0)),
        _const_spec(cdec.shape, (0, 0, 0)), _const_spec(gcb.shape, (0, 0, 0)),
        _const_spec(tri.shape, (0, 0)), _const_spec((1, dv), (0, 0)), _const_spec(ones.shape, (0, 0)),
    ]
    scratch = [pltpu.VMEM((RET_HEADS, dk, dv), F32), pltpu.VMEM((HG_HEADS, dv, dv), F32)]
    if sample:
        nb = sret.shape[0]
        nst = nb // ng
        return pl.pallas_call(
            kern,
            grid=(nst,),
            in_specs=[pl.BlockSpec((TM, z.shape[1]), lambda i: (npt + i, 0))] + const_specs + [
                pl.BlockSpec((ng, RET_HEADS, dk, dv), lambda i: (i, 0, 0, 0)),
                pl.BlockSpec((ng, HG_HEADS, dv, dv), lambda i: (i, 0, 0, 0)),
            ],
            out_specs=[
                pl.BlockSpec((TM, d), lambda i: (i, 0)),
                pl.BlockSpec((ng, RET_HEADS, dk, dv), lambda i: (i, 0, 0, 0)),
                pl.BlockSpec((ng, HG_HEADS, dv, dv), lambda i: (i, 0, 0, 0)),
            ],
            out_shape=[
                jax.ShapeDtypeStruct((nst * TM, d), F32),
                jax.ShapeDtypeStruct((nb, RET_HEADS, dk, dv), F32),
                jax.ShapeDtypeStruct((nb, HG_HEADS, dv, dv), F32),
            ],
            scratch_shapes=scratch,
            compiler_params=_params(("arbitrary",)),
            name="ab_rec_sample",
        )(z, *consts, sret, shg_t)
    return pl.pallas_call(
        kern,
        grid=(npt,),
        in_specs=[pl.BlockSpec((TM, z.shape[1]), lambda i: (i, 0))] + const_specs,
        out_specs=[
            pl.BlockSpec((TM, d), lambda i: (i, 0)),
            pl.BlockSpec((RET_HEADS, dk, dv), lambda i: (0, 0, 0)),
            pl.BlockSpec((HG_HEADS, dv, dv), lambda i: (0, 0, 0)),
        ],
        out_shape=[
            jax.ShapeDtypeStruct((npt * TM, d), F32),
            jax.ShapeDtypeStruct((RET_HEADS, dk, dv), F32),
            jax.ShapeDtypeStruct((HG_HEADS, dv, dv), F32),
        ],
        scratch_shapes=scratch,
        compiler_params=_params(("arbitrary",)),
        name="ab_rec_prompt",
    )(z, *consts)


def _post_kernel(*refs, d, npt, nx, rwkv, final):
    x = _rows_value(refs[:nx], npt)
    it = iter(refs[nx:])
    if rwkv:
        yp_ref, ys_ref, bonus_ref, gate_ref, lng_ref, lnb_ref, seg_ref = (next(it) for _ in range(7))
    else:
        op_ref, os_ref = next(it), next(it)
    mod_ref, wout_ref, g2_ref, w1_ref, w2_ref = (next(it) for _ in range(5))
    if final:
        fg_ref = next(it)
        outp_ref, outs_ref = next(it), next(it)
    else:
        out_ref = next(it)

    i = pl.program_id(0)
    if rwkv:
        s_i = jnp.where(i >= npt, 1, 0)
        is_sample = (lax.broadcasted_iota(jnp.int32, (x.shape[0], LANES), 0) * 0 + s_i) == 1
        y = jnp.concatenate([jnp.where(is_sample, ys_ref[p], yp_ref[p]) for p in range(yp_ref.shape[0])],
                            axis=-1)
        seg = seg_ref[...]
        inv_n = 1.0 / RW_N
        mean = _segsum(y, seg) * inv_n
        yc = y - mean
        var = _segsum(yc * yc, seg) * inv_n
        o = yc * lax.rsqrt(var + RW_LN_EPS) * lng_ref[0] + lnb_ref[0]
        o = (o + bonus_ref[...]) * gate_ref[...]
    else:
        o = _rows_value((op_ref, os_ref), npt)
    gt1 = mod_ref[0, :, :, 2 * d:3 * d]
    sh2 = mod_ref[0, :, :, 3 * d:4 * d]
    sc2 = mod_ref[0, :, :, 4 * d:5 * d]
    gt2 = mod_ref[0, :, :, 5 * d:6 * d]
    x1 = _gate_add(x, gt1, _dg(o.astype(BF16), wout_ref[0], _NN))
    hb = _modulate(_rms(x1, g2_ref[0]), sc2, sh2).astype(BF16)
    ff = w1_ref.shape[2]
    acc = jnp.zeros_like(x1)
    for j in range(ff // d):
        u = jnp.maximum(_dg(hb, w1_ref[0, :, j * d:(j + 1) * d], _NN), 0.0)
        acc = acc + _dg((u * u).astype(BF16), w2_ref[0, j * d:(j + 1) * d, :], _NN)
    x2 = _gate_add(x1, gt2, acc)
    if final:
        x2 = _rms(x2, fg_ref[...])

        @pl.when(i < npt)
        def _():
            outp_ref[...] = x2

        @pl.when(i >= npt)
        def _():
            outs_ref[...] = x2
    else:
        out_ref[...] = x2


def _post(x, mixer_in, modt, wout, g2, w1, w2, final_g, layer, m, cfg, rwkv, final):
    ng, npt, n = cfg["ng"], cfg["npt"], cfg["n"]
    args, specs = _rows_specs(x, npt)
    d = args[0].shape[1]
    nt = n // TM
    kern = functools.partial(_post_kernel, d=d, npt=npt, nx=len(args), rwkv=rwkv, final=final)
    tok = pl.BlockSpec((TM, d), lambda i: (i, 0))
    tok_p = pl.BlockSpec((TM, d), lambda i: (jnp.minimum(i, npt - 1), 0))
    tok_s = pl.BlockSpec((TM, d), lambda i: (jnp.maximum(i - npt, 0), 0))
    if rwkv:
        y_p, y_s, bonus, gate, lng, lnb, seg = mixer_in
        args += [y_p, y_s, bonus, gate, lng, lnb, seg]
        specs += [pl.BlockSpec((y_p.shape[0], TM, LANES), lambda i: (0, jnp.minimum(i, npt - 1), 0)),
                  pl.BlockSpec((y_s.shape[0], TM, LANES), lambda i: (0, jnp.maximum(i - npt, 0), 0)), tok, tok,
                  _const_spec((1, 1, d), (m, 0, 0)), _const_spec((1, 1, d), (m, 0, 0)),
                  _const_spec((d, LANES), (0, 0))]
    else:
        args += list(mixer_in)
        specs += [tok_p, tok_s]
    args += [modt, wout, g2, w1, w2]
    specs += [
        pl.BlockSpec((1, ng, 1, 6 * d), lambda i: (layer, jnp.maximum(i - npt + cfg["mod_off"], 0), 0, 0)),
        _const_spec((1, d, d), (m, 0, 0)),
        _const_spec((1, 1, d), (layer, 0, 0)),
        _const_spec((1, d, w1.shape[2]), (layer, 0, 0)),
        _const_spec((1, w2.shape[1], d), (layer, 0, 0)),
    ]
    if final:
        args += [final_g]
        specs += [_const_spec((1, d), (0, 0))]
    if final:
        out_specs = [tok_p, tok_s]
        out_shape = [jax.ShapeDtypeStruct((npt * TM, d), F32), jax.ShapeDtypeStruct((n - npt * TM, d), F32)]
    else:
        out_specs = tok
        out_shape = jax.ShapeDtypeStruct((n, d), F32)
    return pl.pallas_call(
        kern,
        grid=(nt,),
        in_specs=specs,
        out_specs=out_specs,
        out_shape=out_shape,
        compiler_params=_params(("arbitrary",)),
        name="post_rwkv" if rwkv else "post_ab",
    )(*args)


def _rw_in_kernel(*refs, d, ts, npt, vres):
    it = iter(refs)
    x_ref, mod_ref, g_ref, shift_ref, mu_ref, wrkv_ref = (next(it) for _ in range(6))
    w0_ref, w1_ref, w2_ref, a0_ref, a1_ref, a2_ref, g1_ref, g2_ref = (next(it) for _ in range(8))
    kk_ref, ka_ref, rk_ref, seg_ref = (next(it) for _ in range(4))
    if vres:
        vf_ref, v0_ref, v1_ref, v2_ref = (next(it) for _ in range(4))
    r_out, ld_out, k_out, v_out, kk_out, kka_out, gate_out, bonus_out, hlast_out = (next(it) for _ in range(9))
    if not vres:
        vtok_out = next(it)
    h_scr, carry_scr = next(it), next(it)

    i = pl.program_id(0)
    rows = x_ref.shape[0]
    ng = rows // ts

    @pl.when(i == 0)
    def _():
        carry_scr[...] = jnp.zeros_like(carry_scr)

    x = x_ref[...]
    h = _modulate(_rms(x, g_ref[0]), mod_ref[0, :, :, d:2 * d], mod_ref[0, :, :, 0:d])
    for p in range(d // LANES):
        h_scr[p] = h[:, LANES * p:LANES * (p + 1)]
        hlast_out[:, LANES * p:LANES * (p + 1)] = h_scr[p, pl.ds(ts - 1, ng, stride=ts), :]
    rolled = pltpu.roll(h, 1, 0)
    row = lax.broadcasted_iota(jnp.int32, (rows, d), 0)
    s_i = jnp.where(i >= npt, 1, 0)
    first = ((row % ts) == 0) & ((row * (1 - s_i)) == 0)
    seq_prev = jnp.broadcast_to(shift_ref[...], (ng, ts, d)).reshape(rows, d)
    prev_first = jnp.where((row * 0 + s_i) == 1, seq_prev, jnp.broadcast_to(carry_scr[0:1, :], (rows, d)))
    prev = jnp.where(first, prev_first, rolled)
    carry_scr[0:1, :] = h[rows - 1:rows, :]

    seg = seg_ref[...]

    def block(b0):
        rb = slice(b0, b0 + ROW_BLOCK)
        hb = h[rb]
        xx = prev[rb] - hb
        mix = lambda j: (hb + xx * mu_ref[0, j:j + 1, :]).astype(BF16)
        xr, xw, xk, xv, xa, xg = (mix(j) for j in range(6))
        yield
        w_dn = _dg(xw, w1_ref[0], _NN)
        a_dn = _dg(xa, a1_ref[0], _NN)
        g_dn = _dg(xg, g1_ref[0], _NN)
        if vres:
            v_dn = _dg(xv, v1_ref[0], _NN)
        r = _dg(xr, wrkv_ref[0, 0], _NN)
        yield
        k = _dg(xk, wrkv_ref[0, 1], _NN)
        yield
        v = _dg(xv, wrkv_ref[0, 2], _NN)
        yield
        kk = k * kk_ref[0]
        kk_ss = _segsum(kk * kk, seg)
        wl = w0_ref[0] + _dot1(jnp.tanh(w_dn), w2_ref[0])
        a = _sigmoid(a0_ref[0] + _dot1(a_dn, a2_ref[0]))
        gate_out[rb, :] = _dot1(_sigmoid(g_dn), g2_ref[0])
        if vres:
            lv = v0_ref[0] + _dot1(v_dn, v2_ref[0])
            v = v + (vf_ref[rb, :] - v) * _sigmoid(lv)
        else:
            vtok_out[rb, :] = v
        yield
        nwl = -wl
        w = -(jnp.maximum(nwl, 0.0) + jnp.log1p(jnp.exp(-jnp.abs(nwl)))) - 0.5
        ld = -jnp.exp(w)
        kk = kk / jnp.maximum(jnp.sqrt(kk_ss), 1e-12)
        km = k * (1.0 + (a - 1.0) * ka_ref[0])
        bonus_out[rb, :] = _segsum(r * km * rk_ref[0], seg) * v
        kka = kk * a
        for p in range(d // LANES):
            cs = slice(LANES * p, LANES * (p + 1))
            r_out[p, rb, :] = r[:, cs]
            ld_out[p, rb, :] = ld[:, cs]
            k_out[p, rb, :] = km[:, cs]
            v_out[p, rb, :] = v[:, cs]
            kk_out[p, rb, :] = kk[:, cs]
            kka_out[p, rb, :] = kka[:, cs]
        yield

    live = []
    starts = list(range(0, rows, ROW_BLOCK))
    while starts or live:
        if starts:
            live.append(block(starts.pop(0)))
        for g in list(live):
            try:
                next(g)
            except StopIteration:
                live.remove(g)


def _rw_in(x, modt, g, shift_rows, W, seg, v_first, layer, m, cfg):
    n, d = x.shape
    ts, tm = cfg["ts"], TM_IN
    ng, npt = tm // ts, cfg["npt"] * TM // tm
    nt = n // tm
    npair = d // LANES
    vres = v_first is not None
    kern = functools.partial(_rw_in_kernel, d=d, ts=ts, npt=npt, vres=vres)
    tok = pl.BlockSpec((tm, d), lambda i: (i, 0))
    vec = lambda: _const_spec((1, 1, d), (m, 0, 0))
    lora = lambda a: _const_spec((1,) + a.shape[1:], (m, 0, 0))
    args = [x, modt, g, shift_rows, W["rw_mu"], W["rw_w_rkv"],
            W["rw_w0"], W["rw_w1"], W["rw_w2"], W["rw_a0"], W["rw_a1"], W["rw_a2"], W["rw_g1"], W["rw_g2"],
            W["rw_k_k"], W["rw_k_a"], W["rw_r_k"], seg]
    specs = [
        tok,
        pl.BlockSpec((1, ng, 1, 2 * d), lambda i: (layer, jnp.maximum(i - npt + 1, 0), 0, 0)),
        _const_spec((1, 1, d), (layer, 0, 0)),
        pl.BlockSpec((ng, 1, d), lambda i: (jnp.maximum(i - npt, 0), 0, 0)),
        _const_spec((1, 6, d), (m, 0, 0)),
        _const_spec((1, 3, d, d), (m, 0, 0, 0)),
        vec(), lora(W["rw_w1"]), lora(W["rw_w2"]), vec(), lora(W["rw_a1"]), lora(W["rw_a2"]),
        lora(W["rw_g1"]), lora(W["rw_g2"]), vec(), vec(), vec(), _const_spec((d, LANES), (0, 0)),
    ]
    if vres:
        args += [v_first, W["rw_v0"], W["rw_v1"], W["rw_v2"]]
        specs += [tok, _const_spec((1, 1, d), (m - 1, 0, 0)),
                  _const_spec((1,) + W["rw_v1"].shape[1:], (m - 1, 0, 0)),
                  _const_spec((1,) + W["rw_v2"].shape[1:], (m - 1, 0, 0))]
    pm = pl.BlockSpec((npair, tm, LANES), lambda i: (0, i, 0))
    pm_shape = jax.ShapeDtypeStruct((npair, n, LANES), F32)
    tok_shape = jax.ShapeDtypeStruct((n, d), F32)
    out_specs = [pm] * 6 + [tok, tok, pl.BlockSpec((ng, d), lambda i: (i, 0))]
    out_shape = [pm_shape] * 6 + [tok_shape, tok_shape, jax.ShapeDtypeStruct((nt * ng, d), F32)]
    if not vres:
        out_specs.append(tok)
        out_shape.append(tok_shape)
    return pl.pallas_call(
        kern,
        grid=(nt,),
        in_specs=specs,
        out_specs=out_specs,
        out_shape=out_shape,
        scratch_shapes=[pltpu.VMEM((npair, tm, LANES), F32), pltpu.VMEM((8, d), F32)],
        compiler_params=pltpu.CompilerParams(dimension_semantics=("arbitrary",), vmem_limit_bytes=VMEM_LIMIT_IN),
        name="rw_in",
    )(*args)


def _rw_rec_kernel(*refs, c, sample):
    if sample:
        (r_ref, ld_ref, k_ref, v_ref, kk_ref, kka_ref, tri_ref, ms_ref, mi_ref, eye_ref, s_in,
         y_ref, s_out, s_scr) = refs
    else:
        (r_ref, ld_ref, k_ref, v_ref, kk_ref, kka_ref, tri_ref, ms_ref, mi_ref, eye_ref,
         y_ref, s_out, s_scr) = refs
    npair, rows, _ = r_ref.shape
    n_chunks = rows // c
    n = 2 * c

    if not sample:
        @pl.when(pl.program_id(0) == 0)
        def _():
            s_scr[...] = jnp.zeros_like(s_scr)

    left_k = lax.broadcasted_iota(jnp.int32, (c, LANES), 1) < RW_N
    left_t = lax.broadcasted_iota(jnp.int32, (c, n), 1) < c
    tri = tri_ref[...]
    mask_strict = ms_ref[...]
    mask_incl = mi_ref[...]
    eye = eye_ref[...]
    steps = int(np.log2(c)) - 1

    def stack(xv, left):
        return jnp.concatenate([jnp.where(left, xv, 0.0), jnp.where(left, 0.0, xv)], axis=0)

    stack_k = lambda xv: stack(xv, left_k)
    stack_t = lambda xv: stack(xv, left_t)
    pairs = range(npair)

    nu = min(RW_GROUP, n_chunks)

    def group_body(gi, carry):
        ci = {j: gi * nu + j for j in range(nu)}
        rs = {j: pl.ds(pl.multiple_of(ci[j] * c, c), c) for j in range(nu)}
        ur, bk_end, vst, ltot, a_r, t, mv = {}, {}, {}, {}, {}, {}, {}
        state = {"s": None if sample else [s_scr[p] for p in pairs]}

        def head(js):
            units = [(j, p) for j in js for p in pairs]
            bk = {}
            for j in js:
                ld_all = jnp.concatenate([ld_ref[p, rs[j], :] for p in pairs], axis=1)
                lc_all = _cumsum_rows(tri, ld_all)
                for p in pairs:
                    lc = lc_all[:, LANES * p:LANES * (p + 1)]
                    ld = ld_all[:, LANES * p:LANES * (p + 1)]
                    lt = lc[c - 1:c, :]
                    g_inv = jnp.exp(-lc)
                    g_end = jnp.exp(lt - lc)
                    k = k_ref[p, rs[j], :]
                    kka = kka_ref[p, rs[j], :]
                    ur[j, p] = jnp.concatenate([-kk_ref[p, rs[j], :] * jnp.exp(lc - ld),
                                                r_ref[p, rs[j], :] * jnp.exp(lc)], axis=0)
                    bk[j, p] = jnp.concatenate([stack_k(kka * g_inv), stack_k(k * g_inv)], axis=0)
                    bk_end[j, p] = jnp.concatenate([stack_k(kka * g_end), stack_k(k * g_end)], axis=0)
                    vst[j, p] = stack_k(v_ref[p, rs[j], :])
                    ltot[j, p] = lt
                yield
            big = {u: _dotp(ur[u], bk[u], _NT, P_BIG) for u in units}
            m_ub = {u: big[u][0:c, 0:n] * mask_strict for u in units}
            m_uk = {u: big[u][0:c, n:2 * n] * mask_strict for u in units}
            for u in units:
                a_r[u] = jnp.concatenate([big[u][c:n, 0:n] * mask_incl, big[u][c:n, n:2 * n] * mask_incl], axis=1)
            yield
            tt = {u: eye + m_ub[u] for u in units}
            mp = {u: _dot1(m_ub[u], stack_t(m_ub[u])) for u in units}
            yield
            for _ in range(steps - 1):
                both = {u: _dot1(jnp.concatenate([tt[u], mp[u]], axis=0), stack_t(mp[u])) for u in units}
                tt = {u: tt[u] + both[u][0:c] for u in units}
                mp = {u: both[u][c:n] for u in units}
                yield
            tt = {u: tt[u] + _dot1(tt[u], stack_t(mp[u])) for u in units}
            yield
            res = {}
            for u in units:
                mh, ml = _split2(m_ub[u])
                th, tl = _split2(stack_t(tt[u]))
                hl = _dg(jnp.concatenate([mh, ml], axis=0), th, _NN)
                res[u] = eye - tt[u] + (hl[0:c] + hl[c:n] + _dg(mh, tl, _NN))
            yield
            for u in units:
                t[u] = tt[u] + _dot1(tt[u], stack_t(res[u]))
                mv[u] = _dotp(m_uk[u], vst[u], _NN, P_ST)
            yield

        def tail(js):
            for j in js:
                sj = [s_in[p, ci[j]] for p in pairs] if sample else state["s"]
                urs = [_dotp(ur[j, p], sj[p], _NT, P_ST) for p in pairs]
                yield
                e = [_dotp(t[j, p], stack_k(urs[p][0:c] + mv[j, p]), _NN, P_ST) for p in pairs]
                yield
                ev = [jnp.concatenate([stack_k(e[p]), vst[j, p]], axis=0) for p in pairs]
                upd = [_dotp(ev[p], bk_end[j, p], _TN, P_ST) for p in pairs]
                state["s"] = [sj[p] * jnp.exp(ltot[j, p]) + upd[p] for p in pairs]
                yield
                y = [urs[p][c:n] + _dotp(a_r[j, p], ev[p], _NN, P_ST) for p in pairs]
                for p in pairs:
                    y_ref[p, rs[j], :] = y[p]
                    if sample:
                        s_out[p, ci[j]] = state["s"][p]
                yield

        def run(*gens):
            live = list(gens)
            while live:
                for g in list(live):
                    try:
                        next(g)
                    except StopIteration:
                        live.remove(g)

        waves = [list(range(w, min(w + RW_WAVE, nu))) for w in range(0, nu, RW_WAVE)]
        run(head(waves[0]))
        for prev_wave, wave in zip(waves[:-1], waves[1:]):
            run(head(wave), tail(prev_wave))
        run(tail(waves[-1]))
        if not sample:
            for p in pairs:
                s_scr[p] = state["s"][p]
        return carry

    lax.fori_loop(0, n_chunks // nu, group_body, 0)

    if not sample:
        @pl.when(pl.program_id(0) == pl.num_programs(0) - 1)
        def _():
            s_out[...] = s_scr[...]


def _rw_rec(seqs, s_bd, cfg, sample):
    npair, n, _ = seqs[0].shape
    tm = TM if sample else TM_RW
    ng, npt = tm // cfg["ts"], cfg["npt"] * TM // tm
    c = cfg["ts"] if sample else CHUNK
    tri = jnp.asarray(np.tril(np.ones((c, c))), BF16)
    side = lambda a: jnp.asarray(np.concatenate([a, a], axis=1), F32)
    mask_incl = side(np.tril(np.ones((c, c))))
    mask_strict = side(np.tril(np.ones((c, c)), -1))
    eye = side(np.eye(c))
    consts = [tri, mask_strict, mask_incl, eye]
    const_specs = [_const_spec(a.shape, (0, 0)) for a in consts]
    kern = functools.partial(_rw_rec_kernel, c=c, sample=sample)
    scratch = [pltpu.VMEM((npair, LANES, LANES), F32)]
    tokp = pl.BlockSpec((npair, tm, LANES), lambda i: (0, i, 0))
    if sample:
        nb = s_bd.shape[1]
        nst = nb // ng
        tok_in = pl.BlockSpec((npair, tm, LANES), lambda i: (0, npt + i, 0))
        sspec = pl.BlockSpec((npair, ng, LANES, LANES), lambda i: (0, i, 0, 0))
        return pl.pallas_call(
            kern,
            grid=(nst,),
            in_specs=[tok_in] * 6 + const_specs + [sspec],
            out_specs=[tokp, sspec],
            out_shape=[jax.ShapeDtypeStruct((npair, nst * tm, LANES), F32),
                       jax.ShapeDtypeStruct((npair, nb, LANES, LANES), F32)],
            scratch_shapes=scratch,
            compiler_params=_params(("arbitrary",)),
            name="rw_rec_sample",
        )(*seqs, *consts, s_bd)
    return pl.pallas_call(
        kern,
        grid=(npt,),
        in_specs=[tokp] * 6 + const_specs,
        out_specs=[tokp, pl.BlockSpec((npair, LANES, LANES), lambda i: (0, 0, 0))],
        out_shape=[jax.ShapeDtypeStruct((npair, npt * tm, LANES), F32),
                   jax.ShapeDtypeStruct((npair, LANES, LANES), F32)],
        scratch_shapes=scratch,
        compiler_params=_params(("arbitrary",)),
        name="rw_rec_prompt",
    )(*seqs, *consts)


def kernel(x_prompt, x_sample, state_ret, state_hgrn, state_wkv, state_shift, c_prompt, c_sample, mod_w, mod_b, norm_mix_g, norm_mlp_g, final_g, mlp_w1, mlp_w2, ab_w_in, ab_w_out, hg_lb, hg_norm_g, rw_mu, rw_w_rkv, rw_w0, rw_w1, rw_w2, rw_a0, rw_a1, rw_a2, rw_v0, rw_v1, rw_v2, rw_g1, rw_g2, rw_k_k, rw_k_a, rw_r_k, rw_ln_g, rw_ln_b, rw_w_out):
    bp, tp, d = x_prompt.shape
    bs, ts, _ = x_sample.shape
    depth = mod_w.shape[0]
    n_ab = ab_w_in.shape[0]
    n_c = rw_w_rkv.shape[0]
    assert bp == 1 and d == 1024 and tp % TM == 0 and tp % CHUNK == 0
    assert (bs * ts) % TM == 0 and TM % ts == 0 and ts % SUB == 0 and ts <= CHUNK
    np_rows, ns_rows = bp * tp, bs * ts
    n = np_rows + ns_rows
    ng = TM // ts
    assert bs % ng == 0
    assert np_rows % TM_RW == 0 and ns_rows % TM_RW == 0 and TM_RW % TM == 0
    assert np_rows % TM_IN == 0 and ns_rows % TM_IN == 0 and TM_IN % TM == 0 and bs % (TM_IN // ts) == 0
    ng_max = TM_IN // ts
    cfg = dict(ng=ng, npt=np_rows // TM, ts=ts, n=n, mod_off=ng_max // ng)
    dk, dv = d // 16, d // 8
    npair = d // LANES
    nh = d // RW_N

    c_all = jnp.concatenate([c_prompt, c_sample], axis=0)
    pad = (-c_all.shape[0]) % 8
    c_all = jnp.pad(c_all, ((0, pad), (0, 0)))
    mod = _modulation(c_all, mod_w, mod_b)
    modt = jnp.concatenate([jnp.broadcast_to(mod[:, 0:1], (depth, ng_max, 6 * d)), mod[:, bp:bp + bs]], axis=1)
    modt = modt.reshape(depth, ng_max + bs, 1, 6 * d)

    x = (x_prompt.reshape(np_rows, d), x_sample.reshape(ns_rows, d))

    pos = jnp.concatenate([jnp.arange(tp, dtype=F32), jnp.tile(PAST_LEN + jnp.arange(ts, dtype=F32), bs)])
    half = dk // 2
    inv = ROPE_BASE ** (-jnp.arange(half, dtype=F32) / half)
    ang = pos[:, None] * inv[None, :]
    cos_t = jnp.tile(jnp.cos(ang), (1, 4))
    sin_t = jnp.tile(jnp.concatenate([-jnp.sin(ang), jnp.sin(ang)], axis=1), (1, 2))

    lb_all = jnp.cumsum(jax.nn.softmax(hg_lb.astype(F32), axis=0), axis=0)
    lb_all = lb_all - lb_all[:1]
    lbc = jnp.stack([jnp.log(lb_all), jnp.log1p(-lb_all), 1.0 - lb_all], axis=1)
    lbc = jnp.pad(lbc, ((0, 0), (0, 5), (0, 0))).reshape(n_ab * 8, d // 2)

    bf = lambda a: a.astype(BF16)
    ab_w_in_b, ab_w_out_b = bf(ab_w_in), bf(ab_w_out)
    mlp_w1_b, mlp_w2_b = bf(mlp_w1), bf(mlp_w2)
    rw_w_out_b = bf(rw_w_out)
    vec = lambda a: a.reshape(a.shape[0], 1, d)
    W = dict(rw_mu=rw_mu, rw_w_rkv=bf(rw_w_rkv), rw_w0=vec(rw_w0), rw_w1=bf(rw_w1), rw_w2=bf(rw_w2),
             rw_a0=vec(rw_a0), rw_a1=bf(rw_a1), rw_a2=bf(rw_a2), rw_g1=bf(rw_g1), rw_g2=bf(rw_g2),
             rw_k_k=vec(rw_k_k), rw_k_a=vec(rw_k_a), rw_r_k=vec(rw_r_k), rw_v0=vec(rw_v0), rw_v1=bf(rw_v1),
             rw_v2=bf(rw_v2))
    seg = jnp.asarray(np.pad(np.kron(np.eye(nh), np.ones((RW_N, 1))), ((0, 0), (0, LANES - nh))), BF16)
    g_mix = norm_mix_g.reshape(depth, 1, d)
    g_mlp = norm_mlp_g.reshape(depth, 1, d)
    fin_g = final_g.reshape(1, d)

    ret_p, ret_s, hg_p, hg_s, wkv_p, wkv_s, sh_p, sh_s = ([] for _ in range(8))
    v_first = None
    for layer in range(depth):
        m = layer // 2
        final = layer == depth - 1
        if layer % 2 == 0:
            z = _ab_in(x, modt, g_mix, ab_w_in_b, cos_t, sin_t, lbc, layer, m, cfg)
            o_p, r_p, h_p = _ab_rec(z, hg_norm_g[m], None, None, cfg, False, d)
            o_s, r_s, h_s = _ab_rec(z, hg_norm_g[m], state_ret[m], jnp.swapaxes(state_hgrn[m], -1, -2), cfg, True, d)
            ret_p.append(r_p[None])
            hg_p.append(jnp.swapaxes(h_p, -1, -2)[None])
            ret_s.append(r_s)
            hg_s.append(jnp.swapaxes(h_s, -1, -2))
            x = _post(x, (o_p, o_s), modt, ab_w_out_b, g_mlp, mlp_w1_b, mlp_w2_b, fin_g, layer, m, cfg, False, final)
        else:
            shift_rows = state_shift[m].reshape(bs, 1, d)
            outs = _rw_in(x, modt, g_mix, shift_rows, W, seg, v_first, layer, m, cfg)
            seqs, gate, bonus, hlast = outs[0:6], outs[6], outs[7], outs[8]
            if v_first is None:
                v_first = outs[9]
            y_p, s_p = _rw_rec(seqs, None, cfg, False)
            sw = state_wkv[m].reshape(bs, npair, 2, RW_N, RW_N)
            zero = jnp.zeros_like(sw[:, :, 0])
            s_bd = jnp.concatenate([jnp.concatenate([sw[:, :, 0], zero], axis=-1),
                                    jnp.concatenate([zero, sw[:, :, 1]], axis=-1)], axis=-2)
            y_s, s_s = _rw_rec(seqs, jnp.swapaxes(s_bd, 0, 1), cfg, True)
            unbd = lambda a: jnp.stack([a[..., :RW_N, :RW_N], a[..., RW_N:, RW_N:]], axis=-3)
            wkv_p.append(unbd(s_p).reshape(1, 1, nh, RW_N, RW_N))
            wkv_s.append(unbd(jnp.swapaxes(s_s, 0, 1)).reshape(bs, nh, RW_N, RW_N))
            sh_p.append(hlast[cfg["npt"] * ng - 1][None, None])
            sh_s.append(hlast[cfg["npt"] * ng:][None])
            x = _post(x, (y_p, y_s, bonus, gate, vec(rw_ln_g), vec(rw_ln_b), seg), modt, rw_w_out_b, g_mlp,
                      mlp_w1_b, mlp_w2_b, fin_g, layer, m, cfg, True, final)

    out_p, out_s = x
    return (out_p.reshape(bp, tp, d), out_s.reshape(bs, ts, d),
            jnp.stack(ret_p), jnp.stack(ret_s), jnp.stack(hg_p), jnp.stack(hg_s),
            jnp.concatenate(wkv_p, axis=0), jnp.stack(wkv_s),
            jnp.concatenate(sh_p, axis=0), jnp.concatenate(sh_s, axis=0))
```

```python
import functools

import numpy as np
import jax
import jax.numpy as jnp
from jax import lax
from jax.experimental import pallas as pl
from jax.experimental.pallas import tpu as pltpu

F32 = jnp.float32
BF16 = jnp.bfloat16

CHUNK = 64
PAST_LEN = 2048
ROPE_BASE = 10000.0
RET_HEADS = 4
HG_HEADS = 4
RW_N = 64
NORM_EPS = 1e-6
RW_LN_EPS = 64e-5

LANES = 128
SUB = 16
RW_GROUP = 2
RW_WAVE = 2
RW_GROUP_SAMPLE = 4
TM_RW = 256
ROW_BLOCK = 256
TM_IN = 512
VMEM_LIMIT_IN = 60 * 1024 * 1024
TM = 256
VMEM_LIMIT = 56 * 1024 * 1024


def _sigmoid(x):
    return 1.0 / (1.0 + jnp.exp(-x))


def _split2(x):
    hi = x.astype(BF16)
    lo = (x - hi.astype(F32)).astype(BF16)
    return hi, lo


def _split3(x):
    a = x.astype(BF16)
    r = x - a.astype(F32)
    b = r.astype(BF16)
    c = (r - b.astype(F32)).astype(BF16)
    return a, b, c


_NN = (((1,), (0,)), ((), ()))
_NT = (((1,), (1,)), ((), ()))
_TN = (((0,), (0,)), ((), ()))


def _dg(a, b, dims):
    return lax.dot_general(a, b, dims, preferred_element_type=F32)


def _dot1(a, b, dims=_NN):
    return _dg(a.astype(BF16), b.astype(BF16), dims)


def _dot3(a, b, dims=_NN):
    ah, al = _split2(a)
    bh, bl = _split2(b)
    return _dg(ah, bh, dims) + _dg(ah, bl, dims) + _dg(al, bh, dims)


def _dotp(a, b, dims, passes):
    return _dot1(a, b, dims) if passes == 1 else _dot3(a, b, dims)


P_BIG = 1
P_ST = 1


def _cumsum_rows(tri_bf16, x):
    a, b, c = _split3(x)
    return _dg(tri_bf16, a, _NN) + _dg(tri_bf16, b, _NN) + _dg(tri_bf16, c, _NN)


def _segsum(x, seg_bf16):
    per_head = _dg(x.astype(BF16), seg_bf16, _NN)
    return _dg(per_head.astype(BF16), seg_bf16, _NT)


def _rms(x, g):
    ms = jnp.mean(x * x, axis=-1, keepdims=True)
    return x * lax.rsqrt(ms + NORM_EPS) * g


def _modulate(y, sc, sh):
    rows, d = y.shape
    ng = sc.shape[0]
    y3 = y.reshape(ng, rows // ng, d)
    return (y3 * (1.0 + sc) + sh).reshape(rows, d)


def _gate_add(x, gt, out):
    rows, d = x.shape
    ng = gt.shape[0]
    return (x.reshape(ng, rows // ng, d) + gt * out.reshape(ng, rows // ng, d)).reshape(rows, d)


def _rows_specs(x, npt):
    if isinstance(x, tuple):
        d = x[0].shape[1]
        return list(x), [pl.BlockSpec((TM, d), lambda i: (jnp.minimum(i, npt - 1), 0)),
                         pl.BlockSpec((TM, d), lambda i: (jnp.maximum(i - npt, 0), 0))]
    return [x], [pl.BlockSpec((TM, x.shape[1]), lambda i: (i, 0))]


def _rows_value(refs, npt):
    if len(refs) == 1:
        return refs[0][...]
    p_ref, s_ref = refs
    rows, d = p_ref.shape
    s_i = jnp.where(pl.program_id(0) >= npt, 1, 0)
    is_sample = (lax.broadcasted_iota(jnp.int32, (rows, LANES), 0) * 0 + s_i) == 1
    return jnp.concatenate([jnp.where(is_sample, s_ref[:, LANES * p:LANES * (p + 1)],
                                      p_ref[:, LANES * p:LANES * (p + 1)]) for p in range(d // LANES)], axis=-1)


def _const_spec(block, index):
    return pl.BlockSpec(block, lambda *_: index, pipeline_mode=pl.Buffered(1))


def _params(sem):
    return pltpu.CompilerParams(dimension_semantics=sem, vmem_limit_bytes=VMEM_LIMIT)


def _mod_kernel(c_ref, w_ref, b_ref, o_ref):
    c = c_ref[...]
    s = c * _sigmoid(c)
    o_ref[0] = _dot1(s, w_ref[0]) + b_ref[0]


def _modulation(c_all, mod_w, mod_b):
    depth, d, d6 = mod_w.shape
    rows = c_all.shape[0]
    nt = d6 // d
    return pl.pallas_call(
        _mod_kernel,
        grid=(depth, nt),
        in_specs=[
            pl.BlockSpec((rows, d), lambda l, j: (0, 0)),
            pl.BlockSpec((1, d, d), lambda l, j: (l, 0, j)),
            pl.BlockSpec((1, 1, d), lambda l, j: (l, 0, j)),
        ],
        out_specs=pl.BlockSpec((1, rows, d), lambda l, j: (l, 0, j)),
        out_shape=jax.ShapeDtypeStruct((depth, rows, d6), F32),
        compiler_params=_params(("arbitrary", "arbitrary")),
        name="modulation",
    )(c_all, mod_w, mod_b.reshape(depth, 1, d6))


def _ab_in_kernel(*refs, d, npt, dk_scale):
    mod_ref, g_ref, w_ref, cos_ref, sin_ref, lb_ref, o_ref = refs[-7:]
    x = _rows_value(refs[:-7], npt)
    h = _modulate(_rms(x, g_ref[0]), mod_ref[0, :, :, d:2 * d], mod_ref[0, :, :, 0:d])
    hb = h.astype(BF16)
    hw = d // 2

    def proj(a):
        return _dg(hb, w_ref[0, :, a:a + hw], _NN)

    z = proj(4 * hw)
    log_lb = lb_ref[0:1, :]
    log1m_lb = lb_ref[1:2, :]
    ls = jnp.minimum(z, 0.0) - jnp.log1p(jnp.exp(-jnp.abs(z)))
    c = log1m_lb + ls
    m = jnp.maximum(log_lb, c)
    o_ref[:, 4 * hw:5 * hw] = m + jnp.log1p(jnp.exp(-jnp.abs(log_lb - c)))
    o_ref[:, 5 * hw:6 * hw] = c - z
    qk = proj(0)
    lane = lax.broadcasted_iota(jnp.int32, (x.shape[0], LANES), 1)
    first_half = (lane % 64) < 32
    cos = cos_ref[...]
    sin = sin_ref[...]
    for s in range(hw // LANES):
        xs = qk[:, LANES * s:LANES * (s + 1)]
        sw = jnp.where(first_half, pltpu.roll(xs, 96, 1), pltpu.roll(xs, 32, 1))
        rot = xs * cos + sw * sin
        if s < hw // (2 * LANES):
            rot = rot * dk_scale
        o_ref[:, LANES * s:LANES * (s + 1)] = rot
    o_ref[:, hw:2 * hw] = proj(hw)
    z = proj(2 * hw)
    o_ref[:, 2 * hw:3 * hw] = z * _sigmoid(z)
    z = proj(3 * hw)
    o_ref[:, 3 * hw:4 * hw] = z * _sigmoid(z)
    o_ref[:, 6 * hw:7 * hw] = proj(5 * hw)
    o_ref[:, 7 * hw:8 * hw] = _sigmoid(proj(6 * hw))


def _ab_in(x, modt, g, w_in, cos_t, sin_t, lbc, layer, m, cfg):
    ng, npt, n = cfg["ng"], cfg["npt"], cfg["n"]
    x_args, x_specs = _rows_specs(x, npt)
    d = x_args[0].shape[1]
    nt = n // TM
    zw = 4 * d
    kern = functools.partial(_ab_in_kernel, d=d, npt=npt, dk_scale=float((d // 16) ** -0.5))
    return pl.pallas_call(
        kern,
        grid=(nt,),
        in_specs=x_specs + [
            pl.BlockSpec((1, ng, 1, 6 * d), lambda i: (layer, jnp.maximum(i - npt + cfg["mod_off"], 0), 0, 0)),
            _const_spec((1, 1, d), (layer, 0, 0)),
            _const_spec((1, d, w_in.shape[2]), (m, 0, 0)),
            pl.BlockSpec((TM, LANES), lambda i: (i, 0)),
            pl.BlockSpec((TM, LANES), lambda i: (i, 0)),
            _const_spec((8, d // 2), (m, 0)),
        ],
        out_specs=pl.BlockSpec((TM, zw), lambda i: (i, 0)),
        out_shape=jax.ShapeDtypeStruct((n, zw), F32),
        compiler_params=_params(("arbitrary",)),
        name="ab_in",
    )(*x_args, modt, g, w_in, cos_t, sin_t, lbc)


def _ab_rec_kernel(*refs, d, c, sample):
    if sample:
        (z_ref, dmat_ref, rdec_ref, cdec_ref, gc_ref, tri_ref, hgg_ref, ones_ref, sret_in, shg_in,
         o_ref, sret_out, shg_out, sret_scr, shg_scr) = refs
    else:
        (z_ref, dmat_ref, rdec_ref, cdec_ref, gc_ref, tri_ref, hgg_ref, ones_ref,
         o_ref, sret_out, shg_out, sret_scr, shg_scr) = refs
    hw = d // 2
    rows = z_ref.shape[0]
    n_chunks = rows // c
    dk = d // 16

    if not sample:
        @pl.when(pl.program_id(0) == 0)
        def _():
            sret_scr[...] = jnp.zeros_like(sret_scr)
            shg_scr[...] = jnp.zeros_like(shg_scr)

    lane_c = lax.broadcasted_iota(jnp.int32, (c, LANES), 1)
    row_h = lax.broadcasted_iota(jnp.int32, (SUB // 2, LANES), 0)
    ones = ones_ref[...]
    tri = tri_ref[...]

    def hg_chunk(r0, b):
        col = lambda g, h: slice(g * hw + LANES * h, g * hw + LANES * (h + 1))
        units = [(i, h) for i in range(c // SUB) for h in range(HG_HEADS)]
        rs = {i: pl.ds(r0 + SUB * i, SUB) for i in range(c // SUB)}
        q = {u: z_ref[rs[u[0]], col(3, u[1])] for u in units}
        v = {u: z_ref[rs[u[0]], col(6, u[1])] for u in units}
        bs = {(i, h): b[SUB * i:SUB * (i + 1), LANES * h:LANES * (h + 1)] for (i, h) in units}
        bl = {u: bs[u][SUB - 1:SUB, :] for u in units}
        w = {u: bs[u] - z_ref[rs[u[0]], col(5, u[1])] for u in units}
        hs = SUB // 2
        r = {}
        for u in units:
            q_lo, q_hi = q[u][0:hs], q[u][hs:SUB]
            b_lo, b_hi = bs[u][0:hs], bs[u][hs:SUB]
            p_lo, p_hi = [], []
            for j in range(SUB):
                wj = w[u][j:j + 1, :]
                if j < hs:
                    p_lo.append(jnp.where(row_h >= j, q_lo * jnp.exp(b_lo - wj), 0.0))
                    p_hi.append(q_hi * jnp.exp(b_hi - wj))
                else:
                    p_hi.append(jnp.where(row_h >= j - hs, q_hi * jnp.exp(b_hi - wj), 0.0))
            r[u] = _dot1(jnp.concatenate(p_lo + p_hi, axis=0), ones)
        upd = {u: _dot1(v[u], jnp.exp(bl[u] - w[u]), _TN) for u in units}
        inner = {}
        for u in units:
            acc_lo = r[u][0:hs] * v[u][0:1, :]
            for j in range(1, hs):
                acc_lo = acc_lo + r[u][hs * j:hs * (j + 1)] * v[u][j:j + 1, :]
            base = hs * hs
            acc_hi = r[u][base:base + hs] * v[u][0:1, :]
            for j in range(1, SUB):
                acc_hi = acc_hi + r[u][base + hs * j:base + hs * (j + 1)] * v[u][j:j + 1, :]
            inner[u] = jnp.concatenate([acc_lo, acc_hi], axis=0)
        st = {h: shg_scr[h] for h in range(HG_HEADS)}
        for u in units:
            i, h = u
            out = inner[u] + _dot1(q[u] * jnp.exp(bs[u]), st[h], _NT)
            st[h] = st[h] * jnp.exp(bl[u]) + upd[u]
            ms = jnp.mean(out * out, axis=-1, keepdims=True)
            o_ref[rs[i], col(1, h)] = out * lax.rsqrt(ms + NORM_EPS) * hgg_ref[...] * z_ref[rs[i], col(7, h)]
        for h in range(HG_HEADS):
            shg_scr[h] = st[h]

    def ret_chunk(r0):
        rs = pl.ds(r0, c)
        heads = range(RET_HEADS)
        zeros = jnp.zeros((dk, LANES), F32)
        qp = [z_ref[rs, LANES * p:LANES * (p + 1)] for p in range(RET_HEADS // 2)]
        kp = [z_ref[rs, hw // 2 + LANES * p:hw // 2 + LANES * (p + 1)] for p in range(RET_HEADS // 2)]
        v = [z_ref[rs, hw + LANES * h:hw + LANES * (h + 1)] for h in heads]
        s = [sret_scr[h] for h in heads]
        qh = [jnp.where((lane_c // dk) == (h % 2), qp[h // 2], 0.0) for h in heads]
        scores = [_dot1(qh[h], kp[h // 2], _NT) * dmat_ref[h] for h in heads]
        spad = [jnp.concatenate([s[h], zeros] if h % 2 == 0 else [zeros, s[h]], axis=0) for h in heads]
        cross = [_dot1(qp[h // 2], spad[h]) * rdec_ref[h] for h in heads]
        u = [_dot1(kp[h // 2] * cdec_ref[h], v[h], _TN) for h in heads]
        inner = [_dot1(scores[h], v[h]) for h in heads]
        for h in heads:
            e = h % 2
            out = inner[h] + cross[h]
            sret_scr[h] = s[h] * gc_ref[h] + u[h][dk * e:dk * (e + 1)]
            ms = jnp.mean(out * out, axis=-1, keepdims=True)
            o_ref[rs, LANES * h:LANES * (h + 1)] = (out * lax.rsqrt(ms + NORM_EPS)
                                                    * z_ref[rs, 2 * hw + LANES * h:2 * hw + LANES * (h + 1)])

    def chunk_body(ci, carry):
        r0 = pl.multiple_of(ci * c, c)
        if sample:
            sret_scr[...] = sret_in[ci]
            shg_scr[...] = shg_in[ci]
        b = _cumsum_rows(tri, z_ref[pl.ds(r0, c), 4 * hw:5 * hw])
        ret_chunk(r0)
        hg_chunk(r0, b)
        if sample:
            sret_out[ci] = sret_scr[...]
            shg_out[ci] = shg_scr[...]
        return carry

    lax.fori_loop(0, n_chunks, chunk_body, 0, unroll=4)

    if not sample:
        @pl.when(pl.program_id(0) == pl.num_programs(0) - 1)
        def _():
            sret_out[...] = sret_scr[...]
            shg_out[...] = shg_scr[...]


def _ret_consts(c):
    heads = jnp.arange(RET_HEADS, dtype=F32)
    log_g = jnp.log1p(-jnp.exp2(-5.0 - heads))
    idx = jnp.arange(c, dtype=F32)
    rel = idx[:, None] - idx[None, :]
    dmat = jnp.exp(jnp.where(rel[None] >= 0, rel[None] * log_g[:, None, None], -jnp.inf))
    rdec = jnp.exp((idx[None, :] + 1.0) * log_g[:, None])
    cdec = jnp.exp((c - 1.0 - idx)[None, :] * log_g[:, None])
    gc = jnp.exp(c * log_g)
    bc = lambda a: jnp.broadcast_to(a[:, :, None], (RET_HEADS, c, LANES))
    return dmat, bc(rdec), bc(cdec), gc


def _ab_rec(z, hg_g, sret, shg_t, cfg, sample, d):
    n = z.shape[0]
    ng, npt = cfg["ng"], cfg["npt"]
    c = cfg["ts"] if sample else CHUNK
    dk, dv = d // 16, d // 8
    dmat, rdec, cdec, gc = _ret_consts(c)
    gcb = jnp.broadcast_to(gc[:, None, None], (RET_HEADS, dk, LANES))
    tri = np.kron(np.eye(c // SUB), np.tril(np.ones((SUB, SUB))))
    tri = jnp.asarray(tri, BF16)
    ones = jnp.ones((LANES, LANES), BF16)
    kern = functools.partial(_ab_rec_kernel, d=d, c=c, sample=sample)
    consts = [dmat, rdec, cdec, gcb, tri, hg_g.reshape(1, dv), ones]
    const_specs = [
        _const_spec(dmat.shape, (0, 0, 0)), _const_spec(rdec.shape, (0, 0, 0)),
        _const_spec(cdec.shape, (0, 0, 0)), _const_spec(gcb.shape, (0, 0, 0)),
        _const_spec(tri.shape, (0, 0)), _const_spec((1, dv), (0, 0)), _const_spec(ones.shape, (0, 0)),
    ]
    scratch = [pltpu.VMEM((RET_HEADS, dk, dv), F32), pltpu.VMEM((HG_HEADS, dv, dv), F32)]
    if sample:
        nb = sret.shape[0]
        nst = nb // ng
        return pl.pallas_call(
            kern,
            grid=(nst,),
            in_specs=[pl.BlockSpec((TM, z.shape[1]), lambda i: (npt + i, 0))] + const_specs + [
                pl.BlockSpec((ng, RET_HEADS, dk, dv), lambda i: (i, 0, 0, 0)),
                pl.BlockSpec((ng, HG_HEADS, dv, dv), lambda i: (i, 0, 0, 0)),
            ],
            out_specs=[
                pl.BlockSpec((TM, d), lambda i: (i, 0)),
                pl.BlockSpec((ng, RET_HEADS, dk, dv), lambda i: (i, 0, 0, 0)),
                pl.BlockSpec((ng, HG_HEADS, dv, dv), lambda i: (i, 0, 0, 0)),
            ],
            out_shape=[
                jax.ShapeDtypeStruct((nst * TM, d), F32),
                jax.ShapeDtypeStruct((nb, RET_HEADS, dk, dv), F32),
                jax.ShapeDtypeStruct((nb, HG_HEADS, dv, dv), F32),
            ],
            scratch_shapes=scratch,
            compiler_params=_params(("arbitrary",)),
            name="ab_rec_sample",
        )(z, *consts, sret, shg_t)
    return pl.pallas_call(
        kern,
        grid=(npt,),
        in_specs=[pl.BlockSpec((TM, z.shape[1]), lambda i: (i, 0))] + const_specs,
        out_specs=[
            pl.BlockSpec((TM, d), lambda i: (i, 0)),
            pl.BlockSpec((RET_HEADS, dk, dv), lambda i: (0, 0, 0)),
            pl.BlockSpec((HG_HEADS, dv, dv), lambda i: (0, 0, 0)),
        ],
        out_shape=[
            jax.ShapeDtypeStruct((npt * TM, d), F32),
            jax.ShapeDtypeStruct((RET_HEADS, dk, dv), F32),
            jax.ShapeDtypeStruct((HG_HEADS, dv, dv), F32),
        ],
        scratch_shapes=scratch,
        compiler_params=_params(("arbitrary",)),
        name="ab_rec_prompt",
    )(z, *consts)


def _post_kernel(*refs, d, npt, nx, rwkv, final):
    x = _rows_value(refs[:nx], npt)
    it = iter(refs[nx:])
    if rwkv:
        yp_ref, ys_ref, bonus_ref, gate_ref, lng_ref, lnb_ref, seg_ref = (next(it) for _ in range(7))
    else:
        op_ref, os_ref = next(it), next(it)
    mod_ref, wout_ref, g2_ref, w1_ref, w2_ref = (next(it) for _ in range(5))
    if final:
        fg_ref = next(it)
        outp_ref, outs_ref = next(it), next(it)
    else:
        out_ref = next(it)

    i = pl.program_id(0)
    if rwkv:
        s_i = jnp.where(i >= npt, 1, 0)
        is_sample = (lax.broadcasted_iota(jnp.int32, (x.shape[0], LANES), 0) * 0 + s_i) == 1
        y = jnp.concatenate([jnp.where(is_sample, ys_ref[p], yp_ref[p]) for p in range(yp_ref.shape[0])],
                            axis=-1)
        seg = seg_ref[...]
        inv_n = 1.0 / RW_N
        mean = _segsum(y, seg) * inv_n
        yc = y - mean
        var = _segsum(yc * yc, seg) * inv_n
        o = yc * lax.rsqrt(var + RW_LN_EPS) * lng_ref[0] + lnb_ref[0]
        o = (o + bonus_ref[...]) * gate_ref[...]
    else:
        o = _rows_value((op_ref, os_ref), npt)
    gt1 = mod_ref[0, :, :, 2 * d:3 * d]
    sh2 = mod_ref[0, :, :, 3 * d:4 * d]
    sc2 = mod_ref[0, :, :, 4 * d:5 * d]
    gt2 = mod_ref[0, :, :, 5 * d:6 * d]
    x1 = _gate_add(x, gt1, _dg(o.astype(BF16), wout_ref[0], _NN))
    hb = _modulate(_rms(x1, g2_ref[0]), sc2, sh2).astype(BF16)
    ff = w1_ref.shape[2]
    acc = jnp.zeros_like(x1)
    for j in range(ff // d):
        u = jnp.maximum(_dg(hb, w1_ref[0, :, j * d:(j + 1) * d], _NN), 0.0)
        acc = acc + _dg((u * u).astype(BF16), w2_ref[0, j * d:(j + 1) * d, :], _NN)
    x2 = _gate_add(x1, gt2, acc)
    if final:
        x2 = _rms(x2, fg_ref[...])

        @pl.when(i < npt)
        def _():
            outp_ref[...] = x2

        @pl.when(i >= npt)
        def _():
            outs_ref[...] = x2
    else:
        out_ref[...] = x2


def _post(x, mixer_in, modt, wout, g2, w1, w2, final_g, layer, m, cfg, rwkv, final):
    ng, npt, n = cfg["ng"], cfg["npt"], cfg["n"]
    args, specs = _rows_specs(x, npt)
    d = args[0].shape[1]
    nt = n // TM
    kern = functools.partial(_post_kernel, d=d, npt=npt, nx=len(args), rwkv=rwkv, final=final)
    tok = pl.BlockSpec((TM, d), lambda i: (i, 0))
    tok_p = pl.BlockSpec((TM, d), lambda i: (jnp.minimum(i, npt - 1), 0))
    tok_s = pl.BlockSpec((TM, d), lambda i: (jnp.maximum(i - npt, 0), 0))
    if rwkv:
        y_p, y_s, bonus, gate, lng, lnb, seg = mixer_in
        args += [y_p, y_s, bonus, gate, lng, lnb, seg]
        specs += [pl.BlockSpec((y_p.shape[0], TM, LANES), lambda i: (0, jnp.minimum(i, npt - 1), 0)),
                  pl.BlockSpec((y_s.shape[0], TM, LANES), lambda i: (0, jnp.maximum(i - npt, 0), 0)), tok, tok,
                  _const_spec((1, 1, d), (m, 0, 0)), _const_spec((1, 1, d), (m, 0, 0)),
                  _const_spec((d, LANES), (0, 0))]
    else:
        args += list(mixer_in)
        specs += [tok_p, tok_s]
    args += [modt, wout, g2, w1, w2]
    specs += [
        pl.BlockSpec((1, ng, 1, 6 * d), lambda i: (layer, jnp.maximum(i - npt + cfg["mod_off"], 0), 0, 0)),
        _const_spec((1, d, d), (m, 0, 0)),
        _const_spec((1, 1, d), (layer, 0, 0)),
        _const_spec((1, d, w1.shape[2]), (layer, 0, 0)),
        _const_spec((1, w2.shape[1], d), (layer, 0, 0)),
    ]
    if final:
        args += [final_g]
        specs += [_const_spec((1, d), (0, 0))]
    if final:
        out_specs = [tok_p, tok_s]
        out_shape = [jax.ShapeDtypeStruct((npt * TM, d), F32), jax.ShapeDtypeStruct((n - npt * TM, d), F32)]
    else:
        out_specs = tok
        out_shape = jax.ShapeDtypeStruct((n, d), F32)
    return pl.pallas_call(
        kern,
        grid=(nt,),
        in_specs=specs,
        out_specs=out_specs,
        out_shape=out_shape,
        compiler_params=_params(("arbitrary",)),
        name="post_rwkv" if rwkv else "post_ab",
    )(*args)


def _rw_in_kernel(*refs, d, ts, npt, vres):
    it = iter(refs)
    x_ref, mod_ref, g_ref, shift_ref, mu_ref, wrkv_ref = (next(it) for _ in range(6))
    w0_ref, w1_ref, w2_ref, a0_ref, a1_ref, a2_ref, g1_ref, g2_ref = (next(it) for _ in range(8))
    kk_ref, ka_ref, rk_ref, seg_ref = (next(it) for _ in range(4))
    if vres:
        vf_ref, v0_ref, v1_ref, v2_ref = (next(it) for _ in range(4))
    r_out, ld_out, k_out, v_out, kk_out, kka_out, gate_out, bonus_out, hlast_out = (next(it) for _ in range(9))
    if not vres:
        vtok_out = next(it)
    h_scr, carry_scr = next(it), next(it)

    i = pl.program_id(0)
    rows = x_ref.shape[0]
    ng = rows // ts

    @pl.when(i == 0)
    def _():
        carry_scr[...] = jnp.zeros_like(carry_scr)

    x = x_ref[...]
    h = _modulate(_rms(x, g_ref[0]), mod_ref[0, :, :, d:2 * d], mod_ref[0, :, :, 0:d])
    for p in range(d // LANES):
        h_scr[p] = h[:, LANES * p:LANES * (p + 1)]
        hlast_out[:, LANES * p:LANES * (p + 1)] = h_scr[p, pl.ds(ts - 1, ng, stride=ts), :]
    rolled = pltpu.roll(h, 1, 0)
    row = lax.broadcasted_iota(jnp.int32, (rows, d), 0)
    s_i = jnp.where(i >= npt, 1, 0)
    first = ((row % ts) == 0) & ((row * (1 - s_i)) == 0)
    seq_prev = jnp.broadcast_to(shift_ref[...], (ng, ts, d)).reshape(rows, d)
    prev_first = jnp.where((row * 0 + s_i) == 1, seq_prev, jnp.broadcast_to(carry_scr[0:1, :], (rows, d)))
    prev = jnp.where(first, prev_first, rolled)
    carry_scr[0:1, :] = h[rows - 1:rows, :]

    seg = seg_ref[...]

    def block(b0):
        rb = slice(b0, b0 + ROW_BLOCK)
        hb = h[rb]
        xx = prev[rb] - hb
        mix = lambda j: (hb + xx * mu_ref[0, j:j + 1, :]).astype(BF16)
        xr, xw, xk, xv, xa, xg = (mix(j) for j in range(6))
        yield
        w_dn = _dg(xw, w1_ref[0], _NN)
        a_dn = _dg(xa, a1_ref[0], _NN)
        g_dn = _dg(xg, g1_ref[0], _NN)
        if vres:
            v_dn = _dg(xv, v1_ref[0], _NN)
        r = _dg(xr, wrkv_ref[0, 0], _NN)
        yield
        k = _dg(xk, wrkv_ref[0, 1], _NN)
        yield
        v = _dg(xv, wrkv_ref[0, 2], _NN)
        yield
        kk = k * kk_ref[0]
        kk_ss = _segsum(kk * kk, seg)
        wl = w0_ref[0] + _dot1(jnp.tanh(w_dn), w2_ref[0])
        a = _sigmoid(a0_ref[0] + _dot1(a_dn, a2_ref[0]))
        gate_out[rb, :] = _dot1(_sigmoid(g_dn), g2_ref[0])
        if vres:
            lv = v0_ref[0] + _dot1(v_dn, v2_ref[0])
            v = v + (vf_ref[rb, :] - v) * _sigmoid(lv)
        else:
            vtok_out[rb, :] = v
        yield
        nwl = -wl
        w = -(jnp.maximum(nwl, 0.0) + jnp.log1p(jnp.exp(-jnp.abs(nwl)))) - 0.5
        ld = -jnp.exp(w)
        kk = kk / jnp.maximum(jnp.sqrt(kk_ss), 1e-12)
        km = k * (1.0 + (a - 1.0) * ka_ref[0])
        bonus_out[rb, :] = _segsum(r * km * rk_ref[0], seg) * v
        kka = kk * a
        for p in range(d // LANES):
            cs = slice(LANES * p, LANES * (p + 1))
            r_out[p, rb, :] = r[:, cs]
            ld_out[p, rb, :] = ld[:, cs]
            k_out[p, rb, :] = km[:, cs]
            v_out[p, rb, :] = v[:, cs]
            kk_out[p, rb, :] = kk[:, cs]
            kka_out[p, rb, :] = kka[:, cs]
        yield

    live = []
    starts = list(range(0, rows, ROW_BLOCK))
    while starts or live:
        if starts:
            live.append(block(starts.pop(0)))
        for g in list(live):
            try:
                next(g)
            except StopIteration:
                live.remove(g)


def _rw_in(x, modt, g, shift_rows, W, seg, v_first, layer, m, cfg):
    n, d = x.shape
    ts, tm = cfg["ts"], TM_IN
    ng, npt = tm // ts, cfg["npt"] * TM // tm
    nt = n // tm
    npair = d // LANES
    vres = v_first is not None
    kern = functools.partial(_rw_in_kernel, d=d, ts=ts, npt=npt, vres=vres)
    tok = pl.BlockSpec((tm, d), lambda i: (i, 0))
    vec = lambda: _const_spec((1, 1, d), (m, 0, 0))
    lora = lambda a: _const_spec((1,) + a.shape[1:], (m, 0, 0))
    args = [x, modt, g, shift_rows, W["rw_mu"], W["rw_w_rkv"],
            W["rw_w0"], W["rw_w1"], W["rw_w2"], W["rw_a0"], W["rw_a1"], W["rw_a2"], W["rw_g1"], W["rw_g2"],
            W["rw_k_k"], W["rw_k_a"], W["rw_r_k"], seg]
    specs = [
        tok,
        pl.BlockSpec((1, ng, 1, 2 * d), lambda i: (layer, jnp.maximum(i - npt + 1, 0), 0, 0)),
        _const_spec((1, 1, d), (layer, 0, 0)),
        pl.BlockSpec((ng, 1, d), lambda i: (jnp.maximum(i - npt, 0), 0, 0)),
        _const_spec((1, 6, d), (m, 0, 0)),
        _const_spec((1, 3, d, d), (m, 0, 0, 0)),
        vec(), lora(W["rw_w1"]), lora(W["rw_w2"]), vec(), lora(W["rw_a1"]), lora(W["rw_a2"]),
        lora(W["rw_g1"]), lora(W["rw_g2"]), vec(), vec(), vec(), _const_spec((d, LANES), (0, 0)),
    ]
    if vres:
        args += [v_first, W["rw_v0"], W["rw_v1"], W["rw_v2"]]
        specs += [tok, _const_spec((1, 1, d), (m - 1, 0, 0)),
                  _const_spec((1,) + W["rw_v1"].shape[1:], (m - 1, 0, 0)),
                  _const_spec((1,) + W["rw_v2"].shape[1:], (m - 1, 0, 0))]
    pm = pl.BlockSpec((npair, tm, LANES), lambda i: (0, i, 0))
    pm_shape = jax.ShapeDtypeStruct((npair, n, LANES), F32)
    tok_shape = jax.ShapeDtypeStruct((n, d), F32)
    out_specs = [pm] * 6 + [tok, tok, pl.BlockSpec((ng, d), lambda i: (i, 0))]
    out_shape = [pm_shape] * 6 + [tok_shape, tok_shape, jax.ShapeDtypeStruct((nt * ng, d), F32)]
    if not vres:
        out_specs.append(tok)
        out_shape.append(tok_shape)
    return pl.pallas_call(
        kern,
        grid=(nt,),
        in_specs=specs,
        out_specs=out_specs,
        out_shape=out_shape,
        scratch_shapes=[pltpu.VMEM((npair, tm, LANES), F32), pltpu.VMEM((8, d), F32)],
        compiler_params=pltpu.CompilerParams(dimension_semantics=("arbitrary",), vmem_limit_bytes=VMEM_LIMIT_IN),
        name="rw_in",
    )(*args)


def _rw_rec_kernel(*refs, c, sample):
    if sample:
        (r_ref, ld_ref, k_ref, v_ref, kk_ref, kka_ref, tri_ref, ms_ref, mi_ref, eye_ref, s_in,
         y_ref, s_out, s_scr) = refs
    else:
        (r_ref, ld_ref, k_ref, v_ref, kk_ref, kka_ref, tri_ref, ms_ref, mi_ref, eye_ref,
         y_ref, s_out, s_scr) = refs
    npair, rows, _ = r_ref.shape
    n_chunks = rows // c
    n = 2 * c

    if not sample:
        @pl.when(pl.program_id(0) == 0)
        def _():
            s_scr[...] = jnp.zeros_like(s_scr)

    left_k = lax.broadcasted_iota(jnp.int32, (c, LANES), 1) < RW_N
    left_t = lax.broadcasted_iota(jnp.int32, (c, n), 1) < c
    tri = tri_ref[...]
    mask_strict = ms_ref[...]
    mask_incl = mi_ref[...]
    eye = eye_ref[...]
    steps = int(np.log2(c)) - 1

    def stack(xv, left):
        return jnp.concatenate([jnp.where(left, xv, 0.0), jnp.where(left, 0.0, xv)], axis=0)

    stack_k = lambda xv: stack(xv, left_k)
    stack_t = lambda xv: stack(xv, left_t)
    pairs = range(npair)

    nu = min(RW_GROUP_SAMPLE if sample else RW_GROUP, n_chunks)
    wave = nu if sample else RW_WAVE

    def group_body(gi, carry):
        ci = {j: gi * nu + j for j in range(nu)}
        rs = {j: pl.ds(pl.multiple_of(ci[j] * c, c), c) for j in range(nu)}
        ur, bk_end, vst, ltot, a_r, t, mv = {}, {}, {}, {}, {}, {}, {}
        state = {"s": None if sample else [s_scr[p] for p in pairs]}

        def head(js):
            units = [(j, p) for j in js for p in pairs]
            bk = {}
            for j in js:
                ld_all = jnp.concatenate([ld_ref[p, rs[j], :] for p in pairs], axis=1)
                lc_all = _cumsum_rows(tri, ld_all)
                for p in pairs:
                    lc = lc_all[:, LANES * p:LANES * (p + 1)]
                    ld = ld_all[:, LANES * p:LANES * (p + 1)]
                    lt = lc[c - 1:c, :]
                    g_inv = jnp.exp(-lc)
                    g_end = jnp.exp(lt - lc)
                    k = k_ref[p, rs[j], :]
                    kka = kka_ref[p, rs[j], :]
                    ur[j, p] = jnp.concatenate([-kk_ref[p, rs[j], :] * jnp.exp(lc - ld),
                                                r_ref[p, rs[j], :] * jnp.exp(lc)], axis=0)
                    bk[j, p] = jnp.concatenate([stack_k(kka * g_inv), stack_k(k * g_inv)], axis=0)
                    bk_end[j, p] = jnp.concatenate([stack_k(kka * g_end), stack_k(k * g_end)], axis=0)
                    vst[j, p] = stack_k(v_ref[p, rs[j], :])
                    ltot[j, p] = lt
                yield
            big = {u: _dotp(ur[u], bk[u], _NT, P_BIG) for u in units}
            m_ub = {u: big[u][0:c, 0:n] * mask_strict for u in units}
            m_uk = {u: big[u][0:c, n:2 * n] * mask_strict for u in units}
            for u in units:
                a_r[u] = jnp.concatenate([big[u][c:n, 0:n] * mask_incl, big[u][c:n, n:2 * n] * mask_incl], axis=1)
            for u in units:
                mv[u] = _dotp(m_uk[u], vst[u], _NN, P_ST)
            yield
            tt = {u: eye + m_ub[u] for u in units}
            mp = {u: _dot1(m_ub[u], stack_t(m_ub[u])) for u in units}
            yield
            for _ in range(steps - 1):
                both = {u: _dot1(jnp.concatenate([tt[u], mp[u]], axis=0), stack_t(mp[u])) for u in units}
                tt = {u: tt[u] + both[u][0:c] for u in units}
                mp = {u: both[u][c:n] for u in units}
                yield
            tt = {u: tt[u] + _dot1(tt[u], stack_t(mp[u])) for u in units}
            yield
            msp = {u: _split2(m_ub[u]) for u in units}
            tsp = {u: _split2(stack_t(tt[u])) for u in units}
            hl = {u: _dg(jnp.concatenate(msp[u], axis=0), tsp[u][0], _NN) for u in units}
            lo = {u: _dg(msp[u][0], tsp[u][1], _NN) for u in units}
            res = {u: eye - tt[u] + (hl[u][0:c] + hl[u][c:n] + lo[u]) for u in units}
            yield
            for u in units:
                t[u] = tt[u] + _dot1(tt[u], stack_t(res[u]))
            yield

        def tail_independent(js):
            units = [(j, p) for j in js for p in pairs]
            s0 = {(j, p): s_in[p, ci[j]] for (j, p) in units}
            urs = {u: _dotp(ur[u], s0[u], _NT, P_ST) for u in units}
            yield
            e = {u: _dotp(t[u], stack_k(urs[u][0:c] + mv[u]), _NN, P_ST) for u in units}
            yield
            ev = {u: jnp.concatenate([stack_k(e[u]), vst[u]], axis=0) for u in units}
            upd = {u: _dotp(ev[u], bk_end[u], _TN, P_ST) for u in units}
            yield
            for (j, p) in units:
                y_ref[p, rs[j], :] = urs[j, p][c:n] + _dotp(a_r[j, p], ev[j, p], _NN, P_ST)
                s_out[p, ci[j]] = s0[j, p] * jnp.exp(ltot[j, p]) + upd[j, p]
            yield

        def tail(js):
            if sample:
                yield from tail_independent(js)
                return
            emit_y = None
            for j in js:
                sj = [s_in[p, ci[j]] for p in pairs] if sample else state["s"]
                urs = [_dotp(ur[j, p], sj[p], _NT, P_ST) for p in pairs]
                if emit_y is not None:
                    emit_y()
                yield
                e = [_dotp(t[j, p], stack_k(urs[p][0:c] + mv[j, p]), _NN, P_ST) for p in pairs]
                yield
                ev = [jnp.concatenate([stack_k(e[p]), vst[j, p]], axis=0) for p in pairs]
                upd = [_dotp(ev[p], bk_end[j, p], _TN, P_ST) for p in pairs]
                state["s"] = s_new = [sj[p] * jnp.exp(ltot[j, p]) + upd[p] for p in pairs]
                yield

                def emit_y(j=j, urs=urs, ev=ev, s_new=s_new):
                    for p in pairs:
                        y_ref[p, rs[j], :] = urs[p][c:n] + _dotp(a_r[j, p], ev[p], _NN, P_ST)
                        if sample:
                            s_out[p, ci[j]] = s_new[p]
            emit_y()
            yield

        def run(*gens):
            live = list(gens)
            while live:
                for g in list(live):
                    try:
                        next(g)
                    except StopIteration:
                        live.remove(g)

        waves = [list(range(w, min(w + wave, nu))) for w in range(0, nu, wave)]
        run(head(waves[0]))
        for w_prev, w_next in zip(waves[:-1], waves[1:]):
            run(head(w_next), tail(w_prev))
        run(tail(waves[-1]))
        if not sample:
            for p in pairs:
                s_scr[p] = state["s"][p]
        return carry

    lax.fori_loop(0, n_chunks // nu, group_body, 0)

    if not sample:
        @pl.when(pl.program_id(0) == pl.num_programs(0) - 1)
        def _():
            s_out[...] = s_scr[...]


def _rw_rec(seqs, s_bd, cfg, sample):
    npair, n, _ = seqs[0].shape
    tm = TM if sample else TM_RW
    ng, npt = tm // cfg["ts"], cfg["npt"] * TM // tm
    c = cfg["ts"] if sample else CHUNK
    tri = jnp.asarray(np.tril(np.ones((c, c))), BF16)
    side = lambda a: jnp.asarray(np.concatenate([a, a], axis=1), F32)
    mask_incl = side(np.tril(np.ones((c, c))))
    mask_strict = side(np.tril(np.ones((c, c)), -1))
    eye = side(np.eye(c))
    consts = [tri, mask_strict, mask_incl, eye]
    const_specs = [_const_spec(a.shape, (0, 0)) for a in consts]
    kern = functools.partial(_rw_rec_kernel, c=c, sample=sample)
    scratch = [pltpu.VMEM((npair, LANES, LANES), F32)]
    tokp = pl.BlockSpec((npair, tm, LANES), lambda i: (0, i, 0))
    if sample:
        nb = s_bd.shape[1]
        nst = nb // ng
        tok_in = pl.BlockSpec((npair, tm, LANES), lambda i: (0, npt + i, 0))
        sspec = pl.BlockSpec((npair, ng, LANES, LANES), lambda i: (0, i, 0, 0))
        return pl.pallas_call(
            kern,
            grid=(nst,),
            in_specs=[tok_in] * 6 + const_specs + [sspec],
            out_specs=[tokp, sspec],
            out_shape=[jax.ShapeDtypeStruct((npair, nst * tm, LANES), F32),
                       jax.ShapeDtypeStruct((npair, nb, LANES, LANES), F32)],
            scratch_shapes=scratch,
            compiler_params=_params(("arbitrary",)),
            name="rw_rec_sample",
        )(*seqs, *consts, s_bd)
    return pl.pallas_call(
        kern,
        grid=(npt,),
        in_specs=[tokp] * 6 + const_specs,
        out_specs=[tokp, pl.BlockSpec((npair, LANES, LANES), lambda i: (0, 0, 0))],
        out_shape=[jax.ShapeDtypeStruct((npair, npt * tm, LANES), F32),
                   jax.ShapeDtypeStruct((npair, LANES, LANES), F32)],
        scratch_shapes=scratch,
        compiler_params=_params(("arbitrary",)),
        name="rw_rec_prompt",
    )(*seqs, *consts)


def kernel(x_prompt, x_sample, state_ret, state_hgrn, state_wkv, state_shift, c_prompt, c_sample, mod_w, mod_b, norm_mix_g, norm_mlp_g, final_g, mlp_w1, mlp_w2, ab_w_in, ab_w_out, hg_lb, hg_norm_g, rw_mu, rw_w_rkv, rw_w0, rw_w1, rw_w2, rw_a0, rw_a1, rw_a2, rw_v0, rw_v1, rw_v2, rw_g1, rw_g2, rw_k_k, rw_k_a, rw_r_k, rw_ln_g, rw_ln_b, rw_w_out):
    bp, tp, d = x_prompt.shape
    bs, ts, _ = x_sample.shape
    depth = mod_w.shape[0]
    n_ab = ab_w_in.shape[0]
    n_c = rw_w_rkv.shape[0]
    assert bp == 1 and d == 1024 and tp % TM == 0 and tp % CHUNK == 0
    assert (bs * ts) % TM == 0 and TM % ts == 0 and ts % SUB == 0 and ts <= CHUNK
    np_rows, ns_rows = bp * tp, bs * ts
    n = np_rows + ns_rows
    ng = TM // ts
    assert bs % ng == 0
    assert np_rows % TM_RW == 0 and ns_rows % TM_RW == 0 and TM_RW % TM == 0
    assert np_rows % TM_IN == 0 and ns_rows % TM_IN == 0 and TM_IN % TM == 0 and bs % (TM_IN // ts) == 0
    ng_max = TM_IN // ts
    cfg = dict(ng=ng, npt=np_rows // TM, ts=ts, n=n, mod_off=ng_max // ng)
    dk, dv = d // 16, d // 8
    npair = d // LANES
    nh = d // RW_N

    c_all = jnp.concatenate([c_prompt, c_sample], axis=0)
    pad = (-c_all.shape[0]) % 8
    c_all = jnp.pad(c_all, ((0, pad), (0, 0)))
    mod = _modulation(c_all, mod_w, mod_b)
    modt = jnp.concatenate([jnp.broadcast_to(mod[:, 0:1], (depth, ng_max, 6 * d)), mod[:, bp:bp + bs]], axis=1)
    modt = modt.reshape(depth, ng_max + bs, 1, 6 * d)

    x = (x_prompt.reshape(np_rows, d), x_sample.reshape(ns_rows, d))

    pos = jnp.concatenate([jnp.arange(tp, dtype=F32), jnp.tile(PAST_LEN + jnp.arange(ts, dtype=F32), bs)])
    half = dk // 2
    inv = ROPE_BASE ** (-jnp.arange(half, dtype=F32) / half)
    ang = pos[:, None] * inv[None, :]
    cos_t = jnp.tile(jnp.cos(ang), (1, 4))
    sin_t = jnp.tile(jnp.concatenate([-jnp.sin(ang), jnp.sin(ang)], axis=1), (1, 2))

    lb_all = jnp.cumsum(jax.nn.softmax(hg_lb.astype(F32), axis=0), axis=0)
    lb_all = lb_all - lb_all[:1]
    lbc = jnp.stack([jnp.log(lb_all), jnp.log1p(-lb_all), 1.0 - lb_all], axis=1)
    lbc = jnp.pad(lbc, ((0, 0), (0, 5), (0, 0))).reshape(n_ab * 8, d // 2)

    bf = lambda a: a.astype(BF16)
    ab_w_in_b, ab_w_out_b = bf(ab_w_in), bf(ab_w_out)
    mlp_w1_b, mlp_w2_b = bf(mlp_w1), bf(mlp_w2)
    rw_w_out_b = bf(rw_w_out)
    vec = lambda a: a.reshape(a.shape[0], 1, d)
    W = dict(rw_mu=rw_mu, rw_w_rkv=bf(rw_w_rkv), rw_w0=vec(rw_w0), rw_w1=bf(rw_w1), rw_w2=bf(rw_w2),
             rw_a0=vec(rw_a0), rw_a1=bf(rw_a1), rw_a2=bf(rw_a2), rw_g1=bf(rw_g1), rw_g2=bf(rw_g2),
             rw_k_k=vec(rw_k_k), rw_k_a=vec(rw_k_a), rw_r_k=vec(rw_r_k), rw_v0=vec(rw_v0), rw_v1=bf(rw_v1),
             rw_v2=bf(rw_v2))
    seg = jnp.asarray(np.pad(np.kron(np.eye(nh), np.ones((RW_N, 1))), ((0, 0), (0, LANES - nh))), BF16)
    g_mix = norm_mix_g.reshape(depth, 1, d)
    g_mlp = norm_mlp_g.reshape(depth, 1, d)
    fin_g = final_g.reshape(1, d)

    ret_p, ret_s, hg_p, hg_s, wkv_p, wkv_s, sh_p, sh_s = ([] for _ in range(8))
    v_first = None
    for layer in range(depth):
        m = layer // 2
        final = layer == depth - 1
        if layer % 2 == 0:
            z = _ab_in(x, modt, g_mix, ab_w_in_b, cos_t, sin_t, lbc, layer, m, cfg)
            o_p, r_p, h_p = _ab_rec(z, hg_norm_g[m], None, None, cfg, False, d)
            o_s, r_s, h_s = _ab_rec(z, hg_norm_g[m], state_ret[m], jnp.swapaxes(state_hgrn[m], -1, -2), cfg, True, d)
            ret_p.append(r_p[None])
            hg_p.append(jnp.swapaxes(h_p, -1, -2)[None])
            ret_s.append(r_s)
            hg_s.append(jnp.swapaxes(h_s, -1, -2))
            x = _post(x, (o_p, o_s), modt, ab_w_out_b, g_mlp, mlp_w1_b, mlp_w2_b, fin_g, layer, m, cfg, False, final)
        else:
            shift_rows = state_shift[m].reshape(bs, 1, d)
            outs = _rw_in(x, modt, g_mix, shift_rows, W, seg, v_first, layer, m, cfg)
            seqs, gate, bonus, hlast = outs[0:6], outs[6], outs[7], outs[8]
            if v_first is None:
                v_first = outs[9]
            y_p, s_p = _rw_rec(seqs, None, cfg, False)
            sw = state_wkv[m].reshape(bs, npair, 2, RW_N, RW_N)
            zero = jnp.zeros_like(sw[:, :, 0])
            s_bd = jnp.concatenate([jnp.concatenate([sw[:, :, 0], zero], axis=-1),
                                    jnp.concatenate([zero, sw[:, :, 1]], axis=-1)], axis=-2)
            y_s, s_s = _rw_rec(seqs, jnp.swapaxes(s_bd, 0, 1), cfg, True)
            unbd = lambda a: jnp.stack([a[..., :RW_N, :RW_N], a[..., RW_N:, RW_N:]], axis=-3)
            wkv_p.append(unbd(s_p).reshape(1, 1, nh, RW_N, RW_N))
            wkv_s.append(unbd(jnp.swapaxes(s_s, 0, 1)).reshape(bs, nh, RW_N, RW_N))
            sh_p.append(hlast[cfg["npt"] * ng - 1][None, None])
            sh_s.append(hlast[cfg["npt"] * ng:][None])
            x = _post(x, (y_p, y_s, bonus, gate, vec(rw_ln_g), vec(rw_ln_b), seg), modt, rw_w_out_b, g_mlp,
                      mlp_w1_b, mlp_w2_b, fin_g, layer, m, cfg, True, final)

    out_p, out_s = x
    return (out_p.reshape(bp, tp, d), out_s.reshape(bs, ts, d),
            jnp.stack(ret_p), jnp.stack(ret_s), jnp.stack(hg_p), jnp.stack(hg_s),
            jnp.concatenate(wkv_p, axis=0), jnp.stack(wkv_s),
            jnp.concatenate(sh_p, axis=0), jnp.concatenate(sh_s, axis=0))
```

```python
import functools

import numpy as np
import jax
import jax.numpy as jnp
from jax import lax
from jax.experimental import pallas as pl
from jax.experimental.pallas import tpu as pltpu

F32 = jnp.float32
BF16 = jnp.bfloat16

CHUNK = 64
PAST_LEN = 2048
ROPE_BASE = 10000.0
RET_HEADS = 4
HG_HEADS = 4
RW_N = 64
NORM_EPS = 1e-6
RW_LN_EPS = 64e-5

LANES = 128
SUB = 16
RW_GROUP = 2
RW_WAVE = 2
RW_GROUP_SAMPLE = 4
TM_RW = 256
ROW_BLOCK = 256
TM_IN = 512
VMEM_LIMIT_IN = 60 * 1024 * 1024
TM = 256
VMEM_LIMIT = 56 * 1024 * 1024


def _sigmoid(x):
    return 1.0 / (1.0 + jnp.exp(-x))


def _split2(x):
    hi = x.astype(BF16)
    lo = (x - hi.astype(F32)).astype(BF16)
    return hi, lo


def _split3(x):
    a = x.astype(BF16)
    r = x - a.astype(F32)
    b = r.astype(BF16)
    c = (r - b.astype(F32)).astype(BF16)
    return a, b, c


_NN = (((1,), (0,)), ((), ()))
_NT = (((1,), (1,)), ((), ()))
_TN = (((0,), (0,)), ((), ()))


def _dg(a, b, dims):
    return lax.dot_general(a, b, dims, preferred_element_type=F32)


def _dot1(a, b, dims=_NN):
    return _dg(a.astype(BF16), b.astype(BF16), dims)


def _dot3(a, b, dims=_NN):
    ah, al = _split2(a)
    bh, bl = _split2(b)
    return _dg(ah, bh, dims) + _dg(ah, bl, dims) + _dg(al, bh, dims)


def _dotp(a, b, dims, passes):
    return _dot1(a, b, dims) if passes == 1 else _dot3(a, b, dims)


P_BIG = 1
P_ST = 1


def _cumsum_rows(tri_bf16, x):
    a, b, c = _split3(x)
    return _dg(tri_bf16, a, _NN) + _dg(tri_bf16, b, _NN) + _dg(tri_bf16, c, _NN)


def _segsum(x, seg_bf16):
    per_head = _dg(x.astype(BF16), seg_bf16, _NN)
    return _dg(per_head.astype(BF16), seg_bf16, _NT)


def _rms(x, g):
    ms = jnp.mean(x * x, axis=-1, keepdims=True)
    return x * lax.rsqrt(ms + NORM_EPS) * g


def _modulate(y, sc, sh):
    rows, d = y.shape
    ng = sc.shape[0]
    y3 = y.reshape(ng, rows // ng, d)
    return (y3 * (1.0 + sc) + sh).reshape(rows, d)


def _gate_add(x, gt, out):
    rows, d = x.shape
    ng = gt.shape[0]
    return (x.reshape(ng, rows // ng, d) + gt * out.reshape(ng, rows // ng, d)).reshape(rows, d)


def _rows_specs(x, npt):
    if isinstance(x, tuple):
        d = x[0].shape[1]
        return list(x), [pl.BlockSpec((TM, d), lambda i: (jnp.minimum(i, npt - 1), 0)),
                         pl.BlockSpec((TM, d), lambda i: (jnp.maximum(i - npt, 0), 0))]
    return [x], [pl.BlockSpec((TM, x.shape[1]), lambda i: (i, 0))]


def _rows_value(refs, npt):
    if len(refs) == 1:
        return refs[0][...]
    p_ref, s_ref = refs
    rows, d = p_ref.shape
    s_i = jnp.where(pl.program_id(0) >= npt, 1, 0)
    is_sample = (lax.broadcasted_iota(jnp.int32, (rows, LANES), 0) * 0 + s_i) == 1
    return jnp.concatenate([jnp.where(is_sample, s_ref[:, LANES * p:LANES * (p + 1)],
                                      p_ref[:, LANES * p:LANES * (p + 1)]) for p in range(d // LANES)], axis=-1)


def _const_spec(block, index):
    return pl.BlockSpec(block, lambda *_: index, pipeline_mode=pl.Buffered(1))


def _params(sem):
    return pltpu.CompilerParams(dimension_semantics=sem, vmem_limit_bytes=VMEM_LIMIT)


def _mod_kernel(c_ref, w_ref, b_ref, o_ref):
    c = c_ref[...]
    s = c * _sigmoid(c)
    o_ref[0] = _dot1(s, w_ref[0]) + b_ref[0]


def _modulation(c_all, mod_w, mod_b):
    depth, d, d6 = mod_w.shape
    rows = c_all.shape[0]
    nt = d6 // d
    return pl.pallas_call(
        _mod_kernel,
        grid=(depth, nt),
        in_specs=[
            pl.BlockSpec((rows, d), lambda l, j: (0, 0)),
            pl.BlockSpec((1, d, d), lambda l, j: (l, 0, j)),
            pl.BlockSpec((1, 1, d), lambda l, j: (l, 0, j)),
        ],
        out_specs=pl.BlockSpec((1, rows, d), lambda l, j: (l, 0, j)),
        out_shape=jax.ShapeDtypeStruct((depth, rows, d6), F32),
        compiler_params=_params(("arbitrary", "arbitrary")),
        name="modulation",
    )(c_all, mod_w, mod_b.reshape(depth, 1, d6))


def _ab_in_kernel(*refs, d, npt, dk_scale):
    mod_ref, g_ref, w_ref, cos_ref, sin_ref, lb_ref, o_ref = refs[-7:]
    x = _rows_value(refs[:-7], npt)
    h = _modulate(_rms(x, g_ref[0]), mod_ref[0, :, :, d:2 * d], mod_ref[0, :, :, 0:d])
    hb = h.astype(BF16)
    hw = d // 2

    def proj(a):
        return _dg(hb, w_ref[0, :, a:a + hw], _NN)

    z = proj(4 * hw)
    log_lb = lb_ref[0:1, :]
    log1m_lb = lb_ref[1:2, :]
    ls = jnp.minimum(z, 0.0) - jnp.log1p(jnp.exp(-jnp.abs(z)))
    c = log1m_lb + ls
    m = jnp.maximum(log_lb, c)
    o_ref[:, 4 * hw:5 * hw] = m + jnp.log1p(jnp.exp(-jnp.abs(log_lb - c)))
    o_ref[:, 5 * hw:6 * hw] = c - z
    qk = proj(0)
    lane = lax.broadcasted_iota(jnp.int32, (x.shape[0], LANES), 1)
    first_half = (lane % 64) < 32
    cos = cos_ref[...]
    sin = sin_ref[...]
    for s in range(hw // LANES):
        xs = qk[:, LANES * s:LANES * (s + 1)]
        sw = jnp.where(first_half, pltpu.roll(xs, 96, 1), pltpu.roll(xs, 32, 1))
        rot = xs * cos + sw * sin
        if s < hw // (2 * LANES):
            rot = rot * dk_scale
        o_ref[:, LANES * s:LANES * (s + 1)] = rot
    o_ref[:, hw:2 * hw] = proj(hw)
    z = proj(2 * hw)
    o_ref[:, 2 * hw:3 * hw] = z * _sigmoid(z)
    z = proj(3 * hw)
    o_ref[:, 3 * hw:4 * hw] = z * _sigmoid(z)
    o_ref[:, 6 * hw:7 * hw] = proj(5 * hw)
    o_ref[:, 7 * hw:8 * hw] = _sigmoid(proj(6 * hw))


def _ab_in(x, modt, g, w_in, cos_t, sin_t, lbc, layer, m, cfg):
    ng, npt, n = cfg["ng"], cfg["npt"], cfg["n"]
    x_args, x_specs = _rows_specs(x, npt)
    d = x_args[0].shape[1]
    nt = n // TM
    zw = 4 * d
    kern = functools.partial(_ab_in_kernel, d=d, npt=npt, dk_scale=float((d // 16) ** -0.5))
    return pl.pallas_call(
        kern,
        grid=(nt,),
        in_specs=x_specs + [
            pl.BlockSpec((1, ng, 1, 6 * d), lambda i: (layer, jnp.maximum(i - npt + cfg["mod_off"], 0), 0, 0)),
            _const_spec((1, 1, d), (layer, 0, 0)),
            _const_spec((1, d, w_in.shape[2]), (m, 0, 0)),
            pl.BlockSpec((TM, LANES), lambda i: (i, 0)),
            pl.BlockSpec((TM, LANES), lambda i: (i, 0)),
            _const_spec((8, d // 2), (m, 0)),
        ],
        out_specs=pl.BlockSpec((TM, zw), lambda i: (i, 0)),
        out_shape=jax.ShapeDtypeStruct((n, zw), F32),
        compiler_params=_params(("arbitrary",)),
        name="ab_in",
    )(*x_args, modt, g, w_in, cos_t, sin_t, lbc)


def _ab_rec_kernel(*refs, d, c, sample):
    if sample:
        (z_ref, dmat_ref, rdec_ref, cdec_ref, gc_ref, tri_ref, hgg_ref, ones_ref, sret_in, shg_in,
         o_ref, sret_out, shg_out, sret_scr, shg_scr) = refs
    else:
        (z_ref, dmat_ref, rdec_ref, cdec_ref, gc_ref, tri_ref, hgg_ref, ones_ref,
         o_ref, sret_out, shg_out, sret_scr, shg_scr) = refs
    hw = d // 2
    rows = z_ref.shape[0]
    n_chunks = rows // c
    dk = d // 16

    if not sample:
        @pl.when(pl.program_id(0) == 0)
        def _():
            sret_scr[...] = jnp.zeros_like(sret_scr)
            shg_scr[...] = jnp.zeros_like(shg_scr)

    lane_c = lax.broadcasted_iota(jnp.int32, (c, LANES), 1)
    row_h = lax.broadcasted_iota(jnp.int32, (SUB // 2, LANES), 0)
    ones = ones_ref[...]
    tri = tri_ref[...]

    def hg_chunk(r0, b):
        col = lambda g, h: slice(g * hw + LANES * h, g * hw + LANES * (h + 1))
        units = [(i, h) for i in range(c // SUB) for h in range(HG_HEADS)]
        rs = {i: pl.ds(r0 + SUB * i, SUB) for i in range(c // SUB)}
        q = {u: z_ref[rs[u[0]], col(3, u[1])] for u in units}
        v = {u: z_ref[rs[u[0]], col(6, u[1])] for u in units}
        bs = {(i, h): b[SUB * i:SUB * (i + 1), LANES * h:LANES * (h + 1)] for (i, h) in units}
        bl = {u: bs[u][SUB - 1:SUB, :] for u in units}
        w = {u: bs[u] - z_ref[rs[u[0]], col(5, u[1])] for u in units}
        hs = SUB // 2
        r = {}
        for u in units:
            q_lo, q_hi = q[u][0:hs], q[u][hs:SUB]
            b_lo, b_hi = bs[u][0:hs], bs[u][hs:SUB]
            p_lo, p_hi = [], []
            for j in range(SUB):
                wj = w[u][j:j + 1, :]
                if j < hs:
                    p_lo.append(jnp.where(row_h >= j, q_lo * jnp.exp(b_lo - wj), 0.0))
                    p_hi.append(q_hi * jnp.exp(b_hi - wj))
                else:
                    p_hi.append(jnp.where(row_h >= j - hs, q_hi * jnp.exp(b_hi - wj), 0.0))
            r[u] = _dot1(jnp.concatenate(p_lo + p_hi, axis=0), ones)
        upd = {u: _dot1(v[u], jnp.exp(bl[u] - w[u]), _TN) for u in units}
        inner = {}
        for u in units:
            acc_lo = r[u][0:hs] * v[u][0:1, :]
            for j in range(1, hs):
                acc_lo = acc_lo + r[u][hs * j:hs * (j + 1)] * v[u][j:j + 1, :]
            base = hs * hs
            acc_hi = r[u][base:base + hs] * v[u][0:1, :]
            for j in range(1, SUB):
                acc_hi = acc_hi + r[u][base + hs * j:base + hs * (j + 1)] * v[u][j:j + 1, :]
            inner[u] = jnp.concatenate([acc_lo, acc_hi], axis=0)
        st = {h: shg_scr[h] for h in range(HG_HEADS)}
        for u in units:
            i, h = u
            out = inner[u] + _dot1(q[u] * jnp.exp(bs[u]), st[h], _NT)
            st[h] = st[h] * jnp.exp(bl[u]) + upd[u]
            ms = jnp.mean(out * out, axis=-1, keepdims=True)
            o_ref[rs[i], col(1, h)] = out * lax.rsqrt(ms + NORM_EPS) * hgg_ref[...] * z_ref[rs[i], col(7, h)]
        for h in range(HG_HEADS):
            shg_scr[h] = st[h]

    def ret_chunk(r0):
        rs = pl.ds(r0, c)
        heads = range(RET_HEADS)
        zeros = jnp.zeros((dk, LANES), F32)
        qp = [z_ref[rs, LANES * p:LANES * (p + 1)] for p in range(RET_HEADS // 2)]
        kp = [z_ref[rs, hw // 2 + LANES * p:hw // 2 + LANES * (p + 1)] for p in range(RET_HEADS // 2)]
        v = [z_ref[rs, hw + LANES * h:hw + LANES * (h + 1)] for h in heads]
        s = [sret_scr[h] for h in heads]
        qh = [jnp.where((lane_c // dk) == (h % 2), qp[h // 2], 0.0) for h in heads]
        scores = [_dot1(qh[h], kp[h // 2], _NT) * dmat_ref[h] for h in heads]
        spad = [jnp.concatenate([s[h], zeros] if h % 2 == 0 else [zeros, s[h]], axis=0) for h in heads]
        cross = [_dot1(qp[h // 2], spad[h]) * rdec_ref[h] for h in heads]
        u = [_dot1(kp[h // 2] * cdec_ref[h], v[h], _TN) for h in heads]
        inner = [_dot1(scores[h], v[h]) for h in heads]
        for h in heads:
            e = h % 2
            out = inner[h] + cross[h]
            sret_scr[h] = s[h] * gc_ref[h] + u[h][dk * e:dk * (e + 1)]
            ms = jnp.mean(out * out, axis=-1, keepdims=True)
            o_ref[rs, LANES * h:LANES * (h + 1)] = (out * lax.rsqrt(ms + NORM_EPS)
                                                    * z_ref[rs, 2 * hw + LANES * h:2 * hw + LANES * (h + 1)])

    def chunk_body(ci, carry):
        r0 = pl.multiple_of(ci * c, c)
        if sample:
            sret_scr[...] = sret_in[ci]
            shg_scr[...] = shg_in[ci]
        b = _cumsum_rows(tri, z_ref[pl.ds(r0, c), 4 * hw:5 * hw])
        ret_chunk(r0)
        hg_chunk(r0, b)
        if sample:
            sret_out[ci] = sret_scr[...]
            shg_out[ci] = shg_scr[...]
        return carry

    lax.fori_loop(0, n_chunks, chunk_body, 0, unroll=4)

    if not sample:
        @pl.when(pl.program_id(0) == pl.num_programs(0) - 1)
        def _():
            sret_out[...] = sret_scr[...]
            shg_out[...] = shg_scr[...]


def _ret_consts(c):
    heads = jnp.arange(RET_HEADS, dtype=F32)
    log_g = jnp.log1p(-jnp.exp2(-5.0 - heads))
    idx = jnp.arange(c, dtype=F32)
    rel = idx[:, None] - idx[None, :]
    dmat = jnp.exp(jnp.where(rel[None] >= 0, rel[None] * log_g[:, None, None], -jnp.inf))
    rdec = jnp.exp((idx[None, :] + 1.0) * log_g[:, None])
    cdec = jnp.exp((c - 1.0 - idx)[None, :] * log_g[:, None])
    gc = jnp.exp(c * log_g)
    bc = lambda a: jnp.broadcast_to(a[:, :, None], (RET_HEADS, c, LANES))
    return dmat, bc(rdec), bc(cdec), gc


def _ab_rec(z, hg_g, sret, shg_t, cfg, sample, d):
    n = z.shape[0]
    ng, npt = cfg["ng"], cfg["npt"]
    c = cfg["ts"] if sample else CHUNK
    dk, dv = d // 16, d // 8
    dmat, rdec, cdec, gc = _ret_consts(c)
    gcb = jnp.broadcast_to(gc[:, None, None], (RET_HEADS, dk, LANES))
    tri = np.kron(np.eye(c // SUB), np.tril(np.ones((SUB, SUB))))
    tri = jnp.asarray(tri, BF16)
    ones = jnp.ones((LANES, LANES), BF16)
    kern = functools.partial(_ab_rec_kernel, d=d, c=c, sample=sample)
    consts = [dmat, rdec, cdec, gcb, tri, hg_g.reshape(1, dv), ones]
    const_specs = [
        _const_spec(dmat.shape, (0, 0, 0)), _const_spec(rdec.shape, (0, 0, 0)),
        _const_spec(cdec.shape, (0, 0, 0)), _const_spec(gcb.shape, (0, 0, 0)),
        _const_spec(tri.shape, (0, 0)), _const_spec((1, dv), (0, 0)), _const_spec(ones.shape, (0, 0)),
    ]
    scratch = [pltpu.VMEM((RET_HEADS, dk, dv), F32), pltpu.VMEM((HG_HEADS, dv, dv), F32)]
    if sample:
        nb = sret.shape[0]
        nst = nb // ng
        return pl.pallas_call(
            kern,
            grid=(nst,),
            in_specs=[pl.BlockSpec((TM, z.shape[1]), lambda i: (npt + i, 0))] + const_specs + [
                pl.BlockSpec((ng, RET_HEADS, dk, dv), lambda i: (i, 0, 0, 0)),
                pl.BlockSpec((ng, HG_HEADS, dv, dv), lambda i: (i, 0, 0, 0)),
            ],
            out_specs=[
                pl.BlockSpec((TM, d), lambda i: (i, 0)),
                pl.BlockSpec((ng, RET_HEADS, dk, dv), lambda i: (i, 0, 0, 0)),
                pl.BlockSpec((ng, HG_HEADS, dv, dv), lambda i: (i, 0, 0, 0)),
            ],
            out_shape=[
                jax.ShapeDtypeStruct((nst * TM, d), F32),
                jax.ShapeDtypeStruct((nb, RET_HEADS, dk, dv), F32),
                jax.ShapeDtypeStruct((nb, HG_HEADS, dv, dv), F32),
            ],
            scratch_shapes=scratch,
            compiler_params=_params(("arbitrary",)),
            name="ab_rec_sample",
        )(z, *consts, sret, shg_t)
    return pl.pallas_call(
        kern,
        grid=(npt,),
        in_specs=[pl.BlockSpec((TM, z.shape[1]), lambda i: (i, 0))] + const_specs,
        out_specs=[
            pl.BlockSpec((TM, d), lambda i: (i, 0)),
            pl.BlockSpec((RET_HEADS, dk, dv), lambda i: (0, 0, 0)),
            pl.BlockSpec((HG_HEADS, dv, dv), lambda i: (0, 0, 0)),
        ],
        out_shape=[
            jax.ShapeDtypeStruct((npt * TM, d), F32),
            jax.ShapeDtypeStruct((RET_HEADS, dk, dv), F32),
            jax.ShapeDtypeStruct((HG_HEADS, dv, dv), F32),
        ],
        scratch_shapes=scratch,
        compiler_params=_params(("arbitrary",)),
        name="ab_rec_prompt",
    )(z, *consts)


def _post_kernel(*refs, d, npt, nx, rwkv, final):
    x = _rows_value(refs[:nx], npt)
    it = iter(refs[nx:])
    if rwkv:
        yp_ref, ys_ref, bonus_ref, gate_ref, lng_ref, lnb_ref, seg_ref = (next(it) for _ in range(7))
    else:
        op_ref, os_ref = next(it), next(it)
    mod_ref, wout_ref, g2_ref, w1_ref, w2_ref = (next(it) for _ in range(5))
    if final:
        fg_ref = next(it)
        outp_ref, outs_ref = next(it), next(it)
    else:
        out_ref = next(it)

    i = pl.program_id(0)
    if rwkv:
        s_i = jnp.where(i >= npt, 1, 0)
        is_sample = (lax.broadcasted_iota(jnp.int32, (x.shape[0], LANES), 0) * 0 + s_i) == 1
        y = jnp.concatenate([jnp.where(is_sample, ys_ref[p], yp_ref[p]) for p in range(yp_ref.shape[0])],
                            axis=-1)
        seg = seg_ref[...]
        inv_n = 1.0 / RW_N
        mean = _segsum(y, seg) * inv_n
        yc = y - mean
        var = _segsum(yc * yc, seg) * inv_n
        o = yc * lax.rsqrt(var + RW_LN_EPS) * lng_ref[0] + lnb_ref[0]
        o = (o + bonus_ref[...]) * gate_ref[...]
    else:
        o = _rows_value((op_ref, os_ref), npt)
    gt1 = mod_ref[0, :, :, 2 * d:3 * d]
    sh2 = mod_ref[0, :, :, 3 * d:4 * d]
    sc2 = mod_ref[0, :, :, 4 * d:5 * d]
    gt2 = mod_ref[0, :, :, 5 * d:6 * d]
    x1 = _gate_add(x, gt1, _dg(o.astype(BF16), wout_ref[0], _NN))
    hb = _modulate(_rms(x1, g2_ref[0]), sc2, sh2).astype(BF16)
    ff = w1_ref.shape[2]
    acc = jnp.zeros_like(x1)
    for j in range(ff // d):
        u = jnp.maximum(_dg(hb, w1_ref[0, :, j * d:(j + 1) * d], _NN), 0.0)
        acc = acc + _dg((u * u).astype(BF16), w2_ref[0, j * d:(j + 1) * d, :], _NN)
    x2 = _gate_add(x1, gt2, acc)
    if final:
        x2 = _rms(x2, fg_ref[...])

        @pl.when(i < npt)
        def _():
            outp_ref[...] = x2

        @pl.when(i >= npt)
        def _():
            outs_ref[...] = x2
    else:
        out_ref[...] = x2


def _post(x, mixer_in, modt, wout, g2, w1, w2, final_g, layer, m, cfg, rwkv, final):
    ng, npt, n = cfg["ng"], cfg["npt"], cfg["n"]
    args, specs = _rows_specs(x, npt)
    d = args[0].shape[1]
    nt = n // TM
    kern = functools.partial(_post_kernel, d=d, npt=npt, nx=len(args), rwkv=rwkv, final=final)
    tok = pl.BlockSpec((TM, d), lambda i: (i, 0))
    tok_p = pl.BlockSpec((TM, d), lambda i: (jnp.minimum(i, npt - 1), 0))
    tok_s = pl.BlockSpec((TM, d), lambda i: (jnp.maximum(i - npt, 0), 0))
    if rwkv:
        y_p, y_s, bonus, gate, lng, lnb, seg = mixer_in
        args += [y_p, y_s, bonus, gate, lng, lnb, seg]
        specs += [pl.BlockSpec((y_p.shape[0], TM, LANES), lambda i: (0, jnp.minimum(i, npt - 1), 0)),
                  pl.BlockSpec((y_s.shape[0], TM, LANES), lambda i: (0, jnp.maximum(i - npt, 0), 0)), tok, tok,
                  _const_spec((1, 1, d), (m, 0, 0)), _const_spec((1, 1, d), (m, 0, 0)),
                  _const_spec((d, LANES), (0, 0))]
    else:
        args += list(mixer_in)
        specs += [tok_p, tok_s]
    args += [modt, wout, g2, w1, w2]
    specs += [
        pl.BlockSpec((1, ng, 1, 6 * d), lambda i: (layer, jnp.maximum(i - npt + cfg["mod_off"], 0), 0, 0)),
        _const_spec((1, d, d), (m, 0, 0)),
        _const_spec((1, 1, d), (layer, 0, 0)),
        _const_spec((1, d, w1.shape[2]), (layer, 0, 0)),
        _const_spec((1, w2.shape[1], d), (layer, 0, 0)),
    ]
    if final:
        args += [final_g]
        specs += [_const_spec((1, d), (0, 0))]
    if final:
        out_specs = [tok_p, tok_s]
        out_shape = [jax.ShapeDtypeStruct((npt * TM, d), F32), jax.ShapeDtypeStruct((n - npt * TM, d), F32)]
    else:
        out_specs = tok
        out_shape = jax.ShapeDtypeStruct((n, d), F32)
    return pl.pallas_call(
        kern,
        grid=(nt,),
        in_specs=specs,
        out_specs=out_specs,
        out_shape=out_shape,
        compiler_params=_params(("arbitrary",)),
        name="post_rwkv" if rwkv else "post_ab",
    )(*args)


def _rw_in_kernel(*refs, d, ts, npt, vres):
    it = iter(refs)
    x_ref, mod_ref, g_ref, shift_ref, mu_ref, wrkv_ref = (next(it) for _ in range(6))
    w0_ref, w1_ref, w2_ref, a0_ref, a1_ref, a2_ref, g1_ref, g2_ref = (next(it) for _ in range(8))
    kk_ref, ka_ref, rk_ref, seg_ref = (next(it) for _ in range(4))
    if vres:
        vf_ref, v0_ref, v1_ref, v2_ref = (next(it) for _ in range(4))
    r_out, ld_out, k_out, v_out, kk_out, kka_out, gate_out, bonus_out, hlast_out = (next(it) for _ in range(9))
    if not vres:
        vtok_out = next(it)
    h_scr, carry_scr = next(it), next(it)

    i = pl.program_id(0)
    rows = x_ref.shape[0]
    ng = rows // ts

    @pl.when(i == 0)
    def _():
        carry_scr[...] = jnp.zeros_like(carry_scr)

    x = x_ref[...]
    h = _modulate(_rms(x, g_ref[0]), mod_ref[0, :, :, d:2 * d], mod_ref[0, :, :, 0:d])
    for p in range(d // LANES):
        h_scr[p] = h[:, LANES * p:LANES * (p + 1)]
        hlast_out[:, LANES * p:LANES * (p + 1)] = h_scr[p, pl.ds(ts - 1, ng, stride=ts), :]
    rolled = pltpu.roll(h, 1, 0)
    row = lax.broadcasted_iota(jnp.int32, (rows, d), 0)
    s_i = jnp.where(i >= npt, 1, 0)
    first = ((row % ts) == 0) & ((row * (1 - s_i)) == 0)
    seq_prev = jnp.broadcast_to(shift_ref[...], (ng, ts, d)).reshape(rows, d)
    prev_first = jnp.where((row * 0 + s_i) == 1, seq_prev, jnp.broadcast_to(carry_scr[0:1, :], (rows, d)))
    prev = jnp.where(first, prev_first, rolled)
    carry_scr[0:1, :] = h[rows - 1:rows, :]

    seg = seg_ref[...]

    def block(b0):
        rb = slice(b0, b0 + ROW_BLOCK)
        hb = h[rb]
        xx = prev[rb] - hb
        mix = lambda j: (hb + xx * mu_ref[0, j:j + 1, :]).astype(BF16)
        xr, xw, xk, xv, xa, xg = (mix(j) for j in range(6))
        yield
        w_dn = _dg(xw, w1_ref[0], _NN)
        a_dn = _dg(xa, a1_ref[0], _NN)
        g_dn = _dg(xg, g1_ref[0], _NN)
        if vres:
            v_dn = _dg(xv, v1_ref[0], _NN)
        k = _dg(xk, wrkv_ref[0, 1], _NN)
        yield
        v = _dg(xv, wrkv_ref[0, 2], _NN)
        kk = k * kk_ref[0]
        yield
        r = _dg(xr, wrkv_ref[0, 0], _NN)
        yield
        kk_ss = _segsum(kk * kk, seg)
        wl = w0_ref[0] + _dot1(jnp.tanh(w_dn), w2_ref[0])
        a = _sigmoid(a0_ref[0] + _dot1(a_dn, a2_ref[0]))
        gate_out[rb, :] = _dot1(_sigmoid(g_dn), g2_ref[0])
        if vres:
            lv = v0_ref[0] + _dot1(v_dn, v2_ref[0])
            v = v + (vf_ref[rb, :] - v) * _sigmoid(lv)
        else:
            vtok_out[rb, :] = v
        yield
        nwl = -wl
        w = -(jnp.maximum(nwl, 0.0) + jnp.log1p(jnp.exp(-jnp.abs(nwl)))) - 0.5
        ld = -jnp.exp(w)
        kk = kk / jnp.maximum(jnp.sqrt(kk_ss), 1e-12)
        km = k * (1.0 + (a - 1.0) * ka_ref[0])
        bonus_out[rb, :] = _segsum(r * km * rk_ref[0], seg) * v
        kka = kk * a
        for p in range(d // LANES):
            cs = slice(LANES * p, LANES * (p + 1))
            r_out[p, rb, :] = r[:, cs]
            ld_out[p, rb, :] = ld[:, cs]
            k_out[p, rb, :] = km[:, cs]
            v_out[p, rb, :] = v[:, cs]
            kk_out[p, rb, :] = kk[:, cs]
            kka_out[p, rb, :] = kka[:, cs]
        yield

    live = []
    starts = list(range(0, rows, ROW_BLOCK))
    while starts or live:
        if starts:
            live.append(block(starts.pop(0)))
        for g in list(live):
            try:
                next(g)
            except StopIteration:
                live.remove(g)


def _rw_in(x, modt, g, shift_rows, W, seg, v_first, layer, m, cfg):
    n, d = x.shape
    ts, tm = cfg["ts"], TM_IN
    ng, npt = tm // ts, cfg["npt"] * TM // tm
    nt = n // tm
    npair = d // LANES
    vres = v_first is not None
    kern = functools.partial(_rw_in_kernel, d=d, ts=ts, npt=npt, vres=vres)
    tok = pl.BlockSpec((tm, d), lambda i: (i, 0))
    vec = lambda: _const_spec((1, 1, d), (m, 0, 0))
    lora = lambda a: _const_spec((1,) + a.shape[1:], (m, 0, 0))
    args = [x, modt, g, shift_rows, W["rw_mu"], W["rw_w_rkv"],
            W["rw_w0"], W["rw_w1"], W["rw_w2"], W["rw_a0"], W["rw_a1"], W["rw_a2"], W["rw_g1"], W["rw_g2"],
            W["rw_k_k"], W["rw_k_a"], W["rw_r_k"], seg]
    specs = [
        tok,
        pl.BlockSpec((1, ng, 1, 2 * d), lambda i: (layer, jnp.maximum(i - npt + 1, 0), 0, 0)),
        _const_spec((1, 1, d), (layer, 0, 0)),
        pl.BlockSpec((ng, 1, d), lambda i: (jnp.maximum(i - npt, 0), 0, 0)),
        _const_spec((1, 6, d), (m, 0, 0)),
        _const_spec((1, 3, d, d), (m, 0, 0, 0)),
        vec(), lora(W["rw_w1"]), lora(W["rw_w2"]), vec(), lora(W["rw_a1"]), lora(W["rw_a2"]),
        lora(W["rw_g1"]), lora(W["rw_g2"]), vec(), vec(), vec(), _const_spec((d, LANES), (0, 0)),
    ]
    if vres:
        args += [v_first, W["rw_v0"], W["rw_v1"], W["rw_v2"]]
        specs += [tok, _const_spec((1, 1, d), (m - 1, 0, 0)),
                  _const_spec((1,) + W["rw_v1"].shape[1:], (m - 1, 0, 0)),
                  _const_spec((1,) + W["rw_v2"].shape[1:], (m - 1, 0, 0))]
    pm = pl.BlockSpec((npair, tm, LANES), lambda i: (0, i, 0))
    pm_shape = jax.ShapeDtypeStruct((npair, n, LANES), F32)
    tok_shape = jax.ShapeDtypeStruct((n, d), F32)
    out_specs = [pm] * 6 + [tok, tok, pl.BlockSpec((ng, d), lambda i: (i, 0))]
    out_shape = [pm_shape] * 6 + [tok_shape, tok_shape, jax.ShapeDtypeStruct((nt * ng, d), F32)]
    if not vres:
        out_specs.append(tok)
        out_shape.append(tok_shape)
    return pl.pallas_call(
        kern,
        grid=(nt,),
        in_specs=specs,
        out_specs=out_specs,
        out_shape=out_shape,
        scratch_shapes=[pltpu.VMEM((npair, tm, LANES), F32), pltpu.VMEM((8, d), F32)],
        compiler_params=pltpu.CompilerParams(dimension_semantics=("arbitrary",), vmem_limit_bytes=VMEM_LIMIT_IN),
        name="rw_in",
    )(*args)


def _rw_rec_kernel(*refs, c, sample):
    if sample:
        (r_ref, ld_ref, k_ref, v_ref, kk_ref, kka_ref, tri_ref, ms_ref, mi_ref, eye_ref, lvl_ref, s_in,
         y_ref, s_out, s_scr) = refs
    else:
        (r_ref, ld_ref, k_ref, v_ref, kk_ref, kka_ref, tri_ref, ms_ref, mi_ref, eye_ref, lvl_ref,
         y_ref, s_out, s_scr) = refs
    npair, rows, _ = r_ref.shape
    n_chunks = rows // c
    n = 2 * c

    if not sample:
        @pl.when(pl.program_id(0) == 0)
        def _():
            s_scr[...] = jnp.zeros_like(s_scr)

    left_k = lax.broadcasted_iota(jnp.int32, (c, LANES), 1) < RW_N
    left_t = lax.broadcasted_iota(jnp.int32, (c, n), 1) < c
    tri = tri_ref[...]
    mask_strict = ms_ref[...]
    mask_incl = mi_ref[...]
    eye = eye_ref[...]
    levels = lvl_ref.shape[0]

    def stack(xv, left):
        return jnp.concatenate([jnp.where(left, xv, 0.0), jnp.where(left, 0.0, xv)], axis=0)

    stack_k = lambda xv: stack(xv, left_k)
    stack_t = lambda xv: stack(xv, left_t)
    pairs = range(npair)

    nu = min(RW_GROUP_SAMPLE if sample else RW_GROUP, n_chunks)
    wave = nu if sample else RW_WAVE

    def group_body(gi, carry):
        ci = {j: gi * nu + j for j in range(nu)}
        rs = {j: pl.ds(pl.multiple_of(ci[j] * c, c), c) for j in range(nu)}
        ur, bk_end, vst, ltot, a_r, t, mv = {}, {}, {}, {}, {}, {}, {}
        state = {"s": None if sample else [s_scr[p] for p in pairs]}

        def head(js):
            units = [(j, p) for j in js for p in pairs]
            bk = {}
            for j in js:
                ld_all = jnp.concatenate([ld_ref[p, rs[j], :] for p in pairs], axis=1)
                lc_all = _cumsum_rows(tri, ld_all)
                for p in pairs:
                    lc = lc_all[:, LANES * p:LANES * (p + 1)]
                    ld = ld_all[:, LANES * p:LANES * (p + 1)]
                    lt = lc[c - 1:c, :]
                    g_inv = jnp.exp(-lc)
                    g_end = jnp.exp(lt - lc)
                    k = k_ref[p, rs[j], :]
                    kka = kka_ref[p, rs[j], :]
                    ur[j, p] = jnp.concatenate([-kk_ref[p, rs[j], :] * jnp.exp(lc - ld),
                                                r_ref[p, rs[j], :] * jnp.exp(lc)], axis=0)
                    bk[j, p] = jnp.concatenate([stack_k(kka * g_inv), stack_k(k * g_inv)], axis=0)
                    bk_end[j, p] = jnp.concatenate([stack_k(kka * g_end), stack_k(k * g_end)], axis=0)
                    vst[j, p] = stack_k(v_ref[p, rs[j], :])
                    ltot[j, p] = lt
                yield
            big = {u: _dotp(ur[u], bk[u], _NT, P_BIG) for u in units}
            m_ub = {u: big[u][0:c, 0:n] * mask_strict for u in units}
            m_uk = {u: big[u][0:c, n:2 * n] * mask_strict for u in units}
            for u in units:
                a_r[u] = jnp.concatenate([big[u][c:n, 0:n] * mask_incl, big[u][c:n, n:2 * n] * mask_incl], axis=1)
            for u in units:
                mv[u] = _dotp(m_uk[u], vst[u], _NN, P_ST)
            yield
            tt = {u: eye + m_ub[u] * lvl_ref[0] for u in units}
            for lv in range(1, levels):
                w = {u: _dot1(m_ub[u] * lvl_ref[lv], stack_t(tt[u])) for u in units}
                yield
                tt = {u: tt[u] + _dot1(tt[u], stack_t(w[u])) for u in units}
                yield
            t.update(tt)

        def tail_independent(js):
            units = [(j, p) for j in js for p in pairs]
            zero = jnp.zeros((RW_N, RW_N), F32)
            s0 = {(j, p): jnp.concatenate([jnp.concatenate([s_in[ci[j], p, 0], zero], axis=1),
                                           jnp.concatenate([zero, s_in[ci[j], p, 1]], axis=1)], axis=0)
                  for (j, p) in units}
            urs = {u: _dotp(ur[u], s0[u], _NT, P_ST) for u in units}
            yield
            e = {u: _dotp(t[u], stack_k(urs[u][0:c] + mv[u]), _NN, P_ST) for u in units}
            yield
            ev = {u: jnp.concatenate([stack_k(e[u]), vst[u]], axis=0) for u in units}
            upd = {u: _dotp(ev[u], bk_end[u], _TN, P_ST) for u in units}
            yield
            for (j, p) in units:
                y_ref[p, rs[j], :] = urs[j, p][c:n] + _dotp(a_r[j, p], ev[j, p], _NN, P_ST)
                s_new = s0[j, p] * jnp.exp(ltot[j, p]) + upd[j, p]
                s_out[ci[j], p, 0] = s_new[0:RW_N, 0:RW_N]
                s_out[ci[j], p, 1] = s_new[RW_N:, RW_N:]
            yield

        def tail(js):
            if sample:
                yield from tail_independent(js)
                return
            emit_y = None
            for j in js:
                sj = state["s"]
                urs = [_dotp(ur[j, p], sj[p], _NT, P_ST) for p in pairs]
                if emit_y is not None:
                    emit_y()
                yield
                e = [_dotp(t[j, p], stack_k(urs[p][0:c] + mv[j, p]), _NN, P_ST) for p in pairs]
                yield
                ev = [jnp.concatenate([stack_k(e[p]), vst[j, p]], axis=0) for p in pairs]
                upd = [_dotp(ev[p], bk_end[j, p], _TN, P_ST) for p in pairs]
                state["s"] = s_new = [sj[p] * jnp.exp(ltot[j, p]) + upd[p] for p in pairs]
                yield

                def emit_y(j=j, urs=urs, ev=ev):
                    for p in pairs:
                        y_ref[p, rs[j], :] = urs[p][c:n] + _dotp(a_r[j, p], ev[p], _NN, P_ST)
            emit_y()
            yield

        def run(*gens):
            live = list(gens)
            while live:
                for g in list(live):
                    try:
                        next(g)
                    except StopIteration:
                        live.remove(g)

        waves = [list(range(w, min(w + wave, nu))) for w in range(0, nu, wave)]
        run(head(waves[0]))
        for w_prev, w_next in zip(waves[:-1], waves[1:]):
            run(head(w_next), tail(w_prev))
        run(tail(waves[-1]))
        if not sample:
            for p in pairs:
                s_scr[p] = state["s"][p]
        return carry

    lax.fori_loop(0, n_chunks // nu, group_body, 0)

    if not sample:
        @pl.when(pl.program_id(0) == pl.num_programs(0) - 1)
        def _():
            s_out[...] = s_scr[...]


def _rw_rec(seqs, s_bd, cfg, sample):
    npair, n, _ = seqs[0].shape
    tm = TM if sample else TM_RW
    ng, npt = tm // cfg["ts"], cfg["npt"] * TM // tm
    c = cfg["ts"] if sample else CHUNK
    tri = jnp.asarray(np.tril(np.ones((c, c))), BF16)
    side = lambda a: jnp.asarray(np.concatenate([a, a], axis=1), F32)
    mask_incl = side(np.tril(np.ones((c, c))))
    mask_strict = side(np.tril(np.ones((c, c)), -1))
    eye = side(np.eye(c))
    tt, ss = np.meshgrid(np.arange(c), np.arange(c), indexing="ij")
    lvl = jnp.stack([side(((tt // b == ss // b) & (tt % b >= b // 2) & (ss % b < b // 2)).astype(np.float32))
                     for b in (2 ** e for e in range(1, int(np.log2(c)) + 1))])
    consts = [tri, mask_strict, mask_incl, eye, lvl]
    const_specs = [_const_spec(a.shape, (0,) * a.ndim) for a in consts]
    kern = functools.partial(_rw_rec_kernel, c=c, sample=sample)
    scratch = [pltpu.VMEM((npair, LANES, LANES), F32)]
    tokp = pl.BlockSpec((npair, tm, LANES), lambda i: (0, i, 0))
    if sample:
        nb = s_bd.shape[0]
        nst = nb // ng
        tok_in = pl.BlockSpec((npair, tm, LANES), lambda i: (0, npt + i, 0))
        sspec = pl.BlockSpec((ng, npair, 2, RW_N, RW_N), lambda i: (i, 0, 0, 0, 0))
        return pl.pallas_call(
            kern,
            grid=(nst,),
            in_specs=[tok_in] * 6 + const_specs + [sspec],
            out_specs=[tokp, sspec],
            out_shape=[jax.ShapeDtypeStruct((npair, nst * tm, LANES), F32),
                       jax.ShapeDtypeStruct((nb, npair, 2, RW_N, RW_N), F32)],
            scratch_shapes=scratch,
            compiler_params=_params(("arbitrary",)),
            name="rw_rec_sample",
        )(*seqs, *consts, s_bd)
    return pl.pallas_call(
        kern,
        grid=(npt,),
        in_specs=[tokp] * 6 + const_specs,
        out_specs=[tokp, pl.BlockSpec((npair, LANES, LANES), lambda i: (0, 0, 0))],
        out_shape=[jax.ShapeDtypeStruct((npair, npt * tm, LANES), F32),
                   jax.ShapeDtypeStruct((npair, LANES, LANES), F32)],
        scratch_shapes=scratch,
        compiler_params=_params(("arbitrary",)),
        name="rw_rec_prompt",
    )(*seqs, *consts)


def kernel(x_prompt, x_sample, state_ret, state_hgrn, state_wkv, state_shift, c_prompt, c_sample, mod_w, mod_b, norm_mix_g, norm_mlp_g, final_g, mlp_w1, mlp_w2, ab_w_in, ab_w_out, hg_lb, hg_norm_g, rw_mu, rw_w_rkv, rw_w0, rw_w1, rw_w2, rw_a0, rw_a1, rw_a2, rw_v0, rw_v1, rw_v2, rw_g1, rw_g2, rw_k_k, rw_k_a, rw_r_k, rw_ln_g, rw_ln_b, rw_w_out):
    bp, tp, d = x_prompt.shape
    bs, ts, _ = x_sample.shape
    depth = mod_w.shape[0]
    n_ab = ab_w_in.shape[0]
    n_c = rw_w_rkv.shape[0]
    assert bp == 1 and d == 1024 and tp % TM == 0 and tp % CHUNK == 0
    assert (bs * ts) % TM == 0 and TM % ts == 0 and ts % SUB == 0 and ts <= CHUNK
    np_rows, ns_rows = bp * tp, bs * ts
    n = np_rows + ns_rows
    ng = TM // ts
    assert bs % ng == 0
    assert np_rows % TM_RW == 0 and ns_rows % TM_RW == 0 and TM_RW % TM == 0
    assert np_rows % TM_IN == 0 and ns_rows % TM_IN == 0 and TM_IN % TM == 0 and bs % (TM_IN // ts) == 0
    ng_max = TM_IN // ts
    cfg = dict(ng=ng, npt=np_rows // TM, ts=ts, n=n, mod_off=ng_max // ng)
    dk, dv = d // 16, d // 8
    npair = d // LANES
    nh = d // RW_N

    c_all = jnp.concatenate([c_prompt, c_sample], axis=0)
    pad = (-c_all.shape[0]) % 8
    c_all = jnp.pad(c_all, ((0, pad), (0, 0)))
    mod = _modulation(c_all, mod_w, mod_b)
    modt = jnp.concatenate([jnp.broadcast_to(mod[:, 0:1], (depth, ng_max, 6 * d)), mod[:, bp:bp + bs]], axis=1)
    modt = modt.reshape(depth, ng_max + bs, 1, 6 * d)

    x = (x_prompt.reshape(np_rows, d), x_sample.reshape(ns_rows, d))

    pos = jnp.concatenate([jnp.arange(tp, dtype=F32), jnp.tile(PAST_LEN + jnp.arange(ts, dtype=F32), bs)])
    half = dk // 2
    inv = ROPE_BASE ** (-jnp.arange(half, dtype=F32) / half)
    ang = pos[:, None] * inv[None, :]
    cos_t = jnp.tile(jnp.cos(ang), (1, 4))
    sin_t = jnp.tile(jnp.concatenate([-jnp.sin(ang), jnp.sin(ang)], axis=1), (1, 2))

    lb_all = jnp.cumsum(jax.nn.softmax(hg_lb.astype(F32), axis=0), axis=0)
    lb_all = lb_all - lb_all[:1]
    lbc = jnp.stack([jnp.log(lb_all), jnp.log1p(-lb_all), 1.0 - lb_all], axis=1)
    lbc = jnp.pad(lbc, ((0, 0), (0, 5), (0, 0))).reshape(n_ab * 8, d // 2)

    bf = lambda a: a.astype(BF16)
    ab_w_in_b, ab_w_out_b = bf(ab_w_in), bf(ab_w_out)
    mlp_w1_b, mlp_w2_b = bf(mlp_w1), bf(mlp_w2)
    rw_w_out_b = bf(rw_w_out)
    vec = lambda a: a.reshape(a.shape[0], 1, d)
    W = dict(rw_mu=rw_mu, rw_w_rkv=bf(rw_w_rkv), rw_w0=vec(rw_w0), rw_w1=bf(rw_w1), rw_w2=bf(rw_w2),
             rw_a0=vec(rw_a0), rw_a1=bf(rw_a1), rw_a2=bf(rw_a2), rw_g1=bf(rw_g1), rw_g2=bf(rw_g2),
             rw_k_k=vec(rw_k_k), rw_k_a=vec(rw_k_a), rw_r_k=vec(rw_r_k), rw_v0=vec(rw_v0), rw_v1=bf(rw_v1),
             rw_v2=bf(rw_v2))
    seg = jnp.asarray(np.pad(np.kron(np.eye(nh), np.ones((RW_N, 1))), ((0, 0), (0, LANES - nh))), BF16)
    g_mix = norm_mix_g.reshape(depth, 1, d)
    g_mlp = norm_mlp_g.reshape(depth, 1, d)
    fin_g = final_g.reshape(1, d)

    ret_p, ret_s, hg_p, hg_s, wkv_p, wkv_s, sh_p, sh_s = ([] for _ in range(8))
    v_first = None
    for layer in range(depth):
        m = layer // 2
        final = layer == depth - 1
        if layer % 2 == 0:
            z = _ab_in(x, modt, g_mix, ab_w_in_b, cos_t, sin_t, lbc, layer, m, cfg)
            o_p, r_p, h_p = _ab_rec(z, hg_norm_g[m], None, None, cfg, False, d)
            o_s, r_s, h_s = _ab_rec(z, hg_norm_g[m], state_ret[m], jnp.swapaxes(state_hgrn[m], -1, -2), cfg, True, d)
            ret_p.append(r_p[None])
            hg_p.append(jnp.swapaxes(h_p, -1, -2)[None])
            ret_s.append(r_s)
            hg_s.append(jnp.swapaxes(h_s, -1, -2))
            x = _post(x, (o_p, o_s), modt, ab_w_out_b, g_mlp, mlp_w1_b, mlp_w2_b, fin_g, layer, m, cfg, False, final)
        else:
            shift_rows = state_shift[m].reshape(bs, 1, d)
            outs = _rw_in(x, modt, g_mix, shift_rows, W, seg, v_first, layer, m, cfg)
            seqs, gate, bonus, hlast = outs[0:6], outs[6], outs[7], outs[8]
            if v_first is None:
                v_first = outs[9]
            y_p, s_p = _rw_rec(seqs, None, cfg, False)
            y_s, s_s = _rw_rec(seqs, state_wkv[m].reshape(bs, npair, 2, RW_N, RW_N), cfg, True)
            unbd = lambda a: jnp.stack([a[..., :RW_N, :RW_N], a[..., RW_N:, RW_N:]], axis=-3)
            wkv_p.append(unbd(s_p).reshape(1, 1, nh, RW_N, RW_N))
            wkv_s.append(s_s.reshape(bs, nh, RW_N, RW_N))
            sh_p.append(hlast[cfg["npt"] * ng - 1][None, None])
            sh_s.append(hlast[cfg["npt"] * ng:][None])
            x = _post(x, (y_p, y_s, bonus, gate, vec(rw_ln_g), vec(rw_ln_b), seg), modt, rw_w_out_b, g_mlp,
                      mlp_w1_b, mlp_w2_b, fin_g, layer, m, cfg, True, final)

    out_p, out_s = x
    return (out_p.reshape(bp, tp, d), out_s.reshape(bs, ts, d),
            jnp.stack(ret_p), jnp.stack(ret_s), jnp.stack(hg_p), jnp.stack(hg_s),
            jnp.concatenate(wkv_p, axis=0), jnp.stack(wkv_s),
            jnp.concatenate(sh_p, axis=0), jnp.concatenate(sh_s, axis=0))
```

```python
import functools

import numpy as np
import jax
import jax.numpy as jnp
from jax import lax
from jax.experimental import pallas as pl
from jax.experimental.pallas import tpu as pltpu

F32 = jnp.float32
BF16 = jnp.bfloat16

CHUNK = 64
PAST_LEN = 2048
ROPE_BASE = 10000.0
RET_HEADS = 4
HG_HEADS = 4
RW_N = 64
NORM_EPS = 1e-6
RW_LN_EPS = 64e-5

LANES = 128
SUB = 64
LEAF = 8
RW_GROUP = 2
RW_WAVE = 2
RW_GROUP_SAMPLE = 4
TM_RW = 256
ROW_BLOCK = 256
TM_IN = 512
VMEM_LIMIT_IN = 60 * 1024 * 1024
TM = 256
VMEM_LIMIT = 56 * 1024 * 1024


def _sigmoid(x):
    return 1.0 / (1.0 + jnp.exp(-x))


def _split2(x):
    hi = x.astype(BF16)
    lo = (x - hi.astype(F32)).astype(BF16)
    return hi, lo


def _split3(x):
    a = x.astype(BF16)
    r = x - a.astype(F32)
    b = r.astype(BF16)
    c = (r - b.astype(F32)).astype(BF16)
    return a, b, c


_NN = (((1,), (0,)), ((), ()))
_NT = (((1,), (1,)), ((), ()))
_TN = (((0,), (0,)), ((), ()))


def _dg(a, b, dims):
    return lax.dot_general(a, b, dims, preferred_element_type=F32)


def _dot1(a, b, dims=_NN):
    return _dg(a.astype(BF16), b.astype(BF16), dims)


def _dot3(a, b, dims=_NN):
    ah, al = _split2(a)
    bh, bl = _split2(b)
    return _dg(ah, bh, dims) + _dg(ah, bl, dims) + _dg(al, bh, dims)


def _dotp(a, b, dims, passes):
    return _dot1(a, b, dims) if passes == 1 else _dot3(a, b, dims)


P_BIG = 1
P_ST = 1


def _cumsum_rows(tri_bf16, x):
    a, b, c = _split3(x)
    return _dg(tri_bf16, a, _NN) + _dg(tri_bf16, b, _NN) + _dg(tri_bf16, c, _NN)


def _segsum(x, seg_bf16):
    per_head = _dg(x.astype(BF16), seg_bf16, _NN)
    return _dg(per_head.astype(BF16), seg_bf16, _NT)


def _rms(x, g):
    ms = jnp.mean(x * x, axis=-1, keepdims=True)
    return x * lax.rsqrt(ms + NORM_EPS) * g


def _modulate(y, sc, sh):
    rows, d = y.shape
    ng = sc.shape[0]
    y3 = y.reshape(ng, rows // ng, d)
    return (y3 * (1.0 + sc) + sh).reshape(rows, d)


def _gate_add(x, gt, out):
    rows, d = x.shape
    ng = gt.shape[0]
    return (x.reshape(ng, rows // ng, d) + gt * out.reshape(ng, rows // ng, d)).reshape(rows, d)


def _rows_specs(x, npt):
    if isinstance(x, tuple):
        d = x[0].shape[1]
        return list(x), [pl.BlockSpec((TM, d), lambda i: (jnp.minimum(i, npt - 1), 0)),
                         pl.BlockSpec((TM, d), lambda i: (jnp.maximum(i - npt, 0), 0))]
    return [x], [pl.BlockSpec((TM, x.shape[1]), lambda i: (i, 0))]


def _rows_value(refs, npt):
    if len(refs) == 1:
        return refs[0][...]
    p_ref, s_ref = refs
    rows, d = p_ref.shape
    s_i = jnp.where(pl.program_id(0) >= npt, 1, 0)
    is_sample = (lax.broadcasted_iota(jnp.int32, (rows, LANES), 0) * 0 + s_i) == 1
    return jnp.concatenate([jnp.where(is_sample, s_ref[:, LANES * p:LANES * (p + 1)],
                                      p_ref[:, LANES * p:LANES * (p + 1)]) for p in range(d // LANES)], axis=-1)


def _const_spec(block, index):
    return pl.BlockSpec(block, lambda *_: index, pipeline_mode=pl.Buffered(1))


def _params(sem):
    return pltpu.CompilerParams(dimension_semantics=sem, vmem_limit_bytes=VMEM_LIMIT)


def _mod_kernel(c_ref, w_ref, b_ref, o_ref):
    c = c_ref[...]
    s = c * _sigmoid(c)
    o_ref[0] = _dot1(s, w_ref[0]) + b_ref[0]


def _modulation(c_all, mod_w, mod_b):
    depth, d, d6 = mod_w.shape
    rows = c_all.shape[0]
    nt = d6 // d
    return pl.pallas_call(
        _mod_kernel,
        grid=(depth, nt),
        in_specs=[
            pl.BlockSpec((rows, d), lambda l, j: (0, 0)),
            pl.BlockSpec((1, d, d), lambda l, j: (l, 0, j)),
            pl.BlockSpec((1, 1, d), lambda l, j: (l, 0, j)),
        ],
        out_specs=pl.BlockSpec((1, rows, d), lambda l, j: (l, 0, j)),
        out_shape=jax.ShapeDtypeStruct((depth, rows, d6), F32),
        compiler_params=_params(("arbitrary", "arbitrary")),
        name="modulation",
    )(c_all, mod_w, mod_b.reshape(depth, 1, d6))


def _ab_in_kernel(*refs, d, npt, dk_scale):
    mod_ref, g_ref, w_ref, cos_ref, sin_ref, lb_ref, o_ref = refs[-7:]
    x = _rows_value(refs[:-7], npt)
    h = _modulate(_rms(x, g_ref[0]), mod_ref[0, :, :, d:2 * d], mod_ref[0, :, :, 0:d])
    hb = h.astype(BF16)
    hw = d // 2

    def proj(a):
        return _dg(hb, w_ref[0, :, a:a + hw], _NN)

    z = proj(4 * hw)
    log_lb = lb_ref[0:1, :]
    log1m_lb = lb_ref[1:2, :]
    ls = jnp.minimum(z, 0.0) - jnp.log1p(jnp.exp(-jnp.abs(z)))
    c = log1m_lb + ls
    m = jnp.maximum(log_lb, c)
    o_ref[:, 4 * hw:5 * hw] = m + jnp.log1p(jnp.exp(-jnp.abs(log_lb - c)))
    o_ref[:, 5 * hw:6 * hw] = c - z
    qk = proj(0)
    lane = lax.broadcasted_iota(jnp.int32, (x.shape[0], LANES), 1)
    first_half = (lane % 64) < 32
    cos = cos_ref[...]
    sin = sin_ref[...]
    for s in range(hw // LANES):
        xs = qk[:, LANES * s:LANES * (s + 1)]
        sw = jnp.where(first_half, pltpu.roll(xs, 96, 1), pltpu.roll(xs, 32, 1))
        rot = xs * cos + sw * sin
        if s < hw // (2 * LANES):
            rot = rot * dk_scale
        o_ref[:, LANES * s:LANES * (s + 1)] = rot
    o_ref[:, hw:2 * hw] = proj(hw)
    z = proj(2 * hw)
    o_ref[:, 2 * hw:3 * hw] = z * _sigmoid(z)
    z = proj(3 * hw)
    o_ref[:, 3 * hw:4 * hw] = z * _sigmoid(z)
    o_ref[:, 6 * hw:7 * hw] = proj(5 * hw)
    o_ref[:, 7 * hw:8 * hw] = _sigmoid(proj(6 * hw))


def _ab_in(x, modt, g, w_in, cos_t, sin_t, lbc, layer, m, cfg):
    ng, npt, n = cfg["ng"], cfg["npt"], cfg["n"]
    x_args, x_specs = _rows_specs(x, npt)
    d = x_args[0].shape[1]
    nt = n // TM
    zw = 4 * d
    kern = functools.partial(_ab_in_kernel, d=d, npt=npt, dk_scale=float((d // 16) ** -0.5))
    return pl.pallas_call(
        kern,
        grid=(nt,),
        in_specs=x_specs + [
            pl.BlockSpec((1, ng, 1, 6 * d), lambda i: (layer, jnp.maximum(i - npt + cfg["mod_off"], 0), 0, 0)),
            _const_spec((1, 1, d), (layer, 0, 0)),
            _const_spec((1, d, w_in.shape[2]), (m, 0, 0)),
            pl.BlockSpec((TM, LANES), lambda i: (i, 0)),
            pl.BlockSpec((TM, LANES), lambda i: (i, 0)),
            _const_spec((8, d // 2), (m, 0)),
        ],
        out_specs=pl.BlockSpec((TM, zw), lambda i: (i, 0)),
        out_shape=jax.ShapeDtypeStruct((n, zw), F32),
        compiler_params=_params(("arbitrary",)),
        name="ab_in",
    )(*x_args, modt, g, w_in, cos_t, sin_t, lbc)


def _ab_rec_kernel(*refs, d, c, sample):
    if sample:
        (z_ref, dmat_ref, rdec_ref, cdec_ref, gc_ref, tri_ref, hgg_ref, ones_ref, sret_in, shg_in,
         o_ref, sret_out, shg_out, sret_scr, shg_scr) = refs
    else:
        (z_ref, dmat_ref, rdec_ref, cdec_ref, gc_ref, tri_ref, hgg_ref, ones_ref,
         o_ref, sret_out, shg_out, sret_scr, shg_scr) = refs
    hw = d // 2
    rows = z_ref.shape[0]
    n_chunks = rows // c
    dk = d // 16

    if not sample:
        @pl.when(pl.program_id(0) == 0)
        def _():
            sret_scr[...] = jnp.zeros_like(sret_scr)
            shg_scr[...] = jnp.zeros_like(shg_scr)

    lane_c = lax.broadcasted_iota(jnp.int32, (c, LANES), 1)
    row_h = lax.broadcasted_iota(jnp.int32, (LEAF, LANES), 0)
    ones = ones_ref[...]
    tri = tri_ref[...]

    def hg_chunk(r0, b):
        sub = min(SUB, c)
        col = lambda g, h: slice(g * hw + LANES * h, g * hw + LANES * (h + 1))
        units = [(i, h) for i in range(c // sub) for h in range(HG_HEADS)]
        rs = {i: pl.ds(r0 + sub * i, sub) for i in range(c // sub)}
        q = {u: z_ref[rs[u[0]], col(3, u[1])] for u in units}
        v = {u: z_ref[rs[u[0]], col(6, u[1])] for u in units}
        bs = {(i, h): b[sub * i:sub * (i + 1), LANES * h:LANES * (h + 1)] for (i, h) in units}
        bl = {u: bs[u][sub - 1:sub, :] for u in units}
        w = {u: bs[u] - z_ref[rs[u[0]], col(5, u[1])] for u in units}
        nl = sub // LEAF
        r, cross = {}, {}
        for u in units:
            ps = []
            for j in range(sub):
                lf = slice(LEAF * (j // LEAF), LEAF * (j // LEAF + 1))
                ps.append(jnp.where(row_h >= j % LEAF, q[u][lf] * jnp.exp(bs[u][lf] - w[u][j:j + 1, :]), 0.0))
            r[u] = _dot1(jnp.concatenate(ps, axis=0), ones)
            for a in range(1, nl):
                lf = slice(LEAF * a, LEAF * (a + 1))
                mid = bs[u][LEAF * a - 1:LEAF * a, :]
                att = _dot1(q[u][lf] * jnp.exp(bs[u][lf] - mid), jnp.exp(mid - w[u][0:LEAF * a]), _NT)
                cross[u, a] = _dot1(att, v[u][0:LEAF * a])
        upd = {u: _dot1(v[u], jnp.exp(bl[u] - w[u]), _TN) for u in units}
        inner = {}
        for u in units:
            leaves = []
            for a in range(nl):
                acc = cross[u, a] if a else None
                for j in range(LEAF * a, LEAF * (a + 1)):
                    term = r[u][LEAF * j:LEAF * (j + 1)] * v[u][j:j + 1, :]
                    acc = term if acc is None else acc + term
                leaves.append(acc)
            inner[u] = jnp.concatenate(leaves, axis=0)
        st = {h: shg_scr[h] for h in range(HG_HEADS)}
        for u in units:
            i, h = u
            out = inner[u] + _dot1(q[u] * jnp.exp(bs[u]), st[h], _NT)
            st[h] = st[h] * jnp.exp(bl[u]) + upd[u]
            ms = jnp.mean(out * out, axis=-1, keepdims=True)
            o_ref[rs[i], col(1, h)] = out * lax.rsqrt(ms + NORM_EPS) * hgg_ref[...] * z_ref[rs[i], col(7, h)]
        for h in range(HG_HEADS):
            shg_scr[h] = st[h]

    def ret_chunk(r0):
        rs = pl.ds(r0, c)
        heads = range(RET_HEADS)
        zeros = jnp.zeros((dk, LANES), F32)
        qp = [z_ref[rs, LANES * p:LANES * (p + 1)] for p in range(RET_HEADS // 2)]
        kp = [z_ref[rs, hw // 2 + LANES * p:hw // 2 + LANES * (p + 1)] for p in range(RET_HEADS // 2)]
        v = [z_ref[rs, hw + LANES * h:hw + LANES * (h + 1)] for h in heads]
        s = [sret_scr[h] for h in heads]
        qh = [jnp.where((lane_c // dk) == (h % 2), qp[h // 2], 0.0) for h in heads]
        scores = [_dot1(qh[h], kp[h // 2], _NT) * dmat_ref[h] for h in heads]
        spad = [jnp.concatenate([s[h], zeros] if h % 2 == 0 else [zeros, s[h]], axis=0) for h in heads]
        cross = [_dot1(qp[h // 2], spad[h]) * rdec_ref[h] for h in heads]
        u = [_dot1(kp[h // 2] * cdec_ref[h], v[h], _TN) for h in heads]
        inner = [_dot1(scores[h], v[h]) for h in heads]
        for h in heads:
            e = h % 2
            out = inner[h] + cross[h]
            sret_scr[h] = s[h] * gc_ref[h] + u[h][dk * e:dk * (e + 1)]
            ms = jnp.mean(out * out, axis=-1, keepdims=True)
            o_ref[rs, LANES * h:LANES * (h + 1)] = (out * lax.rsqrt(ms + NORM_EPS)
                                                    * z_ref[rs, 2 * hw + LANES * h:2 * hw + LANES * (h + 1)])

    def chunk_body(ci, carry):
        r0 = pl.multiple_of(ci * c, c)
        if sample:
            sret_scr[...] = sret_in[ci]
            shg_scr[...] = shg_in[ci]
        b = _cumsum_rows(tri, z_ref[pl.ds(r0, c), 4 * hw:5 * hw])
        ret_chunk(r0)
        hg_chunk(r0, b)
        if sample:
            sret_out[ci] = sret_scr[...]
            shg_out[ci] = shg_scr[...]
        return carry

    lax.fori_loop(0, n_chunks, chunk_body, 0, unroll=4)

    if not sample:
        @pl.when(pl.program_id(0) == pl.num_programs(0) - 1)
        def _():
            sret_out[...] = sret_scr[...]
            shg_out[...] = shg_scr[...]


def _ret_consts(c):
    heads = jnp.arange(RET_HEADS, dtype=F32)
    log_g = jnp.log1p(-jnp.exp2(-5.0 - heads))
    idx = jnp.arange(c, dtype=F32)
    rel = idx[:, None] - idx[None, :]
    dmat = jnp.exp(jnp.where(rel[None] >= 0, rel[None] * log_g[:, None, None], -jnp.inf))
    rdec = jnp.exp((idx[None, :] + 1.0) * log_g[:, None])
    cdec = jnp.exp((c - 1.0 - idx)[None, :] * log_g[:, None])
    gc = jnp.exp(c * log_g)
    bc = lambda a: jnp.broadcast_to(a[:, :, None], (RET_HEADS, c, LANES))
    return dmat, bc(rdec), bc(cdec), gc


def _ab_rec(z, hg_g, sret, shg_t, cfg, sample, d):
    n = z.shape[0]
    ng, npt = cfg["ng"], cfg["npt"]
    c = cfg["ts"] if sample else CHUNK
    dk, dv = d // 16, d // 8
    dmat, rdec, cdec, gc = _ret_consts(c)
    gcb = jnp.broadcast_to(gc[:, None, None], (RET_HEADS, dk, LANES))
    sub = min(SUB, c)
    tri = np.kron(np.eye(c // sub), np.tril(np.ones((sub, sub))))
    tri = jnp.asarray(tri, BF16)
    ones = jnp.ones((LANES, LANES), BF16)
    kern = functools.partial(_ab_rec_kernel, d=d, c=c, sample=sample)
    consts = [dmat, rdec, cdec, gcb, tri, hg_g.reshape(1, dv), ones]
    const_specs = [
        _const_spec(dmat.shape, (0, 0, 0)), _const_spec(rdec.shape, (0, 0, 0)),
        _const_spec(cdec.shape, (0, 0, 0)), _const_spec(gcb.shape, (0, 0, 0)),
        _const_spec(tri.shape, (0, 0)), _const_spec((1, dv), (0, 0)), _const_spec(ones.shape, (0, 0)),
    ]
    scratch = [pltpu.VMEM((RET_HEADS, dk, dv), F32), pltpu.VMEM((HG_HEADS, dv, dv), F32)]
    if sample:
        nb = sret.shape[0]
        nst = nb // ng
        return pl.pallas_call(
            kern,
            grid=(nst,),
            in_specs=[pl.BlockSpec((TM, z.shape[1]), lambda i: (npt + i, 0))] + const_specs + [
                pl.BlockSpec((ng, RET_HEADS, dk, dv), lambda i: (i, 0, 0, 0)),
                pl.BlockSpec((ng, HG_HEADS, dv, dv), lambda i: (i, 0, 0, 0)),
            ],
            out_specs=[
                pl.BlockSpec((TM, d), lambda i: (i, 0)),
                pl.BlockSpec((ng, RET_HEADS, dk, dv), lambda i: (i, 0, 0, 0)),
                pl.BlockSpec((ng, HG_HEADS, dv, dv), lambda i: (i, 0, 0, 0)),
            ],
            out_shape=[
                jax.ShapeDtypeStruct((nst * TM, d), F32),
                jax.ShapeDtypeStruct((nb, RET_HEADS, dk, dv), F32),
                jax.ShapeDtypeStruct((nb, HG_HEADS, dv, dv), F32),
            ],
            scratch_shapes=scratch,
            compiler_params=_params(("arbitrary",)),
            name="ab_rec_sample",
        )(z, *consts, sret, shg_t)
    return pl.pallas_call(
        kern,
        grid=(npt,),
        in_specs=[pl.BlockSpec((TM, z.shape[1]), lambda i: (i, 0))] + const_specs,
        out_specs=[
            pl.BlockSpec((TM, d), lambda i: (i, 0)),
            pl.BlockSpec((RET_HEADS, dk, dv), lambda i: (0, 0, 0)),
            pl.BlockSpec((HG_HEADS, dv, dv), lambda i: (0, 0, 0)),
        ],
        out_shape=[
            jax.ShapeDtypeStruct((npt * TM, d), F32),
            jax.ShapeDtypeStruct((RET_HEADS, dk, dv), F32),
            jax.ShapeDtypeStruct((HG_HEADS, dv, dv), F32),
        ],
        scratch_shapes=scratch,
        compiler_params=_params(("arbitrary",)),
        name="ab_rec_prompt",
    )(z, *consts)


def _post_kernel(*refs, d, npt, nx, rwkv, final):
    x = _rows_value(refs[:nx], npt)
    it = iter(refs[nx:])
    if rwkv:
        yp_ref, ys_ref, bonus_ref, gate_ref, lng_ref, lnb_ref, seg_ref = (next(it) for _ in range(7))
    else:
        op_ref, os_ref = next(it), next(it)
    mod_ref, wout_ref, g2_ref, w1_ref, w2_ref = (next(it) for _ in range(5))
    if final:
        fg_ref = next(it)
        outp_ref, outs_ref = next(it), next(it)
    else:
        out_ref = next(it)

    i = pl.program_id(0)
    if rwkv:
        s_i = jnp.where(i >= npt, 1, 0)
        is_sample = (lax.broadcasted_iota(jnp.int32, (x.shape[0], LANES), 0) * 0 + s_i) == 1
        y = jnp.concatenate([jnp.where(is_sample, ys_ref[p], yp_ref[p]) for p in range(yp_ref.shape[0])],
                            axis=-1)
        seg = seg_ref[...]
        inv_n = 1.0 / RW_N
        mean = _segsum(y, seg) * inv_n
        yc = y - mean
        var = _segsum(yc * yc, seg) * inv_n
        o = yc * lax.rsqrt(var + RW_LN_EPS) * lng_ref[0] + lnb_ref[0]
        o = (o + bonus_ref[...]) * gate_ref[...]
    else:
        o = _rows_value((op_ref, os_ref), npt)
    gt1 = mod_ref[0, :, :, 2 * d:3 * d]
    sh2 = mod_ref[0, :, :, 3 * d:4 * d]
    sc2 = mod_ref[0, :, :, 4 * d:5 * d]
    gt2 = mod_ref[0, :, :, 5 * d:6 * d]
    x1 = _gate_add(x, gt1, _dg(o.astype(BF16), wout_ref[0], _NN))
    hb = _modulate(_rms(x1, g2_ref[0]), sc2, sh2).astype(BF16)
    ff = w1_ref.shape[2]
    acc = jnp.zeros_like(x1)
    for j in range(ff // d):
        u = jnp.maximum(_dg(hb, w1_ref[0, :, j * d:(j + 1) * d], _NN), 0.0)
        acc = acc + _dg((u * u).astype(BF16), w2_ref[0, j * d:(j + 1) * d, :], _NN)
    x2 = _gate_add(x1, gt2, acc)
    if final:
        x2 = _rms(x2, fg_ref[...])

        @pl.when(i < npt)
        def _():
            outp_ref[...] = x2

        @pl.when(i >= npt)
        def _():
            outs_ref[...] = x2
    else:
        out_ref[...] = x2


def _post(x, mixer_in, modt, wout, g2, w1, w2, final_g, layer, m, cfg, rwkv, final):
    ng, npt, n = cfg["ng"], cfg["npt"], cfg["n"]
    args, specs = _rows_specs(x, npt)
    d = args[0].shape[1]
    nt = n // TM
    kern = functools.partial(_post_kernel, d=d, npt=npt, nx=len(args), rwkv=rwkv, final=final)
    tok = pl.BlockSpec((TM, d), lambda i: (i, 0))
    tok_p = pl.BlockSpec((TM, d), lambda i: (jnp.minimum(i, npt - 1), 0))
    tok_s = pl.BlockSpec((TM, d), lambda i: (jnp.maximum(i - npt, 0), 0))
    if rwkv:
        y_p, y_s, bonus, gate, lng, lnb, seg = mixer_in
        args += [y_p, y_s, bonus, gate, lng, lnb, seg]
        specs += [pl.BlockSpec((y_p.shape[0], TM, LANES), lambda i: (0, jnp.minimum(i, npt - 1), 0)),
                  pl.BlockSpec((y_s.shape[0], TM, LANES), lambda i: (0, jnp.maximum(i - npt, 0), 0)), tok, tok,
                  _const_spec((1, 1, d), (m, 0, 0)), _const_spec((1, 1, d), (m, 0, 0)),
                  _const_spec((d, LANES), (0, 0))]
    else:
        args += list(mixer_in)
        specs += [tok_p, tok_s]
    args += [modt, wout, g2, w1, w2]
    specs += [
        pl.BlockSpec((1, ng, 1, 6 * d), lambda i: (layer, jnp.maximum(i - npt + cfg["mod_off"], 0), 0, 0)),
        _const_spec((1, d, d), (m, 0, 0)),
        _const_spec((1, 1, d), (layer, 0, 0)),
        _const_spec((1, d, w1.shape[2]), (layer, 0, 0)),
        _const_spec((1, w2.shape[1], d), (layer, 0, 0)),
    ]
    if final:
        args += [final_g]
        specs += [_const_spec((1, d), (0, 0))]
    if final:
        out_specs = [tok_p, tok_s]
        out_shape = [jax.ShapeDtypeStruct((npt * TM, d), F32), jax.ShapeDtypeStruct((n - npt * TM, d), F32)]
    else:
        out_specs = tok
        out_shape = jax.ShapeDtypeStruct((n, d), F32)
    return pl.pallas_call(
        kern,
        grid=(nt,),
        in_specs=specs,
        out_specs=out_specs,
        out_shape=out_shape,
        compiler_params=_params(("arbitrary",)),
        name="post_rwkv" if rwkv else "post_ab",
    )(*args)


def _rw_in_kernel(*refs, d, ts, npt, vres):
    it = iter(refs)
    x_ref, mod_ref, g_ref, shift_ref, mu_ref, wrkv_ref = (next(it) for _ in range(6))
    w0_ref, w1_ref, w2_ref, a0_ref, a1_ref, a2_ref, g1_ref, g2_ref = (next(it) for _ in range(8))
    kk_ref, ka_ref, rk_ref, seg_ref = (next(it) for _ in range(4))
    if vres:
        vf_ref, v0_ref, v1_ref, v2_ref = (next(it) for _ in range(4))
    r_out, ld_out, k_out, v_out, kk_out, kka_out, gate_out, bonus_out, hlast_out = (next(it) for _ in range(9))
    if not vres:
        vtok_out = next(it)
    h_scr, carry_scr = next(it), next(it)

    i = pl.program_id(0)
    rows = x_ref.shape[0]
    ng = rows // ts

    @pl.when(i == 0)
    def _():
        carry_scr[...] = jnp.zeros_like(carry_scr)

    x = x_ref[...]
    h = _modulate(_rms(x, g_ref[0]), mod_ref[0, :, :, d:2 * d], mod_ref[0, :, :, 0:d])
    for p in range(d // LANES):
        h_scr[p] = h[:, LANES * p:LANES * (p + 1)]
        hlast_out[:, LANES * p:LANES * (p + 1)] = h_scr[p, pl.ds(ts - 1, ng, stride=ts), :]
    rolled = pltpu.roll(h, 1, 0)
    row = lax.broadcasted_iota(jnp.int32, (rows, d), 0)
    s_i = jnp.where(i >= npt, 1, 0)
    first = ((row % ts) == 0) & ((row * (1 - s_i)) == 0)
    seq_prev = jnp.broadcast_to(shift_ref[...], (ng, ts, d)).reshape(rows, d)
    prev_first = jnp.where((row * 0 + s_i) == 1, seq_prev, jnp.broadcast_to(carry_scr[0:1, :], (rows, d)))
    prev = jnp.where(first, prev_first, rolled)
    carry_scr[0:1, :] = h[rows - 1:rows, :]

    seg = seg_ref[...]

    def block(b0):
        rb = slice(b0, b0 + ROW_BLOCK)
        hb = h[rb]
        xx = prev[rb] - hb
        mix = lambda j: (hb + xx * mu_ref[0, j:j + 1, :]).astype(BF16)
        xr, xw, xk, xv, xa, xg = (mix(j) for j in range(6))
        yield
        w_dn = _dg(xw, w1_ref[0], _NN)
        a_dn = _dg(xa, a1_ref[0], _NN)
        g_dn = _dg(xg, g1_ref[0], _NN)
        if vres:
            v_dn = _dg(xv, v1_ref[0], _NN)
        k = _dg(xk, wrkv_ref[0, 1], _NN)
        yield
        v = _dg(xv, wrkv_ref[0, 2], _NN)
        kk = k * kk_ref[0]
        yield
        r = _dg(xr, wrkv_ref[0, 0], _NN)
        yield
        kk_ss = _segsum(kk * kk, seg)
        wl = w0_ref[0] + _dot1(jnp.tanh(w_dn), w2_ref[0])
        a = _sigmoid(a0_ref[0] + _dot1(a_dn, a2_ref[0]))
        gate_out[rb, :] = _dot1(_sigmoid(g_dn), g2_ref[0])
        if vres:
            lv = v0_ref[0] + _dot1(v_dn, v2_ref[0])
            v = v + (vf_ref[rb, :] - v) * _sigmoid(lv)
        else:
            vtok_out[rb, :] = v
        yield
        nwl = -wl
        w = -(jnp.maximum(nwl, 0.0) + jnp.log1p(jnp.exp(-jnp.abs(nwl)))) - 0.5
        ld = -jnp.exp(w)
        kk = kk / jnp.maximum(jnp.sqrt(kk_ss), 1e-12)
        km = k * (1.0 + (a - 1.0) * ka_ref[0])
        bonus_out[rb, :] = _segsum(r * km * rk_ref[0], seg) * v
        kka = kk * a
        for p in range(d // LANES):
            cs = slice(LANES * p, LANES * (p + 1))
            r_out[p, rb, :] = r[:, cs]
            ld_out[p, rb, :] = ld[:, cs]
            k_out[p, rb, :] = km[:, cs]
            v_out[p, rb, :] = v[:, cs]
            kk_out[p, rb, :] = kk[:, cs]
            kka_out[p, rb, :] = kka[:, cs]
        yield

    live = []
    starts = list(range(0, rows, ROW_BLOCK))
    while starts or live:
        if starts:
            live.append(block(starts.pop(0)))
        for g in list(live):
            try:
                next(g)
            except StopIteration:
                live.remove(g)


def _rw_in(x, modt, g, shift_rows, W, seg, v_first, layer, m, cfg):
    n, d = x.shape
    ts, tm = cfg["ts"], TM_IN
    ng, npt = tm // ts, cfg["npt"] * TM // tm
    nt = n // tm
    npair = d // LANES
    vres = v_first is not None
    kern = functools.partial(_rw_in_kernel, d=d, ts=ts, npt=npt, vres=vres)
    tok = pl.BlockSpec((tm, d), lambda i: (i, 0))
    vec = lambda: _const_spec((1, 1, d), (m, 0, 0))
    lora = lambda a: _const_spec((1,) + a.shape[1:], (m, 0, 0))
    args = [x, modt, g, shift_rows, W["rw_mu"], W["rw_w_rkv"],
            W["rw_w0"], W["rw_w1"], W["rw_w2"], W["rw_a0"], W["rw_a1"], W["rw_a2"], W["rw_g1"], W["rw_g2"],
            W["rw_k_k"], W["rw_k_a"], W["rw_r_k"], seg]
    specs = [
        tok,
        pl.BlockSpec((1, ng, 1, 2 * d), lambda i: (layer, jnp.maximum(i - npt + 1, 0), 0, 0)),
        _const_spec((1, 1, d), (layer, 0, 0)),
        pl.BlockSpec((ng, 1, d), lambda i: (jnp.maximum(i - npt, 0), 0, 0)),
        _const_spec((1, 6, d), (m, 0, 0)),
        _const_spec((1, 3, d, d), (m, 0, 0, 0)),
        vec(), lora(W["rw_w1"]), lora(W["rw_w2"]), vec(), lora(W["rw_a1"]), lora(W["rw_a2"]),
        lora(W["rw_g1"]), lora(W["rw_g2"]), vec(), vec(), vec(), _const_spec((d, LANES), (0, 0)),
    ]
    if vres:
        args += [v_first, W["rw_v0"], W["rw_v1"], W["rw_v2"]]
        specs += [tok, _const_spec((1, 1, d), (m - 1, 0, 0)),
                  _const_spec((1,) + W["rw_v1"].shape[1:], (m - 1, 0, 0)),
                  _const_spec((1,) + W["rw_v2"].shape[1:], (m - 1, 0, 0))]
    pm = pl.BlockSpec((npair, tm, LANES), lambda i: (0, i, 0))
    pm_shape = jax.ShapeDtypeStruct((npair, n, LANES), F32)
    tok_shape = jax.ShapeDtypeStruct((n, d), F32)
    out_specs = [pm] * 6 + [tok, tok, pl.BlockSpec((ng, d), lambda i: (i, 0))]
    out_shape = [pm_shape] * 6 + [tok_shape, tok_shape, jax.ShapeDtypeStruct((nt * ng, d), F32)]
    if not vres:
        out_specs.append(tok)
        out_shape.append(tok_shape)
    return pl.pallas_call(
        kern,
        grid=(nt,),
        in_specs=specs,
        out_specs=out_specs,
        out_shape=out_shape,
        scratch_shapes=[pltpu.VMEM((npair, tm, LANES), F32), pltpu.VMEM((8, d), F32)],
        compiler_params=pltpu.CompilerParams(dimension_semantics=("arbitrary",), vmem_limit_bytes=VMEM_LIMIT_IN),
        name="rw_in",
    )(*args)


def _rw_rec_kernel(*refs, c, sample):
    if sample:
        (r_ref, ld_ref, k_ref, v_ref, kk_ref, kka_ref, tri_ref, ms_ref, mi_ref, eye_ref, lvl_ref, s_in,
         y_ref, s_out, s_scr) = refs
    else:
        (r_ref, ld_ref, k_ref, v_ref, kk_ref, kka_ref, tri_ref, ms_ref, mi_ref, eye_ref, lvl_ref,
         y_ref, s_out, s_scr) = refs
    npair, rows, _ = r_ref.shape
    n_chunks = rows // c
    n = 2 * c

    if not sample:
        @pl.when(pl.program_id(0) == 0)
        def _():
            s_scr[...] = jnp.zeros_like(s_scr)

    left_k = lax.broadcasted_iota(jnp.int32, (c, LANES), 1) < RW_N
    left_t = lax.broadcasted_iota(jnp.int32, (c, n), 1) < c
    tri = tri_ref[...]
    mask_strict = ms_ref[...]
    mask_incl = mi_ref[...]
    eye = eye_ref[...]
    levels = lvl_ref.shape[0]

    def stack(xv, left):
        return jnp.concatenate([jnp.where(left, xv, 0.0), jnp.where(left, 0.0, xv)], axis=0)

    stack_k = lambda xv: stack(xv, left_k)
    stack_t = lambda xv: stack(xv, left_t)
    pairs = range(npair)

    nu = min(RW_GROUP_SAMPLE if sample else RW_GROUP, n_chunks)
    wave = nu if sample else RW_WAVE

    def group_body(gi, carry):
        ci = {j: gi * nu + j for j in range(nu)}
        rs = {j: pl.ds(pl.multiple_of(ci[j] * c, c), c) for j in range(nu)}
        ur, bk_end, vst, ltot, a_r, t, mv = {}, {}, {}, {}, {}, {}, {}
        state = {"s": None if sample else [s_scr[p] for p in pairs]}

        def head(js):
            units = [(j, p) for j in js for p in pairs]
            bk = {}
            for j in js:
                ld_all = jnp.concatenate([ld_ref[p, rs[j], :] for p in pairs], axis=1)
                lc_all = _cumsum_rows(tri, ld_all)
                for p in pairs:
                    lc = lc_all[:, LANES * p:LANES * (p + 1)]
                    ld = ld_all[:, LANES * p:LANES * (p + 1)]
                    lt = lc[c - 1:c, :]
                    g_inv = jnp.exp(-lc)
                    g_end = jnp.exp(lt - lc)
                    k = k_ref[p, rs[j], :]
                    kka = kka_ref[p, rs[j], :]
                    ur[j, p] = jnp.concatenate([-kk_ref[p, rs[j], :] * jnp.exp(lc - ld),
                                                r_ref[p, rs[j], :] * jnp.exp(lc)], axis=0)
                    bk[j, p] = jnp.concatenate([stack_k(kka * g_inv), stack_k(k * g_inv)], axis=0)
                    bk_end[j, p] = jnp.concatenate([stack_k(kka * g_end), stack_k(k * g_end)], axis=0)
                    vst[j, p] = stack_k(v_ref[p, rs[j], :])
                    ltot[j, p] = lt
                yield
            big = {u: _dotp(ur[u], bk[u], _NT, P_BIG) for u in units}
            m_ub = {u: big[u][0:c, 0:n] * mask_strict for u in units}
            m_uk = {u: big[u][0:c, n:2 * n] * mask_strict for u in units}
            for u in units:
                a_r[u] = jnp.concatenate([big[u][c:n, 0:n] * mask_incl, big[u][c:n, n:2 * n] * mask_incl], axis=1)
            for u in units:
                mv[u] = _dotp(m_uk[u], vst[u], _NN, P_ST)
            yield
            tt = {u: eye + m_ub[u] * lvl_ref[0] for u in units}
            for lv in range(1, levels):
                w = {u: _dot1(m_ub[u] * lvl_ref[lv], stack_t(tt[u])) for u in units}
                yield
                tt = {u: tt[u] + _dot1(tt[u], stack_t(w[u])) for u in units}
                yield
            t.update(tt)

        def tail_independent(js):
            units = [(j, p) for j in js for p in pairs]
            zero = jnp.zeros((RW_N, RW_N), F32)
            s0 = {(j, p): jnp.concatenate([jnp.concatenate([s_in[ci[j], p, 0], zero], axis=1),
                                           jnp.concatenate([zero, s_in[ci[j], p, 1]], axis=1)], axis=0)
                  for (j, p) in units}
            urs = {u: _dotp(ur[u], s0[u], _NT, P_ST) for u in units}
            yield
            e = {u: _dotp(t[u], stack_k(urs[u][0:c] + mv[u]), _NN, P_ST) for u in units}
            yield
            ev = {u: jnp.concatenate([stack_k(e[u]), vst[u]], axis=0) for u in units}
            upd = {u: _dotp(ev[u], bk_end[u], _TN, P_ST) for u in units}
            yield
            for (j, p) in units:
                y_ref[p, rs[j], :] = urs[j, p][c:n] + _dotp(a_r[j, p], ev[j, p], _NN, P_ST)
                s_new = s0[j, p] * jnp.exp(ltot[j, p]) + upd[j, p]
                s_out[ci[j], p, 0] = s_new[0:RW_N, 0:RW_N]
                s_out[ci[j], p, 1] = s_new[RW_N:, RW_N:]
            yield

        def tail(js):
            if sample:
                yield from tail_independent(js)
                return
            emit_y = None
            for j in js:
                sj = state["s"]
                urs = [_dotp(ur[j, p], sj[p], _NT, P_ST) for p in pairs]
                if emit_y is not None:
                    emit_y()
                yield
                e = [_dotp(t[j, p], stack_k(urs[p][0:c] + mv[j, p]), _NN, P_ST) for p in pairs]
                yield
                ev = [jnp.concatenate([stack_k(e[p]), vst[j, p]], axis=0) for p in pairs]
                upd = [_dotp(ev[p], bk_end[j, p], _TN, P_ST) for p in pairs]
                state["s"] = s_new = [sj[p] * jnp.exp(ltot[j, p]) + upd[p] for p in pairs]
                yield

                def emit_y(j=j, urs=urs, ev=ev):
                    for p in pairs:
                        y_ref[p, rs[j], :] = urs[p][c:n] + _dotp(a_r[j, p], ev[p], _NN, P_ST)
            emit_y()
            yield

        def run(*gens):
            live = list(gens)
            while live:
                for g in list(live):
                    try:
                        next(g)
                    except StopIteration:
                        live.remove(g)

        waves = [list(range(w, min(w + wave, nu))) for w in range(0, nu, wave)]
        run(head(waves[0]))
        for w_prev, w_next in zip(waves[:-1], waves[1:]):
            run(head(w_next), tail(w_prev))
        run(tail(waves[-1]))
        if not sample:
            for p in pairs:
                s_scr[p] = state["s"][p]
        return carry

    lax.fori_loop(0, n_chunks // nu, group_body, 0)

    if not sample:
        @pl.when(pl.program_id(0) == pl.num_programs(0) - 1)
        def _():
            s_out[...] = s_scr[...]


def _rw_rec(seqs, s_bd, cfg, sample):
    npair, n, _ = seqs[0].shape
    tm = TM if sample else TM_RW
    ng, npt = tm // cfg["ts"], cfg["npt"] * TM // tm
    c = cfg["ts"] if sample else CHUNK
    tri = jnp.asarray(np.tril(np.ones((c, c))), BF16)
    side = lambda a: jnp.asarray(np.concatenate([a, a], axis=1), F32)
    mask_incl = side(np.tril(np.ones((c, c))))
    mask_strict = side(np.tril(np.ones((c, c)), -1))
    eye = side(np.eye(c))
    tt, ss = np.meshgrid(np.arange(c), np.arange(c), indexing="ij")
    lvl = jnp.stack([side(((tt // b == ss // b) & (tt % b >= b // 2) & (ss % b < b // 2)).astype(np.float32))
                     for b in (2 ** e for e in range(1, int(np.log2(c)) + 1))])
    consts = [tri, mask_strict, mask_incl, eye, lvl]
    const_specs = [_const_spec(a.shape, (0,) * a.ndim) for a in consts]
    kern = functools.partial(_rw_rec_kernel, c=c, sample=sample)
    scratch = [pltpu.VMEM((npair, LANES, LANES), F32)]
    tokp = pl.BlockSpec((npair, tm, LANES), lambda i: (0, i, 0))
    if sample:
        nb = s_bd.shape[0]
        nst = nb // ng
        tok_in = pl.BlockSpec((npair, tm, LANES), lambda i: (0, npt + i, 0))
        sspec = pl.BlockSpec((ng, npair, 2, RW_N, RW_N), lambda i: (i, 0, 0, 0, 0))
        return pl.pallas_call(
            kern,
            grid=(nst,),
            in_specs=[tok_in] * 6 + const_specs + [sspec],
            out_specs=[tokp, sspec],
            out_shape=[jax.ShapeDtypeStruct((npair, nst * tm, LANES), F32),
                       jax.ShapeDtypeStruct((nb, npair, 2, RW_N, RW_N), F32)],
            scratch_shapes=scratch,
            compiler_params=_params(("arbitrary",)),
            name="rw_rec_sample",
        )(*seqs, *consts, s_bd)
    return pl.pallas_call(
        kern,
        grid=(npt,),
        in_specs=[tokp] * 6 + const_specs,
        out_specs=[tokp, pl.BlockSpec((npair, LANES, LANES), lambda i: (0, 0, 0))],
        out_shape=[jax.ShapeDtypeStruct((npair, npt * tm, LANES), F32),
                   jax.ShapeDtypeStruct((npair, LANES, LANES), F32)],
        scratch_shapes=scratch,
        compiler_params=_params(("arbitrary",)),
        name="rw_rec_prompt",
    )(*seqs, *consts)


def kernel(x_prompt, x_sample, state_ret, state_hgrn, state_wkv, state_shift, c_prompt, c_sample, mod_w, mod_b, norm_mix_g, norm_mlp_g, final_g, mlp_w1, mlp_w2, ab_w_in, ab_w_out, hg_lb, hg_norm_g, rw_mu, rw_w_rkv, rw_w0, rw_w1, rw_w2, rw_a0, rw_a1, rw_a2, rw_v0, rw_v1, rw_v2, rw_g1, rw_g2, rw_k_k, rw_k_a, rw_r_k, rw_ln_g, rw_ln_b, rw_w_out):
    bp, tp, d = x_prompt.shape
    bs, ts, _ = x_sample.shape
    depth = mod_w.shape[0]
    n_ab = ab_w_in.shape[0]
    n_c = rw_w_rkv.shape[0]
    assert bp == 1 and d == 1024 and tp % TM == 0 and tp % CHUNK == 0
    assert (bs * ts) % TM == 0 and TM % ts == 0 and ts % LEAF == 0 and ts <= CHUNK
    np_rows, ns_rows = bp * tp, bs * ts
    n = np_rows + ns_rows
    ng = TM // ts
    assert bs % ng == 0
    assert np_rows % TM_RW == 0 and ns_rows % TM_RW == 0 and TM_RW % TM == 0
    assert np_rows % TM_IN == 0 and ns_rows % TM_IN == 0 and TM_IN % TM == 0 and bs % (TM_IN // ts) == 0
    ng_max = TM_IN // ts
    cfg = dict(ng=ng, npt=np_rows // TM, ts=ts, n=n, mod_off=ng_max // ng)
    dk, dv = d // 16, d // 8
    npair = d // LANES
    nh = d // RW_N

    c_all = jnp.concatenate([c_prompt, c_sample], axis=0)
    pad = (-c_all.shape[0]) % 8
    c_all = jnp.pad(c_all, ((0, pad), (0, 0)))
    mod = _modulation(c_all, mod_w, mod_b)
    modt = jnp.concatenate([jnp.broadcast_to(mod[:, 0:1], (depth, ng_max, 6 * d)), mod[:, bp:bp + bs]], axis=1)
    modt = modt.reshape(depth, ng_max + bs, 1, 6 * d)

    x = (x_prompt.reshape(np_rows, d), x_sample.reshape(ns_rows, d))

    pos = jnp.concatenate([jnp.arange(tp, dtype=F32), jnp.tile(PAST_LEN + jnp.arange(ts, dtype=F32), bs)])
    half = dk // 2
    inv = ROPE_BASE ** (-jnp.arange(half, dtype=F32) / half)
    ang = pos[:, None] * inv[None, :]
    cos_t = jnp.tile(jnp.cos(ang), (1, 4))
    sin_t = jnp.tile(jnp.concatenate([-jnp.sin(ang), jnp.sin(ang)], axis=1), (1, 2))

    lb_all = jnp.cumsum(jax.nn.softmax(hg_lb.astype(F32), axis=0), axis=0)
    lb_all = lb_all - lb_all[:1]
    lbc = jnp.stack([jnp.log(lb_all), jnp.log1p(-lb_all), 1.0 - lb_all], axis=1)
    lbc = jnp.pad(lbc, ((0, 0), (0, 5), (0, 0))).reshape(n_ab * 8, d // 2)

    bf = lambda a: a.astype(BF16)
    ab_w_in_b, ab_w_out_b = bf(ab_w_in), bf(ab_w_out)
    mlp_w1_b, mlp_w2_b = bf(mlp_w1), bf(mlp_w2)
    rw_w_out_b = bf(rw_w_out)
    vec = lambda a: a.reshape(a.shape[0], 1, d)
    W = dict(rw_mu=rw_mu, rw_w_rkv=bf(rw_w_rkv), rw_w0=vec(rw_w0), rw_w1=bf(rw_w1), rw_w2=bf(rw_w2),
             rw_a0=vec(rw_a0), rw_a1=bf(rw_a1), rw_a2=bf(rw_a2), rw_g1=bf(rw_g1), rw_g2=bf(rw_g2),
             rw_k_k=vec(rw_k_k), rw_k_a=vec(rw_k_a), rw_r_k=vec(rw_r_k), rw_v0=vec(rw_v0), rw_v1=bf(rw_v1),
             rw_v2=bf(rw_v2))
    seg = jnp.asarray(np.pad(np.kron(np.eye(nh), np.ones((RW_N, 1))), ((0, 0), (0, LANES - nh))), BF16)
    g_mix = norm_mix_g.reshape(depth, 1, d)
    g_mlp = norm_mlp_g.reshape(depth, 1, d)
    fin_g = final_g.reshape(1, d)

    ret_p, ret_s, hg_p, hg_s, wkv_p, wkv_s, sh_p, sh_s = ([] for _ in range(8))
    v_first = None
    for layer in range(depth):
        m = layer // 2
        final = layer == depth - 1
        if layer % 2 == 0:
            z = _ab_in(x, modt, g_mix, ab_w_in_b, cos_t, sin_t, lbc, layer, m, cfg)
            o_p, r_p, h_p = _ab_rec(z, hg_norm_g[m], None, None, cfg, False, d)
            o_s, r_s, h_s = _ab_rec(z, hg_norm_g[m], state_ret[m], jnp.swapaxes(state_hgrn[m], -1, -2), cfg, True, d)
            ret_p.append(r_p[None])
            hg_p.append(jnp.swapaxes(h_p, -1, -2)[None])
            ret_s.append(r_s)
            hg_s.append(jnp.swapaxes(h_s, -1, -2))
            x = _post(x, (o_p, o_s), modt, ab_w_out_b, g_mlp, mlp_w1_b, mlp_w2_b, fin_g, layer, m, cfg, False, final)
        else:
            shift_rows = state_shift[m].reshape(bs, 1, d)
            outs = _rw_in(x, modt, g_mix, shift_rows, W, seg, v_first, layer, m, cfg)
            seqs, gate, bonus, hlast = outs[0:6], outs[6], outs[7], outs[8]
            if v_first is None:
                v_first = outs[9]
            y_p, s_p = _rw_rec(seqs, None, cfg, False)
            y_s, s_s = _rw_rec(seqs, state_wkv[m].reshape(bs, npair, 2, RW_N, RW_N), cfg, True)
            unbd = lambda a: jnp.stack([a[..., :RW_N, :RW_N], a[..., RW_N:, RW_N:]], axis=-3)
            wkv_p.append(unbd(s_p).reshape(1, 1, nh, RW_N, RW_N))
            wkv_s.append(s_s.reshape(bs, nh, RW_N, RW_N))
            sh_p.append(hlast[cfg["npt"] * ng - 1][None, None])
            sh_s.append(hlast[cfg["npt"] * ng:][None])
            x = _post(x, (y_p, y_s, bonus, gate, vec(rw_ln_g), vec(rw_ln_b), seg), modt, rw_w_out_b, g_mlp,
                      mlp_w1_b, mlp_w2_b, fin_g, layer, m, cfg, True, final)

    out_p, out_s = x
    return (out_p.reshape(bp, tp, d), out_s.reshape(bs, ts, d),
            jnp.stack(ret_p), jnp.stack(ret_s), jnp.stack(hg_p), jnp.stack(hg_s),
            jnp.concatenate(wkv_p, axis=0), jnp.stack(wkv_s),
            jnp.concatenate(sh_p, axis=0), jnp.concatenate(sh_s, axis=0))
```

```python
import functools

import numpy as np
import jax
import jax.numpy as jnp
from jax import lax
from jax.experimental import pallas as pl
from jax.experimental.pallas import tpu as pltpu

F32 = jnp.float32
BF16 = jnp.bfloat16

CHUNK = 64
PAST_LEN = 2048
ROPE_BASE = 10000.0
RET_HEADS = 4
HG_HEADS = 4
RW_N = 64
NORM_EPS = 1e-6
RW_LN_EPS = 64e-5

LANES = 128
SUB = 64
LEAF = 8
RW_GROUP = 2
RW_WAVE = 2
RW_GROUP_SAMPLE = 4
TM_RW = 512
ROW_BLOCK = 256
TM_IN = 512
VMEM_LIMIT_IN = 60 * 1024 * 1024
TM = 256
VMEM_LIMIT = 56 * 1024 * 1024


def _sigmoid(x):
    return 1.0 / (1.0 + jnp.exp(-x))


def _split2(x):
    hi = x.astype(BF16)
    lo = (x - hi.astype(F32)).astype(BF16)
    return hi, lo


def _split3(x):
    a = x.astype(BF16)
    r = x - a.astype(F32)
    b = r.astype(BF16)
    c = (r - b.astype(F32)).astype(BF16)
    return a, b, c


_NN = (((1,), (0,)), ((), ()))
_NT = (((1,), (1,)), ((), ()))
_TN = (((0,), (0,)), ((), ()))


def _dg(a, b, dims):
    return lax.dot_general(a, b, dims, preferred_element_type=F32)


def _dot1(a, b, dims=_NN):
    return _dg(a.astype(BF16), b.astype(BF16), dims)


def _dot3(a, b, dims=_NN):
    ah, al = _split2(a)
    bh, bl = _split2(b)
    return _dg(ah, bh, dims) + _dg(ah, bl, dims) + _dg(al, bh, dims)


def _dotp(a, b, dims, passes):
    return _dot1(a, b, dims) if passes == 1 else _dot3(a, b, dims)


P_BIG = 1
P_ST = 1


def _cumsum_rows(tri_bf16, x):
    a, b, c = _split3(x)
    return _dg(tri_bf16, a, _NN) + _dg(tri_bf16, b, _NN) + _dg(tri_bf16, c, _NN)


def _segsum(x, seg_bf16):
    per_head = _dg(x.astype(BF16), seg_bf16, _NN)
    return _dg(per_head.astype(BF16), seg_bf16, _NT)


def _rms(x, g):
    ms = jnp.mean(x * x, axis=-1, keepdims=True)
    return x * lax.rsqrt(ms + NORM_EPS) * g


def _modulate(y, sc, sh):
    rows, d = y.shape
    ng = sc.shape[0]
    y3 = y.reshape(ng, rows // ng, d)
    return (y3 * (1.0 + sc) + sh).reshape(rows, d)


def _gate_add(x, gt, out):
    rows, d = x.shape
    ng = gt.shape[0]
    return (x.reshape(ng, rows // ng, d) + gt * out.reshape(ng, rows // ng, d)).reshape(rows, d)


def _rows_specs(x, npt):
    if isinstance(x, tuple):
        d = x[0].shape[1]
        return list(x), [pl.BlockSpec((TM, d), lambda i: (jnp.minimum(i, npt - 1), 0)),
                         pl.BlockSpec((TM, d), lambda i: (jnp.maximum(i - npt, 0), 0))]
    return [x], [pl.BlockSpec((TM, x.shape[1]), lambda i: (i, 0))]


def _rows_value(refs, npt):
    if len(refs) == 1:
        return refs[0][...]
    p_ref, s_ref = refs
    rows, d = p_ref.shape
    s_i = jnp.where(pl.program_id(0) >= npt, 1, 0)
    is_sample = (lax.broadcasted_iota(jnp.int32, (rows, LANES), 0) * 0 + s_i) == 1
    return jnp.concatenate([jnp.where(is_sample, s_ref[:, LANES * p:LANES * (p + 1)],
                                      p_ref[:, LANES * p:LANES * (p + 1)]) for p in range(d // LANES)], axis=-1)


def _const_spec(block, index):
    return pl.BlockSpec(block, lambda *_: index, pipeline_mode=pl.Buffered(1))


def _params(sem):
    return pltpu.CompilerParams(dimension_semantics=sem, vmem_limit_bytes=VMEM_LIMIT)


def _mod_kernel(c_ref, w_ref, b_ref, o_ref):
    c = c_ref[...]
    s = c * _sigmoid(c)
    o_ref[0] = _dot1(s, w_ref[0]) + b_ref[0]


def _modulation(c_all, mod_w, mod_b):
    depth, d, d6 = mod_w.shape
    rows = c_all.shape[0]
    nt = d6 // d
    return pl.pallas_call(
        _mod_kernel,
        grid=(depth, nt),
        in_specs=[
            pl.BlockSpec((rows, d), lambda l, j: (0, 0)),
            pl.BlockSpec((1, d, d), lambda l, j: (l, 0, j)),
            pl.BlockSpec((1, 1, d), lambda l, j: (l, 0, j)),
        ],
        out_specs=pl.BlockSpec((1, rows, d), lambda l, j: (l, 0, j)),
        out_shape=jax.ShapeDtypeStruct((depth, rows, d6), F32),
        compiler_params=_params(("arbitrary", "arbitrary")),
        name="modulation",
    )(c_all, mod_w, mod_b.reshape(depth, 1, d6))


def _ab_in_kernel(*refs, d, npt, dk_scale):
    mod_ref, g_ref, w_ref, cos_ref, sin_ref, lb_ref, o_ref = refs[-7:]
    x = _rows_value(refs[:-7], npt)
    h = _modulate(_rms(x, g_ref[0]), mod_ref[0, :, :, d:2 * d], mod_ref[0, :, :, 0:d])
    hb = h.astype(BF16)
    hw = d // 2

    def proj(a):
        return _dg(hb, w_ref[0, :, a:a + hw], _NN)

    z = proj(4 * hw)
    log_lb = lb_ref[0:1, :]
    log1m_lb = lb_ref[1:2, :]
    ls = jnp.minimum(z, 0.0) - jnp.log1p(jnp.exp(-jnp.abs(z)))
    c = log1m_lb + ls
    m = jnp.maximum(log_lb, c)
    o_ref[:, 4 * hw:5 * hw] = m + jnp.log1p(jnp.exp(-jnp.abs(log_lb - c)))
    o_ref[:, 5 * hw:6 * hw] = c - z
    qk = proj(0)
    lane = lax.broadcasted_iota(jnp.int32, (x.shape[0], LANES), 1)
    first_half = (lane % 64) < 32
    cos = cos_ref[...]
    sin = sin_ref[...]
    for s in range(hw // LANES):
        xs = qk[:, LANES * s:LANES * (s + 1)]
        sw = jnp.where(first_half, pltpu.roll(xs, 96, 1), pltpu.roll(xs, 32, 1))
        rot = xs * cos + sw * sin
        if s < hw // (2 * LANES):
            rot = rot * dk_scale
        o_ref[:, LANES * s:LANES * (s + 1)] = rot
    o_ref[:, hw:2 * hw] = proj(hw)
    z = proj(2 * hw)
    o_ref[:, 2 * hw:3 * hw] = z * _sigmoid(z)
    z = proj(3 * hw)
    o_ref[:, 3 * hw:4 * hw] = z * _sigmoid(z)
    o_ref[:, 6 * hw:7 * hw] = proj(5 * hw)
    o_ref[:, 7 * hw:8 * hw] = _sigmoid(proj(6 * hw))


def _ab_in(x, modt, g, w_in, cos_t, sin_t, lbc, layer, m, cfg):
    ng, npt, n = cfg["ng"], cfg["npt"], cfg["n"]
    x_args, x_specs = _rows_specs(x, npt)
    d = x_args[0].shape[1]
    nt = n // TM
    zw = 4 * d
    kern = functools.partial(_ab_in_kernel, d=d, npt=npt, dk_scale=float((d // 16) ** -0.5))
    return pl.pallas_call(
        kern,
        grid=(nt,),
        in_specs=x_specs + [
            pl.BlockSpec((1, ng, 1, 6 * d), lambda i: (layer, jnp.maximum(i - npt + cfg["mod_off"], 0), 0, 0)),
            _const_spec((1, 1, d), (layer, 0, 0)),
            _const_spec((1, d, w_in.shape[2]), (m, 0, 0)),
            pl.BlockSpec((TM, LANES), lambda i: (i, 0)),
            pl.BlockSpec((TM, LANES), lambda i: (i, 0)),
            _const_spec((8, d // 2), (m, 0)),
        ],
        out_specs=pl.BlockSpec((TM, zw), lambda i: (i, 0)),
        out_shape=jax.ShapeDtypeStruct((n, zw), F32),
        compiler_params=_params(("arbitrary",)),
        name="ab_in",
    )(*x_args, modt, g, w_in, cos_t, sin_t, lbc)


def _ab_rec_kernel(*refs, d, c, sample):
    if sample:
        (z_ref, dmat_ref, rdec_ref, cdec_ref, gc_ref, tri_ref, hgg_ref, ones_ref, sret_in, shg_in,
         o_ref, sret_out, shg_out, sret_scr, shg_scr) = refs
    else:
        (z_ref, dmat_ref, rdec_ref, cdec_ref, gc_ref, tri_ref, hgg_ref, ones_ref,
         o_ref, sret_out, shg_out, sret_scr, shg_scr) = refs
    hw = d // 2
    rows = z_ref.shape[0]
    n_chunks = rows // c
    dk = d // 16

    if not sample:
        @pl.when(pl.program_id(0) == 0)
        def _():
            sret_scr[...] = jnp.zeros_like(sret_scr)
            shg_scr[...] = jnp.zeros_like(shg_scr)

    lane_c = lax.broadcasted_iota(jnp.int32, (c, LANES), 1)
    row_h = lax.broadcasted_iota(jnp.int32, (LEAF, LANES), 0)
    ones = ones_ref[...]
    tri = tri_ref[...]

    def hg_chunk(r0, b):
        sub = min(SUB, c)
        col = lambda g, h: slice(g * hw + LANES * h, g * hw + LANES * (h + 1))
        units = [(i, h) for i in range(c // sub) for h in range(HG_HEADS)]
        rs = {i: pl.ds(r0 + sub * i, sub) for i in range(c // sub)}
        q = {u: z_ref[rs[u[0]], col(3, u[1])] for u in units}
        v = {u: z_ref[rs[u[0]], col(6, u[1])] for u in units}
        bs = {(i, h): b[sub * i:sub * (i + 1), LANES * h:LANES * (h + 1)] for (i, h) in units}
        bl = {u: bs[u][sub - 1:sub, :] for u in units}
        w = {u: bs[u] - z_ref[rs[u[0]], col(5, u[1])] for u in units}
        nl = sub // LEAF
        r, cross = {}, {}
        for u in units:
            ps = []
            for j in range(sub):
                lf = slice(LEAF * (j // LEAF), LEAF * (j // LEAF + 1))
                ps.append(jnp.where(row_h >= j % LEAF, q[u][lf] * jnp.exp(bs[u][lf] - w[u][j:j + 1, :]), 0.0))
            r[u] = _dot1(jnp.concatenate(ps, axis=0), ones)
            for a in range(1, nl):
                lf = slice(LEAF * a, LEAF * (a + 1))
                mid = bs[u][LEAF * a - 1:LEAF * a, :]
                att = _dot1(q[u][lf] * jnp.exp(bs[u][lf] - mid), jnp.exp(mid - w[u][0:LEAF * a]), _NT)
                cross[u, a] = _dot1(att, v[u][0:LEAF * a])
        upd = {u: _dot1(v[u], jnp.exp(bl[u] - w[u]), _TN) for u in units}
        inner = {}
        for u in units:
            leaves = []
            for a in range(nl):
                acc = cross[u, a] if a else None
                for j in range(LEAF * a, LEAF * (a + 1)):
                    term = r[u][LEAF * j:LEAF * (j + 1)] * v[u][j:j + 1, :]
                    acc = term if acc is None else acc + term
                leaves.append(acc)
            inner[u] = jnp.concatenate(leaves, axis=0)
        st = {h: shg_scr[h] for h in range(HG_HEADS)}
        for u in units:
            i, h = u
            out = inner[u] + _dot1(q[u] * jnp.exp(bs[u]), st[h], _NT)
            st[h] = st[h] * jnp.exp(bl[u]) + upd[u]
            ms = jnp.mean(out * out, axis=-1, keepdims=True)
            o_ref[rs[i], col(1, h)] = out * lax.rsqrt(ms + NORM_EPS) * hgg_ref[...] * z_ref[rs[i], col(7, h)]
        for h in range(HG_HEADS):
            shg_scr[h] = st[h]

    def ret_chunk(r0):
        rs = pl.ds(r0, c)
        heads = range(RET_HEADS)
        zeros = jnp.zeros((dk, LANES), F32)
        qp = [z_ref[rs, LANES * p:LANES * (p + 1)] for p in range(RET_HEADS // 2)]
        kp = [z_ref[rs, hw // 2 + LANES * p:hw // 2 + LANES * (p + 1)] for p in range(RET_HEADS // 2)]
        v = [z_ref[rs, hw + LANES * h:hw + LANES * (h + 1)] for h in heads]
        s = [sret_scr[h] for h in heads]
        qh = [jnp.where((lane_c // dk) == (h % 2), qp[h // 2], 0.0) for h in heads]
        scores = [_dot1(qh[h], kp[h // 2], _NT) * dmat_ref[h] for h in heads]
        spad = [jnp.concatenate([s[h], zeros] if h % 2 == 0 else [zeros, s[h]], axis=0) for h in heads]
        cross = [_dot1(qp[h // 2], spad[h]) * rdec_ref[h] for h in heads]
        u = [_dot1(kp[h // 2] * cdec_ref[h], v[h], _TN) for h in heads]
        inner = [_dot1(scores[h], v[h]) for h in heads]
        for h in heads:
            e = h % 2
            out = inner[h] + cross[h]
            sret_scr[h] = s[h] * gc_ref[h] + u[h][dk * e:dk * (e + 1)]
            ms = jnp.mean(out * out, axis=-1, keepdims=True)
            o_ref[rs, LANES * h:LANES * (h + 1)] = (out * lax.rsqrt(ms + NORM_EPS)
                                                    * z_ref[rs, 2 * hw + LANES * h:2 * hw + LANES * (h + 1)])

    def chunk_body(ci, carry):
        r0 = pl.multiple_of(ci * c, c)
        if sample:
            sret_scr[...] = sret_in[ci]
            shg_scr[...] = shg_in[ci]
        b = _cumsum_rows(tri, z_ref[pl.ds(r0, c), 4 * hw:5 * hw])
        ret_chunk(r0)
        hg_chunk(r0, b)
        if sample:
            sret_out[ci] = sret_scr[...]
            shg_out[ci] = shg_scr[...]
        return carry

    lax.fori_loop(0, n_chunks, chunk_body, 0, unroll=4)

    if not sample:
        @pl.when(pl.program_id(0) == pl.num_programs(0) - 1)
        def _():
            sret_out[...] = sret_scr[...]
            shg_out[...] = shg_scr[...]


def _ret_consts(c):
    heads = jnp.arange(RET_HEADS, dtype=F32)
    log_g = jnp.log1p(-jnp.exp2(-5.0 - heads))
    idx = jnp.arange(c, dtype=F32)
    rel = idx[:, None] - idx[None, :]
    dmat = jnp.exp(jnp.where(rel[None] >= 0, rel[None] * log_g[:, None, None], -jnp.inf))
    rdec = jnp.exp((idx[None, :] + 1.0) * log_g[:, None])
    cdec = jnp.exp((c - 1.0 - idx)[None, :] * log_g[:, None])
    gc = jnp.exp(c * log_g)
    bc = lambda a: jnp.broadcast_to(a[:, :, None], (RET_HEADS, c, LANES))
    return dmat, bc(rdec), bc(cdec), gc


def _ab_rec(z, hg_g, sret, shg_t, cfg, sample, d):
    n = z.shape[0]
    ng, npt = cfg["ng"], cfg["npt"]
    c = cfg["ts"] if sample else CHUNK
    dk, dv = d // 16, d // 8
    dmat, rdec, cdec, gc = _ret_consts(c)
    gcb = jnp.broadcast_to(gc[:, None, None], (RET_HEADS, dk, LANES))
    sub = min(SUB, c)
    tri = np.kron(np.eye(c // sub), np.tril(np.ones((sub, sub))))
    tri = jnp.asarray(tri, BF16)
    ones = jnp.ones((LANES, LANES), BF16)
    kern = functools.partial(_ab_rec_kernel, d=d, c=c, sample=sample)
    consts = [dmat, rdec, cdec, gcb, tri, hg_g.reshape(1, dv), ones]
    const_specs = [
        _const_spec(dmat.shape, (0, 0, 0)), _const_spec(rdec.shape, (0, 0, 0)),
        _const_spec(cdec.shape, (0, 0, 0)), _const_spec(gcb.shape, (0, 0, 0)),
        _const_spec(tri.shape, (0, 0)), _const_spec((1, dv), (0, 0)), _const_spec(ones.shape, (0, 0)),
    ]
    scratch = [pltpu.VMEM((RET_HEADS, dk, dv), F32), pltpu.VMEM((HG_HEADS, dv, dv), F32)]
    if sample:
        nb = sret.shape[0]
        nst = nb // ng
        return pl.pallas_call(
            kern,
            grid=(nst,),
            in_specs=[pl.BlockSpec((TM, z.shape[1]), lambda i: (npt + i, 0))] + const_specs + [
                pl.BlockSpec((ng, RET_HEADS, dk, dv), lambda i: (i, 0, 0, 0)),
                pl.BlockSpec((ng, HG_HEADS, dv, dv), lambda i: (i, 0, 0, 0)),
            ],
            out_specs=[
                pl.BlockSpec((TM, d), lambda i: (i, 0)),
                pl.BlockSpec((ng, RET_HEADS, dk, dv), lambda i: (i, 0, 0, 0)),
                pl.BlockSpec((ng, HG_HEADS, dv, dv), lambda i: (i, 0, 0, 0)),
            ],
            out_shape=[
                jax.ShapeDtypeStruct((nst * TM, d), F32),
                jax.ShapeDtypeStruct((nb, RET_HEADS, dk, dv), F32),
                jax.ShapeDtypeStruct((nb, HG_HEADS, dv, dv), F32),
            ],
            scratch_shapes=scratch,
            compiler_params=_params(("arbitrary",)),
            name="ab_rec_sample",
        )(z, *consts, sret, shg_t)
    return pl.pallas_call(
        kern,
        grid=(npt,),
        in_specs=[pl.BlockSpec((TM, z.shape[1]), lambda i: (i, 0))] + const_specs,
        out_specs=[
            pl.BlockSpec((TM, d), lambda i: (i, 0)),
            pl.BlockSpec((RET_HEADS, dk, dv), lambda i: (0, 0, 0)),
            pl.BlockSpec((HG_HEADS, dv, dv), lambda i: (0, 0, 0)),
        ],
        out_shape=[
            jax.ShapeDtypeStruct((npt * TM, d), F32),
            jax.ShapeDtypeStruct((RET_HEADS, dk, dv), F32),
            jax.ShapeDtypeStruct((HG_HEADS, dv, dv), F32),
        ],
        scratch_shapes=scratch,
        compiler_params=_params(("arbitrary",)),
        name="ab_rec_prompt",
    )(z, *consts)


def _post_kernel(*refs, d, npt, nx, rwkv, final):
    x = _rows_value(refs[:nx], npt)
    it = iter(refs[nx:])
    if rwkv:
        yp_ref, ys_ref, bonus_ref, gate_ref, lng_ref, lnb_ref, seg_ref = (next(it) for _ in range(7))
    else:
        op_ref, os_ref = next(it), next(it)
    mod_ref, wout_ref, g2_ref, w1_ref, w2_ref = (next(it) for _ in range(5))
    if final:
        fg_ref = next(it)
        outp_ref, outs_ref = next(it), next(it)
    else:
        out_ref = next(it)

    i = pl.program_id(0)
    if rwkv:
        s_i = jnp.where(i >= npt, 1, 0)
        is_sample = (lax.broadcasted_iota(jnp.int32, (x.shape[0], LANES), 0) * 0 + s_i) == 1
        y = jnp.concatenate([jnp.where(is_sample, ys_ref[p], yp_ref[p]) for p in range(yp_ref.shape[0])],
                            axis=-1)
        seg = seg_ref[...]
        inv_n = 1.0 / RW_N
        mean = _segsum(y, seg) * inv_n
        yc = y - mean
        var = _segsum(yc * yc, seg) * inv_n
        o = yc * lax.rsqrt(var + RW_LN_EPS) * lng_ref[0] + lnb_ref[0]
        o = (o + bonus_ref[...]) * gate_ref[...]
    else:
        o = _rows_value((op_ref, os_ref), npt)
    gt1 = mod_ref[0, :, :, 2 * d:3 * d]
    sh2 = mod_ref[0, :, :, 3 * d:4 * d]
    sc2 = mod_ref[0, :, :, 4 * d:5 * d]
    gt2 = mod_ref[0, :, :, 5 * d:6 * d]
    x1 = _gate_add(x, gt1, _dg(o.astype(BF16), wout_ref[0], _NN))
    hb = _modulate(_rms(x1, g2_ref[0]), sc2, sh2).astype(BF16)
    ff = w1_ref.shape[2]
    acc = jnp.zeros_like(x1)
    for j in range(ff // d):
        u = jnp.maximum(_dg(hb, w1_ref[0, :, j * d:(j + 1) * d], _NN), 0.0)
        acc = acc + _dg((u * u).astype(BF16), w2_ref[0, j * d:(j + 1) * d, :], _NN)
    x2 = _gate_add(x1, gt2, acc)
    if final:
        x2 = _rms(x2, fg_ref[...])

        @pl.when(i < npt)
        def _():
            outp_ref[...] = x2

        @pl.when(i >= npt)
        def _():
            outs_ref[...] = x2
    else:
        out_ref[...] = x2


def _post(x, mixer_in, modt, wout, g2, w1, w2, final_g, layer, m, cfg, rwkv, final):
    ng, npt, n = cfg["ng"], cfg["npt"], cfg["n"]
    args, specs = _rows_specs(x, npt)
    d = args[0].shape[1]
    nt = n // TM
    kern = functools.partial(_post_kernel, d=d, npt=npt, nx=len(args), rwkv=rwkv, final=final)
    tok = pl.BlockSpec((TM, d), lambda i: (i, 0))
    tok_p = pl.BlockSpec((TM, d), lambda i: (jnp.minimum(i, npt - 1), 0))
    tok_s = pl.BlockSpec((TM, d), lambda i: (jnp.maximum(i - npt, 0), 0))
    if rwkv:
        y_p, y_s, bonus, gate, lng, lnb, seg = mixer_in
        args += [y_p, y_s, bonus, gate, lng, lnb, seg]
        specs += [pl.BlockSpec((y_p.shape[0], TM, LANES), lambda i: (0, jnp.minimum(i, npt - 1), 0)),
                  pl.BlockSpec((y_s.shape[0], TM, LANES), lambda i: (0, jnp.maximum(i - npt, 0), 0)), tok, tok,
                  _const_spec((1, 1, d), (m, 0, 0)), _const_spec((1, 1, d), (m, 0, 0)),
                  _const_spec((d, LANES), (0, 0))]
    else:
        args += list(mixer_in)
        specs += [tok_p, tok_s]
    args += [modt, wout, g2, w1, w2]
    specs += [
        pl.BlockSpec((1, ng, 1, 6 * d), lambda i: (layer, jnp.maximum(i - npt + cfg["mod_off"], 0), 0, 0)),
        _const_spec((1, d, d), (m, 0, 0)),
        _const_spec((1, 1, d), (layer, 0, 0)),
        _const_spec((1, d, w1.shape[2]), (layer, 0, 0)),
        _const_spec((1, w2.shape[1], d), (layer, 0, 0)),
    ]
    if final:
        args += [final_g]
        specs += [_const_spec((1, d), (0, 0))]
    if final:
        out_specs = [tok_p, tok_s]
        out_shape = [jax.ShapeDtypeStruct((npt * TM, d), F32), jax.ShapeDtypeStruct((n - npt * TM, d), F32)]
    else:
        out_specs = tok
        out_shape = jax.ShapeDtypeStruct((n, d), F32)
    return pl.pallas_call(
        kern,
        grid=(nt,),
        in_specs=specs,
        out_specs=out_specs,
        out_shape=out_shape,
        compiler_params=_params(("arbitrary",)),
        name="post_rwkv" if rwkv else "post_ab",
    )(*args)


def _rw_in_kernel(*refs, d, ts, npt, vres):
    it = iter(refs)
    x_ref, mod_ref, g_ref, shift_ref, mu_ref, wrkv_ref = (next(it) for _ in range(6))
    w0_ref, w1_ref, w2_ref, a0_ref, a1_ref, a2_ref, g1_ref, g2_ref = (next(it) for _ in range(8))
    kk_ref, ka_ref, rk_ref, seg_ref = (next(it) for _ in range(4))
    if vres:
        vf_ref, v0_ref, v1_ref, v2_ref = (next(it) for _ in range(4))
    r_out, ld_out, k_out, v_out, kk_out, kka_out, gate_out, bonus_out, hlast_out = (next(it) for _ in range(9))
    if not vres:
        vtok_out = next(it)
    h_scr, carry_scr = next(it), next(it)

    i = pl.program_id(0)
    rows = x_ref.shape[0]
    ng = rows // ts

    @pl.when(i == 0)
    def _():
        carry_scr[...] = jnp.zeros_like(carry_scr)

    x = x_ref[...]
    h = _modulate(_rms(x, g_ref[0]), mod_ref[0, :, :, d:2 * d], mod_ref[0, :, :, 0:d])
    for p in range(d // LANES):
        h_scr[p] = h[:, LANES * p:LANES * (p + 1)]
        hlast_out[:, LANES * p:LANES * (p + 1)] = h_scr[p, pl.ds(ts - 1, ng, stride=ts), :]
    rolled = pltpu.roll(h, 1, 0)
    row = lax.broadcasted_iota(jnp.int32, (rows, d), 0)
    s_i = jnp.where(i >= npt, 1, 0)
    first = ((row % ts) == 0) & ((row * (1 - s_i)) == 0)
    seq_prev = jnp.broadcast_to(shift_ref[...], (ng, ts, d)).reshape(rows, d)
    prev_first = jnp.where((row * 0 + s_i) == 1, seq_prev, jnp.broadcast_to(carry_scr[0:1, :], (rows, d)))
    prev = jnp.where(first, prev_first, rolled)
    carry_scr[0:1, :] = h[rows - 1:rows, :]

    seg = seg_ref[...]

    def block(b0):
        rb = slice(b0, b0 + ROW_BLOCK)
        hb = h[rb]
        xx = prev[rb] - hb
        mix = lambda j: (hb + xx * mu_ref[0, j:j + 1, :]).astype(BF16)
        xr, xw, xk, xv, xa, xg = (mix(j) for j in range(6))
        yield
        w_dn = _dg(xw, w1_ref[0], _NN)
        a_dn = _dg(xa, a1_ref[0], _NN)
        g_dn = _dg(xg, g1_ref[0], _NN)
        if vres:
            v_dn = _dg(xv, v1_ref[0], _NN)
        k = _dg(xk, wrkv_ref[0, 1], _NN)
        yield
        v = _dg(xv, wrkv_ref[0, 2], _NN)
        kk = k * kk_ref[0]
        yield
        r = _dg(xr, wrkv_ref[0, 0], _NN)
        yield
        kk_ss = _segsum(kk * kk, seg)
        wl = w0_ref[0] + _dot1(jnp.tanh(w_dn), w2_ref[0])
        a = _sigmoid(a0_ref[0] + _dot1(a_dn, a2_ref[0]))
        gate_out[rb, :] = _dot1(_sigmoid(g_dn), g2_ref[0])
        if vres:
            lv = v0_ref[0] + _dot1(v_dn, v2_ref[0])
            v = v + (vf_ref[rb, :] - v) * _sigmoid(lv)
        else:
            vtok_out[rb, :] = v
        yield
        nwl = -wl
        w = -(jnp.maximum(nwl, 0.0) + jnp.log1p(jnp.exp(-jnp.abs(nwl)))) - 0.5
        ld = -jnp.exp(w)
        kk = kk / jnp.maximum(jnp.sqrt(kk_ss), 1e-12)
        km = k * (1.0 + (a - 1.0) * ka_ref[0])
        bonus_out[rb, :] = _segsum(r * km * rk_ref[0], seg) * v
        kka = kk * a
        for p in range(d // LANES):
            cs = slice(LANES * p, LANES * (p + 1))
            r_out[p, rb, :] = r[:, cs]
            ld_out[p, rb, :] = ld[:, cs]
            k_out[p, rb, :] = km[:, cs]
            v_out[p, rb, :] = v[:, cs]
            kk_out[p, rb, :] = kk[:, cs]
            kka_out[p, rb, :] = kka[:, cs]
        yield

    live = []
    starts = list(range(0, rows, ROW_BLOCK))
    while starts or live:
        if starts:
            live.append(block(starts.pop(0)))
        for g in list(live):
            try:
                next(g)
            except StopIteration:
                live.remove(g)


def _rw_in(x, modt, g, shift_rows, W, seg, v_first, layer, m, cfg):
    n, d = x.shape
    ts, tm = cfg["ts"], TM_IN
    ng, npt = tm // ts, cfg["npt"] * TM // tm
    nt = n // tm
    npair = d // LANES
    vres = v_first is not None
    kern = functools.partial(_rw_in_kernel, d=d, ts=ts, npt=npt, vres=vres)
    tok = pl.BlockSpec((tm, d), lambda i: (i, 0))
    vec = lambda: _const_spec((1, 1, d), (m, 0, 0))
    lora = lambda a: _const_spec((1,) + a.shape[1:], (m, 0, 0))
    args = [x, modt, g, shift_rows, W["rw_mu"], W["rw_w_rkv"],
            W["rw_w0"], W["rw_w1"], W["rw_w2"], W["rw_a0"], W["rw_a1"], W["rw_a2"], W["rw_g1"], W["rw_g2"],
            W["rw_k_k"], W["rw_k_a"], W["rw_r_k"], seg]
    specs = [
        tok,
        pl.BlockSpec((1, ng, 1, 2 * d), lambda i: (layer, jnp.maximum(i - npt + 1, 0), 0, 0)),
        _const_spec((1, 1, d), (layer, 0, 0)),
        pl.BlockSpec((ng, 1, d), lambda i: (jnp.maximum(i - npt, 0), 0, 0)),
        _const_spec((1, 6, d), (m, 0, 0)),
        _const_spec((1, 3, d, d), (m, 0, 0, 0)),
        vec(), lora(W["rw_w1"]), lora(W["rw_w2"]), vec(), lora(W["rw_a1"]), lora(W["rw_a2"]),
        lora(W["rw_g1"]), lora(W["rw_g2"]), vec(), vec(), vec(), _const_spec((d, LANES), (0, 0)),
    ]
    if vres:
        args += [v_first, W["rw_v0"], W["rw_v1"], W["rw_v2"]]
        specs += [tok, _const_spec((1, 1, d), (m - 1, 0, 0)),
                  _const_spec((1,) + W["rw_v1"].shape[1:], (m - 1, 0, 0)),
                  _const_spec((1,) + W["rw_v2"].shape[1:], (m - 1, 0, 0))]
    pm = pl.BlockSpec((npair, tm, LANES), lambda i: (0, i, 0))
    pm_shape = jax.ShapeDtypeStruct((npair, n, LANES), F32)
    tok_shape = jax.ShapeDtypeStruct((n, d), F32)
    out_specs = [pm] * 6 + [tok, tok, pl.BlockSpec((ng, d), lambda i: (i, 0))]
    out_shape = [pm_shape] * 6 + [tok_shape, tok_shape, jax.ShapeDtypeStruct((nt * ng, d), F32)]
    if not vres:
        out_specs.append(tok)
        out_shape.append(tok_shape)
    return pl.pallas_call(
        kern,
        grid=(nt,),
        in_specs=specs,
        out_specs=out_specs,
        out_shape=out_shape,
        scratch_shapes=[pltpu.VMEM((npair, tm, LANES), F32), pltpu.VMEM((8, d), F32)],
        compiler_params=pltpu.CompilerParams(dimension_semantics=("arbitrary",), vmem_limit_bytes=VMEM_LIMIT_IN),
        name="rw_in",
    )(*args)


def _rw_rec_kernel(*refs, c, sample):
    if sample:
        (r_ref, ld_ref, k_ref, v_ref, kk_ref, kka_ref, tri_ref, ms_ref, mi_ref, eye_ref, lvl_ref, s_in,
         y_ref, s_out, s_scr) = refs
    else:
        (r_ref, ld_ref, k_ref, v_ref, kk_ref, kka_ref, tri_ref, ms_ref, mi_ref, eye_ref, lvl_ref,
         y_ref, s_out, s_scr) = refs
    npair, rows, _ = r_ref.shape
    n_chunks = rows // c
    n = 2 * c

    if not sample:
        @pl.when(pl.program_id(0) == 0)
        def _():
            s_scr[...] = jnp.zeros_like(s_scr)

    left_k = lax.broadcasted_iota(jnp.int32, (c, LANES), 1) < RW_N
    left_t = lax.broadcasted_iota(jnp.int32, (c, n), 1) < c
    tri = tri_ref[...]
    mask_strict = ms_ref[...]
    mask_incl = mi_ref[...]
    eye = eye_ref[...]
    levels = lvl_ref.shape[0]

    def stack(xv, left):
        return jnp.concatenate([jnp.where(left, xv, 0.0), jnp.where(left, 0.0, xv)], axis=0)

    stack_k = lambda xv: stack(xv, left_k)
    stack_t = lambda xv: stack(xv, left_t)
    pairs = range(npair)

    nu = min(RW_GROUP_SAMPLE if sample else RW_GROUP, n_chunks)
    wave = nu if sample else RW_WAVE

    def group_body(gi, carry):
        ci = {j: gi * nu + j for j in range(nu)}
        rs = {j: pl.ds(pl.multiple_of(ci[j] * c, c), c) for j in range(nu)}
        ur, bk_end, vst, ltot, a_r, t, mv = {}, {}, {}, {}, {}, {}, {}
        state = {"s": None if sample else [s_scr[p] for p in pairs]}

        def head(js):
            units = [(j, p) for j in js for p in pairs]
            bk = {}
            for j in js:
                ld_all = jnp.concatenate([ld_ref[p, rs[j], :] for p in pairs], axis=1)
                lc_all = _cumsum_rows(tri, ld_all)
                for p in pairs:
                    lc = lc_all[:, LANES * p:LANES * (p + 1)]
                    ld = ld_all[:, LANES * p:LANES * (p + 1)]
                    lt = lc[c - 1:c, :]
                    g_inv = jnp.exp(-lc)
                    g_end = jnp.exp(lt - lc)
                    k = k_ref[p, rs[j], :]
                    kka = kka_ref[p, rs[j], :]
                    ur[j, p] = jnp.concatenate([-kk_ref[p, rs[j], :] * jnp.exp(lc - ld),
                                                r_ref[p, rs[j], :] * jnp.exp(lc)], axis=0)
                    bk[j, p] = jnp.concatenate([stack_k(kka * g_inv), stack_k(k * g_inv)], axis=0)
                    bk_end[j, p] = jnp.concatenate([stack_k(kka * g_end), stack_k(k * g_end)], axis=0)
                    vst[j, p] = stack_k(v_ref[p, rs[j], :])
                    ltot[j, p] = lt
                yield
            big = {u: _dotp(ur[u], bk[u], _NT, P_BIG) for u in units}
            m_ub = {u: big[u][0:c, 0:n] * mask_strict for u in units}
            m_uk = {u: big[u][0:c, n:2 * n] * mask_strict for u in units}
            for u in units:
                a_r[u] = jnp.concatenate([big[u][c:n, 0:n] * mask_incl, big[u][c:n, n:2 * n] * mask_incl], axis=1)
            for u in units:
                mv[u] = _dotp(m_uk[u], vst[u], _NN, P_ST)
            yield
            tt = {u: eye + m_ub[u] * lvl_ref[0] for u in units}
            for lv in range(1, levels):
                w = {u: _dot1(m_ub[u] * lvl_ref[lv], stack_t(tt[u])) for u in units}
                yield
                tt = {u: tt[u] + _dot1(tt[u], stack_t(w[u])) for u in units}
                yield
            t.update(tt)

        def tail_independent(js):
            units = [(j, p) for j in js for p in pairs]
            zero = jnp.zeros((RW_N, RW_N), F32)
            s0 = {(j, p): jnp.concatenate([jnp.concatenate([s_in[ci[j], p, 0], zero], axis=1),
                                           jnp.concatenate([zero, s_in[ci[j], p, 1]], axis=1)], axis=0)
                  for (j, p) in units}
            urs = {u: _dotp(ur[u], s0[u], _NT, P_ST) for u in units}
            yield
            e = {u: _dotp(t[u], stack_k(urs[u][0:c] + mv[u]), _NN, P_ST) for u in units}
            yield
            ev = {u: jnp.concatenate([stack_k(e[u]), vst[u]], axis=0) for u in units}
            upd = {u: _dotp(ev[u], bk_end[u], _TN, P_ST) for u in units}
            yield
            for (j, p) in units:
                y_ref[p, rs[j], :] = urs[j, p][c:n] + _dotp(a_r[j, p], ev[j, p], _NN, P_ST)
                s_new = s0[j, p] * jnp.exp(ltot[j, p]) + upd[j, p]
                s_out[ci[j], p, 0] = s_new[0:RW_N, 0:RW_N]
                s_out[ci[j], p, 1] = s_new[RW_N:, RW_N:]
            yield

        def tail(js):
            if sample:
                yield from tail_independent(js)
                return
            emit_y = None
            for j in js:
                sj = state["s"]
                urs = [_dotp(ur[j, p], sj[p], _NT, P_ST) for p in pairs]
                if emit_y is not None:
                    emit_y()
                yield
                e = [_dotp(t[j, p], stack_k(urs[p][0:c] + mv[j, p]), _NN, P_ST) for p in pairs]
                yield
                ev = [jnp.concatenate([stack_k(e[p]), vst[j, p]], axis=0) for p in pairs]
                upd = [_dotp(ev[p], bk_end[j, p], _TN, P_ST) for p in pairs]
                state["s"] = s_new = [sj[p] * jnp.exp(ltot[j, p]) + upd[p] for p in pairs]
                yield

                def emit_y(j=j, urs=urs, ev=ev):
                    for p in pairs:
                        y_ref[p, rs[j], :] = urs[p][c:n] + _dotp(a_r[j, p], ev[p], _NN, P_ST)
            emit_y()
            yield

        def run(*gens):
            live = list(gens)
            while live:
                for g in list(live):
                    try:
                        next(g)
                    except StopIteration:
                        live.remove(g)

        waves = [list(range(w, min(w + wave, nu))) for w in range(0, nu, wave)]
        run(head(waves[0]))
        for w_prev, w_next in zip(waves[:-1], waves[1:]):
            run(head(w_next), tail(w_prev))
        run(tail(waves[-1]))
        if not sample:
            for p in pairs:
                s_scr[p] = state["s"][p]
        return carry

    lax.fori_loop(0, n_chunks // nu, group_body, 0)

    if not sample:
        @pl.when(pl.program_id(0) == pl.num_programs(0) - 1)
        def _():
            s_out[...] = s_scr[...]


def _rw_rec(seqs, s_bd, cfg, sample):
    npair, n, _ = seqs[0].shape
    tm = TM if sample else TM_RW
    ng, npt = tm // cfg["ts"], cfg["npt"] * TM // tm
    c = cfg["ts"] if sample else CHUNK
    tri = jnp.asarray(np.tril(np.ones((c, c))), BF16)
    side = lambda a: jnp.asarray(np.concatenate([a, a], axis=1), F32)
    mask_incl = side(np.tril(np.ones((c, c))))
    mask_strict = side(np.tril(np.ones((c, c)), -1))
    eye = side(np.eye(c))
    tt, ss = np.meshgrid(np.arange(c), np.arange(c), indexing="ij")
    lvl = jnp.stack([side(((tt // b == ss // b) & (tt % b >= b // 2) & (ss % b < b // 2)).astype(np.float32))
                     for b in (2 ** e for e in range(1, int(np.log2(c)) + 1))])
    consts = [tri, mask_strict, mask_incl, eye, lvl]
    const_specs = [_const_spec(a.shape, (0,) * a.ndim) for a in consts]
    kern = functools.partial(_rw_rec_kernel, c=c, sample=sample)
    scratch = [pltpu.VMEM((npair, LANES, LANES), F32)]
    tokp = pl.BlockSpec((npair, tm, LANES), lambda i: (0, i, 0))
    if sample:
        nb = s_bd.shape[0]
        nst = nb // ng
        tok_in = pl.BlockSpec((npair, tm, LANES), lambda i: (0, npt + i, 0))
        sspec = pl.BlockSpec((ng, npair, 2, RW_N, RW_N), lambda i: (i, 0, 0, 0, 0))
        return pl.pallas_call(
            kern,
            grid=(nst,),
            in_specs=[tok_in] * 6 + const_specs + [sspec],
            out_specs=[tokp, sspec],
            out_shape=[jax.ShapeDtypeStruct((npair, nst * tm, LANES), F32),
                       jax.ShapeDtypeStruct((nb, npair, 2, RW_N, RW_N), F32)],
            scratch_shapes=scratch,
            compiler_params=_params(("arbitrary",)),
            name="rw_rec_sample",
        )(*seqs, *consts, s_bd)
    return pl.pallas_call(
        kern,
        grid=(npt,),
        in_specs=[tokp] * 6 + const_specs,
        out_specs=[tokp, pl.BlockSpec((npair, LANES, LANES), lambda i: (0, 0, 0))],
        out_shape=[jax.ShapeDtypeStruct((npair, npt * tm, LANES), F32),
                   jax.ShapeDtypeStruct((npair, LANES, LANES), F32)],
        scratch_shapes=scratch,
        compiler_params=_params(("arbitrary",)),
        name="rw_rec_prompt",
    )(*seqs, *consts)


def kernel(x_prompt, x_sample, state_ret, state_hgrn, state_wkv, state_shift, c_prompt, c_sample, mod_w, mod_b, norm_mix_g, norm_mlp_g, final_g, mlp_w1, mlp_w2, ab_w_in, ab_w_out, hg_lb, hg_norm_g, rw_mu, rw_w_rkv, rw_w0, rw_w1, rw_w2, rw_a0, rw_a1, rw_a2, rw_v0, rw_v1, rw_v2, rw_g1, rw_g2, rw_k_k, rw_k_a, rw_r_k, rw_ln_g, rw_ln_b, rw_w_out):
    bp, tp, d = x_prompt.shape
    bs, ts, _ = x_sample.shape
    depth = mod_w.shape[0]
    n_ab = ab_w_in.shape[0]
    n_c = rw_w_rkv.shape[0]
    assert bp == 1 and d == 1024 and tp % TM == 0 and tp % CHUNK == 0
    assert (bs * ts) % TM == 0 and TM % ts == 0 and ts % LEAF == 0 and ts <= CHUNK
    np_rows, ns_rows = bp * tp, bs * ts
    n = np_rows + ns_rows
    ng = TM // ts
    assert bs % ng == 0
    assert np_rows % TM_RW == 0 and ns_rows % TM_RW == 0 and TM_RW % TM == 0
    assert np_rows % TM_IN == 0 and ns_rows % TM_IN == 0 and TM_IN % TM == 0 and bs % (TM_IN // ts) == 0
    ng_max = TM_IN // ts
    cfg = dict(ng=ng, npt=np_rows // TM, ts=ts, n=n, mod_off=ng_max // ng)
    dk, dv = d // 16, d // 8
    npair = d // LANES
    nh = d // RW_N

    c_all = jnp.concatenate([c_prompt, c_sample], axis=0)
    pad = (-c_all.shape[0]) % 8
    c_all = jnp.pad(c_all, ((0, pad), (0, 0)))
    mod = _modulation(c_all, mod_w, mod_b)
    modt = jnp.concatenate([jnp.broadcast_to(mod[:, 0:1], (depth, ng_max, 6 * d)), mod[:, bp:bp + bs]], axis=1)
    modt = modt.reshape(depth, ng_max + bs, 1, 6 * d)

    x = (x_prompt.reshape(np_rows, d), x_sample.reshape(ns_rows, d))

    pos = jnp.concatenate([jnp.arange(tp, dtype=F32), jnp.tile(PAST_LEN + jnp.arange(ts, dtype=F32), bs)])
    half = dk // 2
    inv = ROPE_BASE ** (-jnp.arange(half, dtype=F32) / half)
    ang = pos[:, None] * inv[None, :]
    cos_t = jnp.tile(jnp.cos(ang), (1, 4))
    sin_t = jnp.tile(jnp.concatenate([-jnp.sin(ang), jnp.sin(ang)], axis=1), (1, 2))

    lb_all = jnp.cumsum(jax.nn.softmax(hg_lb.astype(F32), axis=0), axis=0)
    lb_all = lb_all - lb_all[:1]
    lbc = jnp.stack([jnp.log(lb_all), jnp.log1p(-lb_all), 1.0 - lb_all], axis=1)
    lbc = jnp.pad(lbc, ((0, 0), (0, 5), (0, 0))).reshape(n_ab * 8, d // 2)

    bf = lambda a: a.astype(BF16)
    ab_w_in_b, ab_w_out_b = bf(ab_w_in), bf(ab_w_out)
    mlp_w1_b, mlp_w2_b = bf(mlp_w1), bf(mlp_w2)
    rw_w_out_b = bf(rw_w_out)
    vec = lambda a: a.reshape(a.shape[0], 1, d)
    W = dict(rw_mu=rw_mu, rw_w_rkv=bf(rw_w_rkv), rw_w0=vec(rw_w0), rw_w1=bf(rw_w1), rw_w2=bf(rw_w2),
             rw_a0=vec(rw_a0), rw_a1=bf(rw_a1), rw_a2=bf(rw_a2), rw_g1=bf(rw_g1), rw_g2=bf(rw_g2),
             rw_k_k=vec(rw_k_k), rw_k_a=vec(rw_k_a), rw_r_k=vec(rw_r_k), rw_v0=vec(rw_v0), rw_v1=bf(rw_v1),
             rw_v2=bf(rw_v2))
    seg = jnp.asarray(np.pad(np.kron(np.eye(nh), np.ones((RW_N, 1))), ((0, 0), (0, LANES - nh))), BF16)
    g_mix = norm_mix_g.reshape(depth, 1, d)
    g_mlp = norm_mlp_g.reshape(depth, 1, d)
    fin_g = final_g.reshape(1, d)

    ret_p, ret_s, hg_p, hg_s, wkv_p, wkv_s, sh_p, sh_s = ([] for _ in range(8))
    v_first = None
    for layer in range(depth):
        m = layer // 2
        final = layer == depth - 1
        if layer % 2 == 0:
            z = _ab_in(x, modt, g_mix, ab_w_in_b, cos_t, sin_t, lbc, layer, m, cfg)
            o_p, r_p, h_p = _ab_rec(z, hg_norm_g[m], None, None, cfg, False, d)
            o_s, r_s, h_s = _ab_rec(z, hg_norm_g[m], state_ret[m], jnp.swapaxes(state_hgrn[m], -1, -2), cfg, True, d)
            ret_p.append(r_p[None])
            hg_p.append(jnp.swapaxes(h_p, -1, -2)[None])
            ret_s.append(r_s)
            hg_s.append(jnp.swapaxes(h_s, -1, -2))
            x = _post(x, (o_p, o_s), modt, ab_w_out_b, g_mlp, mlp_w1_b, mlp_w2_b, fin_g, layer, m, cfg, False, final)
        else:
            shift_rows = state_shift[m].reshape(bs, 1, d)
            outs = _rw_in(x, modt, g_mix, shift_rows, W, seg, v_first, layer, m, cfg)
            seqs, gate, bonus, hlast = outs[0:6], outs[6], outs[7], outs[8]
            if v_first is None:
                v_first = outs[9]
            y_p, s_p = _rw_rec(seqs, None, cfg, False)
            y_s, s_s = _rw_rec(seqs, state_wkv[m].reshape(bs, npair, 2, RW_N, RW_N), cfg, True)
            unbd = lambda a: jnp.stack([a[..., :RW_N, :RW_N], a[..., RW_N:, RW_N:]], axis=-3)
            wkv_p.append(unbd(s_p).reshape(1, 1, nh, RW_N, RW_N))
            wkv_s.append(s_s.reshape(bs, nh, RW_N, RW_N))
            sh_p.append(hlast[cfg["npt"] * ng - 1][None, None])
            sh_s.append(hlast[cfg["npt"] * ng:][None])
            x = _post(x, (y_p, y_s, bonus, gate, vec(rw_ln_g), vec(rw_ln_b), seg), modt, rw_w_out_b, g_mlp,
                      mlp_w1_b, mlp_w2_b, fin_g, layer, m, cfg, True, final)

    out_p, out_s = x
    return (out_p.reshape(bp, tp, d), out_s.reshape(bs, ts, d),
            jnp.stack(ret_p), jnp.stack(ret_s), jnp.stack(hg_p), jnp.stack(hg_s),
            jnp.concatenate(wkv_p, axis=0), jnp.stack(wkv_s),
            jnp.concatenate(sh_p, axis=0), jnp.concatenate(sh_s, axis=0))
```
